```python
import math
import jax, jax.numpy as jnp
from jax import lax
import numpy as np

D_MODEL = 1024
BATCH = 2
SEQ = 16384
DEPTH = 1
DEC_BATCH = 128
DEC_SEQ = 4
PAST_LEN = 8192
PAGE_SIZE = 128

H_A = 4
DK_A = 128
DV_A = 128
CONV_W = 4
CHUNK = 64
C_CONV = H_A * (2 * DK_A + DV_A)
H_B = 4
DQK_B = 64
DV_B = 2 * DQK_B
Q_BLOCK = 128
NUM_BUCKETS = 32
MAX_DISTANCE = 128
D_FF = 2816
EPS = 1e-6
NEG_INF = -1e30
COLS = (("qkv_a", C_CONV), ("z_a", H_A * DV_A), ("beta_a", H_A), ("alpha_a", H_A),
        ("q_b", H_B * 2 * DQK_B), ("k_b", H_B * 2 * DQK_B), ("v_b", H_B * DV_B),
        ("gate_a", D_MODEL), ("gate_b", D_MODEL))
D_IN = sum(n for _, n in COLS)

kernel_name = "hybrid_gdn_diffattn_decode_step"

F32 = jnp.float32


def rmsnorm(x, g):
    xf = x.astype(F32)
    y = xf * lax.rsqrt(jnp.mean(xf * xf, axis=-1, keepdims=True) + EPS)
    return (y * g.astype(F32)).astype(x.dtype)


def swiglu(x, w_gate, w_up, w_down):
    return (jax.nn.silu(x @ w_gate) * (x @ w_up)) @ w_down


def l2norm(x):
    xf = x.astype(F32)
    return xf * lax.rsqrt(jnp.sum(xf * xf, axis=-1, keepdims=True) + EPS)


def split_cols(p):
    out, off = {}, 0
    for name, n in COLS:
        out[name] = p[..., off:off + n]
        off += n
    return out


def lambda_init(layer):
    return 0.8 - 0.6 * math.exp(-0.3 * layer)


def causal_conv(u, buf, w):
    up = jnp.concatenate([buf.astype(u.dtype), u], axis=1)
    L = u.shape[1]
    y = sum(up[:, j:j + L] * w[j] for j in range(CONV_W))
    return y, up[:, up.shape[1] - (CONV_W - 1):]


def to_chunks(t, n, C):
    B, _, H = t.shape[:3]
    t = t.reshape((B, n, C, H) + t.shape[3:])
    perm = (1, 0, 3, 2) + tuple(range(4, t.ndim))
    return t.transpose(perm)


def gated_delta_chunked(q, k, v, g, beta, s0):
    B, L, H, DK = q.shape
    C = min(CHUNK, L)
    pad = (-L) % C
    if pad:
        padf = lambda t: jnp.pad(t, [(0, 0), (0, pad)] + [(0, 0)] * (t.ndim - 2))
        q, k, v, g, beta = padf(q), padf(k), padf(v), padf(g), padf(beta)
    n = (L + pad) // C
    qc = to_chunks(q * (DK ** -0.5), n, C)
    kc = to_chunks(k, n, C)
    vc = to_chunks(v, n, C)
    bc = to_chunks(beta, n, C)
    gcum = jnp.cumsum(to_chunks(g, n, C), axis=-1)
    idx = jnp.arange(C)
    causal = idx[:, None] >= idx[None, :]
    strict = idx[:, None] > idx[None, :]
    decay = jnp.exp(jnp.where(causal, gcum[..., :, None] - gcum[..., None, :], -jnp.inf))
    kb = kc * bc[..., None]
    A = jnp.where(strict, jnp.einsum("nbhik,nbhjk->nbhij", kb, kc) * decay, 0.0)
    eye = jnp.broadcast_to(jnp.eye(C, dtype=F32), A.shape)
    T = lax.linalg.triangular_solve(eye + A, eye, left_side=True, lower=True)
    u = T @ (vc * bc[..., None])
    w = T @ (kb * jnp.exp(gcum)[..., None])
    qk = jnp.einsum("nbhik,nbhjk->nbhij", qc, kc) * decay

    def step(S, inp):
        q_i, k_i, u_i, w_i, qk_i, g_i = inp
        v_new = u_i - w_i @ S
        o = (q_i * jnp.exp(g_i)[..., None]) @ S + qk_i @ v_new
        g_last = g_i[..., -1]
        k_dec = k_i * jnp.exp(g_last[..., None] - g_i)[..., None]
        S = S * jnp.exp(g_last)[..., None, None] + jnp.einsum("bhck,bhcv->bhkv", k_dec, v_new)
        return S, o

    S, o = lax.scan(step, s0, (qc, kc, u, w, qk, gcum))
    o = o.transpose(1, 0, 3, 2, 4).reshape(B, n * C, H, -1)[:, :L]
    return o, S


def gdn_branch(qkv, z, b_logit, a_logit, conv_state, ssm_state, conv_w, a_log, dt_bias, g_norm):
    B, L, _ = qkv.shape
    conv_out, conv_new = causal_conv(qkv, conv_state, conv_w)
    conv_out = jax.nn.silu(conv_out)
    q = l2norm(conv_out[..., :H_A * DK_A].reshape(B, L, H_A, DK_A))
    k = l2norm(conv_out[..., H_A * DK_A:2 * H_A * DK_A].reshape(B, L, H_A, DK_A))
    v = conv_out[..., 2 * H_A * DK_A:].reshape(B, L, H_A, DV_A).astype(F32)
    beta = jax.nn.sigmoid(b_logit.astype(F32))
    g = -jnp.exp(a_log.astype(F32)) * jax.nn.softplus(a_logit.astype(F32) + dt_bias.astype(F32))
    o, S = gated_delta_chunked(q, k, v, g, beta, ssm_state.astype(F32))
    o = rmsnorm(o, g_norm) * jax.nn.silu(z.reshape(B, L, H_A, DV_A).astype(F32))
    return o.reshape(B, L, H_A * DV_A).astype(qkv.dtype), conv_new, S.astype(ssm_state.dtype)


def rel_bucket(n):
    n = jnp.maximum(n, 0)
    max_exact = NUM_BUCKETS // 2
    large = max_exact + (jnp.log(jnp.maximum(n, 1).astype(F32) / max_exact)
                         / math.log(MAX_DISTANCE / max_exact) * (NUM_BUCKETS - max_exact)).astype(jnp.int32)
    large = jnp.minimum(large, NUM_BUCKETS - 1)
    return jnp.where(n < max_exact, n, large)


def diff_logits(q, k, q_pos, k_pos, table):
    s = jnp.einsum("bqhmd,bkhmd->bhmqk", q.astype(F32), k.astype(F32)) * (DQK_B ** -0.5)
    rel = q_pos[:, None] - k_pos[None, :]
    bias = jnp.moveaxis(table.astype(F32)[rel_bucket(rel)], -1, 0)
    return jnp.where(rel >= 0, s + bias[None, :, None], NEG_INF)


def diff_core(q, q_pos, segments, lam, table):
    s = jnp.concatenate([diff_logits(q, k, q_pos, kp, table) for k, _, kp in segments], axis=-1)
    p = jax.nn.softmax(s, axis=-1)
    a = p[:, :, 0] - lam * p[:, :, 1]
    out, off = 0.0, 0
    for _, v, kp in segments:
        n = kp.shape[0]
        out = out + jnp.einsum("bhqk,bkhv->bqhv", a[..., off:off + n], v.astype(F32))
        off += n
    return out


def diff_attn_prompt(q, k, v, lam, table):
    B, L = q.shape[:2]
    nb = L // Q_BLOCK
    k_pos = jnp.arange(L)
    qb = q.reshape(B, nb, Q_BLOCK, H_B, 2, DQK_B).transpose(1, 0, 2, 3, 4, 5)

    def block(args):
        i, q_i = args
        q_pos = i * Q_BLOCK + jnp.arange(Q_BLOCK)
        return diff_core(q_i, q_pos, [(k, v, k_pos)], lam, table)

    o = lax.map(block, (jnp.arange(nb), qb))
    return o.transpose(1, 0, 2, 3, 4).reshape(B, L, H_B, DV_B)


def diff_attn_sample(q, k, v, cache_k, cache_v, page_table, layer, lam, table):
    Bd, L = q.shape[:2]
    past = page_table.shape[1] * PAGE_SIZE
    k_past = cache_k[layer, page_table].reshape(Bd, past, H_B, 2, DQK_B)
    v_past = cache_v[layer, page_table].reshape(Bd, past, H_B, DV_B)
    q_pos = past + jnp.arange(L)
    segments = [(k_past, v_past, jnp.arange(past)), (k, v, q_pos)]
    return diff_core(q, q_pos, segments, lam, table)


def decoder_layer(x, l, W, attend, conv_state, ssm_state):
    B, L, _ = x.shape
    h = rmsnorm(x, W["ffn1_pre"][l])
    x = x + 0.5 * rmsnorm(swiglu(h, W["ffn1_wg"][l], W["ffn1_wu"][l], W["ffn1_wd"][l]), W["ffn1_post"][l])
    h = rmsnorm(x, W["mix_pre"][l])
    c = split_cols(h @ W["w_in"][l])
    ya, conv_new, ssm_new = gdn_branch(c["qkv_a"], c["z_a"], c["beta_a"], c["alpha_a"], conv_state, ssm_state,
                                       W["conv_w"][l], W["a_log"][l], W["dt_bias"][l], W["gdn_norm"][l])
    q_b = c["q_b"].reshape(B, L, H_B, 2, DQK_B)
    k_b = c["k_b"].reshape(B, L, H_B, 2, DQK_B)
    v_b = c["v_b"].reshape(B, L, H_B, DV_B)
    lam_init = lambda_init(l)
    lam = (jnp.exp(jnp.sum(W["lam_q1"][l].astype(F32) * W["lam_k1"][l].astype(F32)))
           - jnp.exp(jnp.sum(W["lam_q2"][l].astype(F32) * W["lam_k2"][l].astype(F32))) + lam_init)
    ob = attend(l, q_b, k_b, v_b, lam)
    ob = (rmsnorm(ob, W["subln"][l]) * (1.0 - lam_init)).astype(x.dtype)
    yb = ob.reshape(B, L, H_B * DV_B)
    merged = (jax.nn.sigmoid(c["gate_a"]) * (ya @ W["w_a"][l])
              + jax.nn.sigmoid(c["gate_b"]) * (yb @ W["w_b"][l]))
    x = x + rmsnorm(merged @ W["w_out"][l], W["mix_post"][l])
    h = rmsnorm(x, W["ffn2_pre"][l])
    x = x + 0.5 * rmsnorm(swiglu(h, W["ffn2_wg"][l], W["ffn2_wu"][l], W["ffn2_wd"][l]), W["ffn2_post"][l])
    return x, k_b.reshape(B, L, H_B, 2 * DQK_B), v_b, conv_new, ssm_new


def setup_inputs(seed: int = 0) -> dict:
    key = jax.random.key(seed)
    ks = iter(jax.random.split(key, 48))
    nrm = lambda shape, scale: jax.random.normal(next(ks), shape, F32) * scale
    gain = lambda shape: 1.0 + nrm(shape, 0.02)
    n_pages = PAST_LEN // PAGE_SIZE
    used = DEC_BATCH * n_pages
    n_pool = used + max(1, used // 4)
    page_table = jax.random.permutation(next(ks), n_pool)[:used].reshape(DEC_BATCH, n_pages).astype(jnp.int32)
    dt = jnp.exp(jax.random.uniform(next(ks), (DEPTH, H_A), F32, math.log(1e-3), math.log(1e-1)))
    return {
        "x_prompt": nrm((BATCH, SEQ, D_MODEL), 1.0),
        "x_sample": nrm((DEC_BATCH, DEC_SEQ, D_MODEL), 1.0),
        "cache_k": nrm((DEPTH, n_pool, PAGE_SIZE, H_B, 2 * DQK_B), 1.0),
        "cache_v": nrm((DEPTH, n_pool, PAGE_SIZE, H_B, DV_B), 1.0),
        "state_conv": nrm((DEPTH, DEC_BATCH, CONV_W - 1, C_CONV), 1.0),
        "state_ssm": nrm((DEPTH, DEC_BATCH, H_A, DK_A, DV_A), 0.1),
        "page_table": page_table,
        "rel_table": nrm((NUM_BUCKETS, H_B), 0.5),
        "ffn1_pre": gain((DEPTH, D_MODEL)),
        "ffn1_wg": nrm((DEPTH, D_MODEL, D_FF), D_MODEL ** -0.5),
        "ffn1_wu": nrm((DEPTH, D_MODEL, D_FF), D_MODEL ** -0.5),
        "ffn1_wd": nrm((DEPTH, D_FF, D_MODEL), D_FF ** -0.5),
        "ffn1_post": gain((DEPTH, D_MODEL)),
        "mix_pre": gain((DEPTH, D_MODEL)),
        "w_in": nrm((DEPTH, D_MODEL, D_IN), D_MODEL ** -0.5),
        "conv_w": nrm((DEPTH, CONV_W, C_CONV), 0.5),
        "a_log": jnp.log(jax.random.uniform(next(ks), (DEPTH, H_A), F32, 1.0, 16.0)),
        "dt_bias": dt + jnp.log(-jnp.expm1(-dt)),
        "gdn_norm": gain((DEPTH, DV_A)),
        "lam_q1": nrm((DEPTH, DQK_B), 0.1),
        "lam_k1": nrm((DEPTH, DQK_B), 0.1),
        "lam_q2": nrm((DEPTH, DQK_B), 0.1),
        "lam_k2": nrm((DEPTH, DQK_B), 0.1),
        "subln": gain((DEPTH, DV_B)),
        "w_a": nrm((DEPTH, H_A * DV_A, D_MODEL), (H_A * DV_A) ** -0.5),
        "w_b": nrm((DEPTH, H_B * DV_B, D_MODEL), (H_B * DV_B) ** -0.5),
        "w_out": nrm((DEPTH, D_MODEL, D_MODEL), D_MODEL ** -0.5),
        "mix_post": gain((DEPTH, D_MODEL)),
        "ffn2_pre": gain((DEPTH, D_MODEL)),
        "ffn2_wg": nrm((DEPTH, D_MODEL, D_FF), D_MODEL ** -0.5),
        "ffn2_wu": nrm((DEPTH, D_MODEL, D_FF), D_MODEL ** -0.5),
        "ffn2_wd": nrm((DEPTH, D_FF, D_MODEL), D_FF ** -0.5),
        "ffn2_post": gain((DEPTH, D_MODEL)),
    }


def reference(x_prompt, x_sample, cache_k, cache_v, state_conv, state_ssm, page_table, rel_table,
              ffn1_pre, ffn1_wg, ffn1_wu, ffn1_wd, ffn1_post, mix_pre, w_in, conv_w, a_log, dt_bias,
              gdn_norm, lam_q1, lam_k1, lam_q2, lam_k2, subln, w_a, w_b, w_out, mix_post,
              ffn2_pre, ffn2_wg, ffn2_wu, ffn2_wd, ffn2_post):
    W = dict(ffn1_pre=ffn1_pre, ffn1_wg=ffn1_wg, ffn1_wu=ffn1_wu, ffn1_wd=ffn1_wd, ffn1_post=ffn1_post,
             mix_pre=mix_pre, w_in=w_in, conv_w=conv_w, a_log=a_log, dt_bias=dt_bias, gdn_norm=gdn_norm,
             lam_q1=lam_q1, lam_k1=lam_k1, lam_q2=lam_q2, lam_k2=lam_k2, subln=subln, w_a=w_a, w_b=w_b,
             w_out=w_out, mix_post=mix_post, ffn2_pre=ffn2_pre, ffn2_wg=ffn2_wg, ffn2_wu=ffn2_wu,
             ffn2_wd=ffn2_wd, ffn2_post=ffn2_post)
    attend_prompt = lambda l, q, k, v, lam: diff_attn_prompt(q, k, v, lam, rel_table)
    attend_sample = lambda l, q, k, v, lam: diff_attn_sample(q, k, v, cache_k, cache_v, page_table, l, lam, rel_table)
    bp = x_prompt.shape[0]
    xp, xs = x_prompt, x_sample
    kp, vp, cp, sp, ksm, vsm, csm, ssm = [], [], [], [], [], [], [], []
    for l in range(DEPTH):
        zero_conv = jnp.zeros((bp, CONV_W - 1, C_CONV), x_prompt.dtype)
        zero_ssm = jnp.zeros((bp, H_A, DK_A, DV_A), state_ssm.dtype)
        xp, k1, v1, c1, s1 = decoder_layer(xp, l, W, attend_prompt, zero_conv, zero_ssm)
        xs, k2, v2, c2, s2 = decoder_layer(xs, l, W, attend_sample, state_conv[l], state_ssm[l])
        kp.append(k1); vp.append(v1); cp.append(c1); sp.append(s1)
        ksm.append(k2); vsm.append(v2); csm.append(c2); ssm.append(s2)
    return (xp, xs, jnp.stack(kp), jnp.stack(vp), jnp.stack(cp), jnp.stack(sp),
            jnp.stack(ksm), jnp.stack(vsm), jnp.stack(csm), jnp.stack(ssm))
```

```python
import functools
import math

import jax
import jax.numpy as jnp
from jax import lax
from jax.experimental import pallas as pl
from jax.experimental.pallas import tpu as pltpu

F32 = jnp.float32
BF16 = jnp.bfloat16

H_A = 4
DK_A = 128
DV_A = 128
CONV_W = 4
CHUNK = 64
H_B = 4
DQK_B = 64
DV_B = 2 * DQK_B
PAGE_SIZE = 128
NUM_BUCKETS = 32
MAX_DISTANCE = 128
EPS = 1e-6
NEG_INF = -1e30

LANES = 128
SUBLANES = 8
VMEM_LIMIT = 56 * 1024 * 1024

HIGHEST = lax.Precision.HIGHEST


def _dot(a, b):
    return jnp.dot(a, b, preferred_element_type=F32)


def _dot_nt(a, b, precision=None):
    return lax.dot_general(a, b, (((1,), (1,)), ((), ())), precision=precision,
                           preferred_element_type=F32)


def _rms(x, g):
    return x * lax.rsqrt(jnp.mean(x * x, axis=-1, keepdims=True) + EPS) * g


def _silu(x):
    return x * jax.nn.sigmoid(x)


def _const_spec(shape):
    nd = len(shape)
    return pl.BlockSpec(shape, lambda *_: (0,) * nd, pipeline_mode=pl.Buffered(1))


def _params(sem):
    return pltpu.CompilerParams(dimension_semantics=sem, vmem_limit_bytes=VMEM_LIMIT)


def _ffn_body(x_ref, pre_ref, wg_ref, wu_ref, wd_ref, post_ref, o_ref, *, f_chunk):
    x = x_ref[...]
    h = _rms(x, pre_ref[...]).astype(BF16)
    d_ff = wg_ref.shape[1]
    acc = jnp.zeros(x.shape, F32)
    for c in range(d_ff // f_chunk):
        sl = slice(c * f_chunk, (c + 1) * f_chunk)
        g = _dot(h, wg_ref[:, sl])
        u = _dot(h, wu_ref[:, sl])
        acc = acc + _dot((_silu(g) * u).astype(BF16), wd_ref[sl, :])
    o_ref[...] = x + 0.5 * _rms(acc, post_ref[...])


def _ffn(x, pre, wg, wu, wd, post, *, tm):
    t, d = x.shape
    d_ff = wg.shape[1]
    f_chunk = 256 if d_ff % 256 == 0 else d_ff
    return pl.pallas_call(
        functools.partial(_ffn_body, f_chunk=f_chunk),
        grid=(t // tm,),
        in_specs=[pl.BlockSpec((tm, d), lambda i: (i, 0)),
                  _const_spec((1, d)), _const_spec((d, d_ff)), _const_spec((d, d_ff)),
                  _const_spec((d_ff, d)), _const_spec((1, d))],
        out_specs=pl.BlockSpec((tm, d), lambda i: (i, 0)),
        out_shape=jax.ShapeDtypeStruct((t, d), F32),
        compiler_params=_params(("parallel",)),
        name="ffn",
    )(x, pre, wg, wu, wd, post)


_PROJ_OUT = (("qkv", H_A * (2 * DK_A + DV_A), F32), ("z", H_A * DV_A, F32),
             ("q", H_B * 2 * DQK_B, BF16), ("k", H_B * 2 * DQK_B, F32), ("v", H_B * DV_B, F32),
             ("ga", None, F32), ("gb", None, F32), ("ba", LANES, F32))


def _proj_body(x_ref, g_ref, w_ref, *o_refs, groups):
    h = _rms(x_ref[...], g_ref[...]).astype(BF16)
    refs = dict(zip([n for n, _, _ in groups] + ["k16", "v16"], o_refs))
    off = 0
    for name, w, _ in groups:
        y = _dot(h, w_ref[:, off:off + w])
        refs[name][...] = y.astype(refs[name].dtype)
        if name in ("k", "v"):
            refs[name + "16"][...] = y.astype(BF16)
        off += w


def _proj(x, g, w_r, groups, *, tm):
    t, d = x.shape
    outs = [(n, w, dt) for n, w, dt in groups] + [("k16", groups[3][1], BF16), ("v16", groups[4][1], BF16)]
    res = pl.pallas_call(
        functools.partial(_proj_body, groups=groups),
        grid=(t // tm,),
        in_specs=[pl.BlockSpec((tm, d), lambda i: (i, 0)), _const_spec((1, d)), _const_spec(w_r.shape)],
        out_specs=[pl.BlockSpec((tm, w), lambda i: (i, 0)) for _, w, _ in outs],
        out_shape=[jax.ShapeDtypeStruct((t, w), dt) for _, w, dt in outs],
        compiler_params=_params(("parallel",)),
        name="proj",
    )(x, g, w_r)
    return dict(zip([n for n, _, _ in outs], res))


def _gdn_body(qkv_ref, cst_ref, z_ref, ba_ref, bat_ref, s0_ref, cw_ref, prow_ref, pcol_ref, gn_ref,
              ya_ref, s_ref, ext_ref, *, chunk, l_valid):
    c = chunk
    j = pl.program_id(1)

    @pl.when(j == 0)
    def _():
        ext_ref[0:SUBLANES, :] = cst_ref[0]
        s_ref[0] = s0_ref[0]

    u_raw = qkv_ref[0]
    ext_ref[SUBLANES:SUBLANES + c, :] = u_raw
    cw = cw_ref[...]
    conv = u_raw * cw[CONV_W - 1:CONV_W, :]
    for k in range(1, CONV_W):
        conv = conv + ext_ref[SUBLANES - k:SUBLANES - k + c, :] * cw[CONV_W - 1 - k:CONV_W - k, :]
    ext_ref[0:SUBLANES, :] = ext_ref[c:c + SUBLANES, :]
    act = _silu(conv)

    ba = ba_ref[0]
    beta_c = jax.nn.sigmoid(ba)
    g_c = -jnp.exp(prow_ref[0:1, :]) * jax.nn.softplus(ba + prow_ref[1:2, :])
    bat = bat_ref[0, 0]
    g_r = -jnp.exp(pcol_ref[0][:, :c]) * jax.nn.softplus(bat + pcol_ref[1][:, :c])
    if l_valid < c:
        g_c = jnp.where(lax.broadcasted_iota(jnp.int32, g_c.shape, 0) < l_valid, g_c, 0.0)
        beta_c = jnp.where(lax.broadcasted_iota(jnp.int32, beta_c.shape, 0) < l_valid, beta_c, 0.0)
        g_r = jnp.where(lax.broadcasted_iota(jnp.int32, g_r.shape, 1) < l_valid, g_r, 0.0)

    ii = lax.broadcasted_iota(jnp.int32, (c, c), 0)
    jj = lax.broadcasted_iota(jnp.int32, (c, c), 1)
    causal = ii >= jj
    strict = ii > jj
    ltri = causal.astype(F32)
    eye = (ii == jj).astype(F32)
    gcum_c = jnp.dot(ltri, g_c, precision=HIGHEST, preferred_element_type=F32)
    gcum_r = _dot_nt(g_r, ltri, precision=HIGHEST)

    hk = H_A * DK_A
    for h in range(H_A):
        q_raw = act[:, h * DK_A:(h + 1) * DK_A]
        k_raw = act[:, hk + h * DK_A:hk + (h + 1) * DK_A]
        v = act[:, 2 * hk + h * DV_A:2 * hk + (h + 1) * DV_A]
        q = q_raw * lax.rsqrt(jnp.sum(q_raw * q_raw, axis=-1, keepdims=True) + EPS) * (DK_A ** -0.5)
        k = k_raw * lax.rsqrt(jnp.sum(k_raw * k_raw, axis=-1, keepdims=True) + EPS)
        beta = beta_c[:, h:h + 1]
        gc = gcum_c[:, H_A + h:H_A + h + 1]
        gr = gcum_r[H_A + h:H_A + h + 1, :]
        decay = jnp.where(causal, jnp.exp(jnp.where(causal, gc - gr, 0.0)), 0.0)
        kb = k * beta
        k16 = k.astype(BF16)
        a = jnp.where(strict, _dot_nt(kb.astype(BF16), k16) * decay, 0.0)
        p = -a
        t_inv = eye + p
        n = 2
        while n < c:
            p = jnp.dot(p, p, precision=HIGHEST, preferred_element_type=F32)
            t_inv = t_inv + jnp.dot(t_inv, p, precision=HIGHEST, preferred_element_type=F32)
            n *= 2
        t16 = t_inv.astype(BF16)
        eg = jnp.exp(gc)
        u = _dot(t16, (v * beta).astype(BF16))
        w = _dot(t16, (kb * eg).astype(BF16))
        qk = _dot_nt(q.astype(BF16), k16) * decay
        s_old = s_ref[0, h]
        s16 = s_old.astype(BF16)
        v_new = u - _dot(w.astype(BF16), s16)
        v_new16 = v_new.astype(BF16)
        o = _dot((q * eg).astype(BF16), s16) + _dot(qk.astype(BF16), v_new16)
        g_last = gc[c - 1:c, :]
        k_dec = k * jnp.exp(g_last - gc)
        s_ref[0, h] = s_old * jnp.exp(g_last) + lax.dot_general(
            k_dec.astype(BF16), v_new16, (((0,), (0,)), ((), ())), preferred_element_type=F32)
        zh = z_ref[0, :, h * DV_A:(h + 1) * DV_A]
        ya_ref[0, :, h * DV_A:(h + 1) * DV_A] = (_rms(o, gn_ref[...]) * _silu(zh)).astype(ya_ref.dtype)


def _gdn(qkv, conv_state, z, ba, s0, conv_w, a_log, dt_bias, g_norm, *, chunk, l_valid):
    b, l, cc = qkv.shape
    n = l // chunk
    assert l_valid == chunk or n == 1
    cst = jnp.pad(conv_state, ((0, 0), (SUBLANES - (CONV_W - 1), 0), (0, 0)))
    bat = ba[..., :2 * H_A].reshape(b, n, chunk, 2 * H_A).transpose(0, 1, 3, 2)
    zeros_h = jnp.zeros((H_A,), F32)
    lane_pad = jnp.zeros((LANES - 2 * H_A,), F32)
    prow = jnp.stack([jnp.concatenate([zeros_h, a_log, lane_pad]),
                      jnp.concatenate([zeros_h, dt_bias, lane_pad])])
    pcol = jnp.broadcast_to(prow[:, :2 * H_A, None], (2, 2 * H_A, LANES))
    return pl.pallas_call(
        functools.partial(_gdn_body, chunk=chunk, l_valid=l_valid),
        grid=(b, n),
        in_specs=[pl.BlockSpec((1, chunk, cc), lambda i, j: (i, j, 0)),
                  pl.BlockSpec((1, SUBLANES, cc), lambda i, j: (i, 0, 0)),
                  pl.BlockSpec((1, chunk, H_A * DV_A), lambda i, j: (i, j, 0)),
                  pl.BlockSpec((1, chunk, LANES), lambda i, j: (i, j, 0)),
                  pl.BlockSpec((1, 1, 2 * H_A, chunk), lambda i, j: (i, j, 0, 0)),
                  pl.BlockSpec((1, H_A, DK_A, DV_A), lambda i, j: (i, 0, 0, 0)),
                  _const_spec((CONV_W, cc)), _const_spec((2, LANES)),
                  _const_spec((2, 2 * H_A, LANES)), _const_spec((1, DV_A))],
        out_specs=[pl.BlockSpec((1, chunk, H_A * DV_A), lambda i, j: (i, j, 0)),
                   pl.BlockSpec((1, H_A, DK_A, DV_A), lambda i, j: (i, 0, 0, 0))],
        out_shape=[jax.ShapeDtypeStruct((b, l, H_A * DV_A), BF16),
                   jax.ShapeDtypeStruct((b, H_A, DK_A, DV_A), F32)],
        scratch_shapes=[pltpu.VMEM((SUBLANES + chunk, cc), F32)],
        compiler_params=_params(("parallel", "arbitrary")),
        name="gdn",
    )(qkv, cst, z, ba, bat, s0, conv_w, prow, pcol, g_norm.reshape(1, DV_A))


def _rel_bucket(n):
    n = jnp.maximum(n, 0)
    max_exact = NUM_BUCKETS // 2
    large = max_exact + (jnp.log(jnp.maximum(n, 1).astype(F32) / max_exact)
                         / math.log(MAX_DISTANCE / max_exact) * (NUM_BUCKETS - max_exact)).astype(jnp.int32)
    large = jnp.minimum(large, NUM_BUCKETS - 1)
    return jnp.where(n < max_exact, n, large)


def _bias_of(rel, table):
    b = jnp.moveaxis(table.astype(F32)[_rel_bucket(rel)], -1, 0)
    return jnp.where(rel >= 0, b, NEG_INF)


def _lam_of(lp_ref, lam_init):
    lp = lp_ref[...]
    s1 = jnp.sum(lp[0:1] * lp[1:2], axis=-1, keepdims=True)
    s2 = jnp.sum(lp[2:3] * lp[3:4], axis=-1, keepdims=True)
    return jnp.exp(s1) - jnp.exp(s2) + lam_init


def _attn_prompt_body(q_ref, k_ref, v_ref, tiles_ref, cfar_ref, lp_ref, sub_ref, o_ref,
                      qq_ref, m_ref, l_ref, acc_ref, *, t, lam_init):
    i = pl.program_id(2)
    q = q_ref[0]
    lane = lax.broadcasted_iota(jnp.int32, q.shape, 1)
    scale = jnp.asarray(DQK_B ** -0.5, BF16)
    zero = jnp.zeros_like(q)
    qq_ref[0:t, :] = jnp.where(lane < DQK_B, q * scale, zero)
    qq_ref[t:2 * t, :] = jnp.where(lane >= DQK_B, q * scale, zero)
    m_ref[...] = jnp.full(m_ref.shape, NEG_INF, F32)
    l_ref[...] = jnp.zeros(l_ref.shape, F32)
    acc_ref[...] = jnp.zeros(acc_ref.shape, F32)

    def step(j, bias):
        start = pl.multiple_of(j * t, t)
        k = k_ref[0, pl.ds(start, t), :]
        v = v_ref[0, pl.ds(start, t), :]
        s = (_dot_nt(qq_ref[...], k).reshape(2, t, t) + bias).reshape(2 * t, t)
        m_prev = m_ref[...]
        m_new = jnp.maximum(m_prev, jnp.max(s, axis=-1, keepdims=True))
        alpha = jnp.exp(m_prev - m_new)
        p = jnp.exp(s - m_new)
        l_ref[...] = alpha * l_ref[...] + jnp.sum(p, axis=-1, keepdims=True)
        acc_ref[...] = alpha * acc_ref[...] + _dot(p.astype(BF16), v)
        m_ref[...] = m_new

    cfar = cfar_ref[0][:, 0:1]

    def far(j, carry):
        step(j, cfar)
        return carry

    lax.fori_loop(0, jnp.maximum(i - 1, 0), far, 0)

    @pl.when(i >= 1)
    def _():
        step(i - 1, tiles_ref[0, 1])

    step(i, tiles_ref[0, 0])

    lam = _lam_of(lp_ref, lam_init)
    acc = acc_ref[...] / l_ref[...]
    o = acc[0:t] - lam * acc[t:2 * t]
    o_ref[0] = (_rms(o, sub_ref[...]) * (1.0 - lam_init)).astype(o_ref.dtype)


def _attn_prompt(q, k, v, rel_table, lam_params, subln, *, t, lam_init):
    b, l, _ = q.shape
    nq = l // t
    assert t >= MAX_DISTANCE
    d = jnp.arange(t)[:, None] - jnp.arange(t)[None, :]
    tiles = jnp.stack([_bias_of(d, rel_table), _bias_of(d + t, rel_table)], axis=1)
    cfar = jnp.broadcast_to(rel_table[NUM_BUCKETS - 1].astype(F32)[:, None, None], (H_B, 1, LANES))
    hw = 2 * DQK_B
    return pl.pallas_call(
        functools.partial(_attn_prompt_body, t=t, lam_init=lam_init),
        grid=(b, H_B, nq),
        in_specs=[pl.BlockSpec((1, t, hw), lambda bi, h, i: (bi, i, h)),
                  pl.BlockSpec((1, l, hw), lambda bi, h, i: (bi, 0, h)),
                  pl.BlockSpec((1, l, DV_B), lambda bi, h, i: (bi, 0, h)),
                  pl.BlockSpec((1, 2, t, t), lambda bi, h, i: (h, 0, 0, 0)),
                  pl.BlockSpec((1, 1, LANES), lambda bi, h, i: (h, 0, 0)),
                  _const_spec(lam_params.shape), _const_spec((1, DV_B))],
        out_specs=pl.BlockSpec((1, t, DV_B), lambda bi, h, i: (bi, i, h)),
        out_shape=jax.ShapeDtypeStruct((b, l, H_B * DV_B), BF16),
        scratch_shapes=[pltpu.VMEM((2 * t, hw), BF16), pltpu.VMEM((2 * t, 1), F32),
                        pltpu.VMEM((2 * t, 1), F32), pltpu.VMEM((2 * t, DV_B), F32)],
        compiler_params=_params(("parallel", "parallel", "arbitrary")),
        name="attn_prompt",
    )(q, k, v, tiles, cfar, lam_params, subln.reshape(1, DV_B))


def _attn_sample_body(pt_ref, q_ref, kn_ref, vn_ref, bpast_ref, bself_ref, lp_ref, sub_ref, *rest,
                      pages, l_new, lam_init):
    k_refs = rest[:pages]
    v_refs = rest[pages:2 * pages]
    o_ref, qf_ref, m_ref, l_ref, acc_ref = rest[2 * pages:]
    j = pl.program_id(1)
    rows = l_new * SUBLANES
    width = H_B * 2 * DQK_B

    @pl.when(j == 0)
    def _():
        q = q_ref[0].astype(F32)
        qrep = jnp.concatenate([jnp.broadcast_to(q[t:t + 1], (SUBLANES, width)) for t in range(l_new)], axis=0)
        r = lax.broadcasted_iota(jnp.int32, (rows, width), 0)
        lane = lax.broadcasted_iota(jnp.int32, (rows, width), 1)
        qf = jnp.where(lane // DQK_B == r % SUBLANES, qrep * (DQK_B ** -0.5), 0.0)
        qf_ref[...] = qf
        kn = kn_ref[0]
        vn = vn_ref[0]
        s_self = [jnp.sum(qf * kn[t:t + 1], axis=-1, keepdims=True) + bself_ref[:, t:t + 1]
                  for t in range(l_new)]
        m0 = functools.reduce(jnp.maximum, s_self)
        p_self = [jnp.exp(s - m0) for s in s_self]
        m_ref[...] = m0
        l_ref[...] = functools.reduce(jnp.add, p_self)
        acc_ref[...] = functools.reduce(jnp.add, [p * vn[t:t + 1] for t, p in enumerate(p_self)])

    q16 = qf_ref[...].astype(BF16)
    s = jnp.concatenate([_dot_nt(q16, kr[0].astype(BF16)) for kr in k_refs], axis=-1) + bpast_ref[...]
    m_prev = m_ref[...]
    m_new = jnp.maximum(m_prev, jnp.max(s, axis=-1, keepdims=True))
    alpha = jnp.exp(m_prev - m_new)
    p = jnp.exp(s - m_new)
    l_ref[...] = alpha * l_ref[...] + jnp.sum(p, axis=-1, keepdims=True)
    p16 = p.astype(BF16)
    pv = functools.reduce(jnp.add, [_dot(p16[:, c * PAGE_SIZE:(c + 1) * PAGE_SIZE], vr[0].astype(BF16))
                                    for c, vr in enumerate(v_refs)])
    acc_ref[...] = alpha * acc_ref[...] + pv
    m_ref[...] = m_new

    @pl.when(j == pl.num_programs(1) - 1)
    def _():
        lam = _lam_of(lp_ref, lam_init)
        r = lax.broadcasted_iota(jnp.int32, (rows, width), 0)
        lane = lax.broadcasted_iota(jnp.int32, (rows, width), 1)
        coef = jnp.where(r % 2 == 0, 1.0, -lam)
        own = lane // DV_B == (r % SUBLANES) // 2
        an = jnp.where(own, acc_ref[...] / l_ref[...] * coef, 0.0)
        o = jnp.sum(an.reshape(l_new, SUBLANES, width), axis=1)
        for h in range(H_B):
            oh = o[:, h * DV_B:(h + 1) * DV_B]
            o_ref[0, :, h * DV_B:(h + 1) * DV_B] = _rms(oh, sub_ref[...]) * (1.0 - lam_init)


def _attn_sample(q, k_new, v_new, cache_k, cache_v, page_table, rel_table, lam_params, subln, *, pages, lam_init):
    bd, l_new, width = q.shape
    n_pages = page_table.shape[1]
    past = n_pages * PAGE_SIZE
    rows = l_new * SUBLANES
    t_of = jnp.arange(rows) // SUBLANES
    h_of = (jnp.arange(rows) % SUBLANES) // 2
    rel_past = past + t_of[:, None] - jnp.arange(past)[None, :]
    bpast = _bias_of(rel_past, rel_table)[h_of, jnp.arange(rows)]
    rel_self = t_of[:, None] - jnp.arange(LANES)[None, :]
    rel_self = jnp.where(jnp.arange(LANES)[None, :] < l_new, rel_self, -1)
    bself = _bias_of(rel_self, rel_table)[h_of, jnp.arange(rows)]

    def page_spec(c):
        return pl.BlockSpec((1, PAGE_SIZE, width),
                            lambda b, j, pt: (pt[b * n_pages + j * pages + c], 0, 0))

    tok_spec = pl.BlockSpec((1, l_new, width), lambda b, j, pt: (b, 0, 0))
    grid_spec = pltpu.PrefetchScalarGridSpec(
        num_scalar_prefetch=1,
        grid=(bd, n_pages // pages),
        in_specs=[tok_spec, tok_spec, tok_spec,
                  pl.BlockSpec((rows, pages * PAGE_SIZE), lambda b, j, pt: (0, j)),
                  pl.BlockSpec((rows, LANES), lambda b, j, pt: (0, 0)),
                  pl.BlockSpec(lam_params.shape, lambda b, j, pt: (0, 0)),
                  pl.BlockSpec((1, DV_B), lambda b, j, pt: (0, 0))]
                 + [page_spec(c) for c in range(pages)] + [page_spec(c) for c in range(pages)],
        out_specs=pl.BlockSpec((1, l_new, width), lambda b, j, pt: (b, 0, 0)),
        scratch_shapes=[pltpu.VMEM((rows, width), F32), pltpu.VMEM((rows, 1), F32),
                        pltpu.VMEM((rows, 1), F32), pltpu.VMEM((rows, width), F32)])
    return pl.pallas_call(
        functools.partial(_attn_sample_body, pages=pages, l_new=l_new, lam_init=lam_init),
        grid_spec=grid_spec,
        out_shape=jax.ShapeDtypeStruct((bd, l_new, width), F32),
        compiler_params=_params(("parallel", "arbitrary")),
        name="attn_sample",
    )(page_table.reshape(-1), q, k_new, v_new, bpast, bself, lam_params, subln.reshape(1, DV_B),
      *([cache_k] * pages), *([cache_v] * pages))


def _merge_body(x_ref, ya_ref, yb_ref, ga_ref, gb_ref, wa_ref, wb_ref, wo_ref, post_ref, o_ref):
    merged = (jax.nn.sigmoid(ga_ref[...]) * _dot(ya_ref[...], wa_ref[...])
              + jax.nn.sigmoid(gb_ref[...]) * _dot(yb_ref[...], wb_ref[...]))
    o_ref[...] = x_ref[...] + _rms(_dot(merged.astype(BF16), wo_ref[...]), post_ref[...])


def _merge(x, ya, yb, ga, gb, wa, wb, wo, post, *, tm):
    t, d = x.shape
    row = lambda w: pl.BlockSpec((tm, w), lambda i: (i, 0))
    return pl.pallas_call(
        _merge_body,
        grid=(t // tm,),
        in_specs=[row(d), row(ya.shape[1]), row(yb.shape[1]), row(d), row(d),
                  _const_spec(wa.shape), _const_spec(wb.shape), _const_spec(wo.shape), _const_spec((1, d))],
        out_specs=row(d),
        out_shape=jax.ShapeDtypeStruct((t, d), F32),
        compiler_params=_params(("parallel",)),
        name="merge",
    )(x, ya, yb, ga, gb, wa, wb, wo, post)


def _lambda_init(layer):
    return 0.8 - 0.6 * math.exp(-0.3 * layer)


def _token_tile(t):
    tm = 512
    while t % tm:
        tm //= 2
    return tm


def _attn_tile(seq):
    return min(512, seq)


def _layer_weights(l, W):
    d = W["w_in"].shape[1]
    bf = lambda a: a.astype(BF16)
    row = lambda a: a.reshape(1, -1).astype(F32)
    widths = {"qkv": H_A * (2 * DK_A + DV_A), "z": H_A * DV_A, "beta": H_A, "alpha": H_A,
              "q": H_B * 2 * DQK_B, "k": H_B * 2 * DQK_B, "v": H_B * DV_B, "ga": d, "gb": d}
    order = ("qkv", "z", "beta", "alpha", "q", "k", "v", "ga", "gb")
    offs, off = {}, 0
    for n in order:
        offs[n] = off
        off += widths[n]
    w_in = W["w_in"][l]
    col = lambda n: w_in[:, offs[n]:offs[n] + widths[n]]
    pad = jnp.zeros((d, LANES - 2 * H_A), w_in.dtype)
    w_r = jnp.concatenate([col("qkv"), col("z"), col("q"), col("k"), col("v"), col("ga"), col("gb"),
                           col("beta"), col("alpha"), pad], axis=1)
    groups = tuple((n, (d if w is None else w), dt) for n, w, dt in _PROJ_OUT)
    lam_params = jnp.stack([W["lam_q1"][l], W["lam_k1"][l], W["lam_q2"][l], W["lam_k2"][l]]).astype(F32)
    return dict(
        ffn1=(row(W["ffn1_pre"][l]), bf(W["ffn1_wg"][l]), bf(W["ffn1_wu"][l]), bf(W["ffn1_wd"][l]),
              row(W["ffn1_post"][l])),
        ffn2=(row(W["ffn2_pre"][l]), bf(W["ffn2_wg"][l]), bf(W["ffn2_wu"][l]), bf(W["ffn2_wd"][l]),
              row(W["ffn2_post"][l])),
        mix_pre=row(W["mix_pre"][l]), w_r=bf(w_r), groups=groups,
        conv_w=W["conv_w"][l].astype(F32), a_log=W["a_log"][l].astype(F32), dt_bias=W["dt_bias"][l].astype(F32),
        gdn_norm=W["gdn_norm"][l].astype(F32), lam_params=lam_params, subln=W["subln"][l].astype(F32),
        w_a=bf(W["w_a"][l]), w_b=bf(W["w_b"][l]), w_out=bf(W["w_out"][l]), mix_post=row(W["mix_post"][l]),
        lam_init=_lambda_init(l))


def _decoder_layer(x, lw, conv_state, ssm_state, attend):
    b, l, d = x.shape
    t = b * l
    tm = _token_tile(t)
    x2 = _ffn(x.reshape(t, d), *lw["ffn1"], tm=tm)
    c = _proj(x2, lw["mix_pre"], lw["w_r"], lw["groups"], tm=tm)
    c3 = {n: a.reshape(b, l, a.shape[-1]) for n, a in c.items()}

    chunk = min(CHUNK, l)
    if chunk % SUBLANES:
        chunk = -(-chunk // SUBLANES) * SUBLANES
    lp = -(-l // chunk) * chunk
    padl = lambda a: jnp.pad(a, ((0, 0), (0, lp - l), (0, 0))) if lp != l else a
    ya, ssm_new = _gdn(padl(c3["qkv"]), conv_state, padl(c3["z"]), padl(c3["ba"]), ssm_state,
                       lw["conv_w"], lw["a_log"], lw["dt_bias"], lw["gdn_norm"],
                       chunk=chunk, l_valid=min(l - (lp - chunk), chunk))
    ya = ya[:, :l]
    if l >= CONV_W - 1:
        conv_new = c3["qkv"][:, l - (CONV_W - 1):]
    else:
        conv_new = jnp.concatenate([conv_state.astype(F32), c3["qkv"]], axis=1)[:, -(CONV_W - 1):]

    yb = attend(c3["q"], c3["k"], c3["v"], c3["k16"], c3["v16"])
    y = _merge(x2, ya.reshape(t, -1), yb.reshape(t, -1).astype(BF16), c["ga"], c["gb"],
               lw["w_a"], lw["w_b"], lw["w_out"], lw["mix_post"], tm=tm)
    y = _ffn(y, *lw["ffn2"], tm=tm)
    return (y.reshape(b, l, d), c3["k"].reshape(b, l, H_B, 2 * DQK_B), c3["v"].reshape(b, l, H_B, DV_B),
            conv_new, ssm_new)


def kernel(x_prompt, x_sample, cache_k, cache_v, state_conv, state_ssm, page_table, rel_table,
           ffn1_pre, ffn1_wg, ffn1_wu, ffn1_wd, ffn1_post, mix_pre, w_in, conv_w, a_log, dt_bias,
           gdn_norm, lam_q1, lam_k1, lam_q2, lam_k2, subln, w_a, w_b, w_out, mix_post,
           ffn2_pre, ffn2_wg, ffn2_wu, ffn2_wd, ffn2_post):
    W = dict(ffn1_pre=ffn1_pre, ffn1_wg=ffn1_wg, ffn1_wu=ffn1_wu, ffn1_wd=ffn1_wd, ffn1_post=ffn1_post,
             mix_pre=mix_pre, w_in=w_in, conv_w=conv_w, a_log=a_log, dt_bias=dt_bias, gdn_norm=gdn_norm,
             lam_q1=lam_q1, lam_k1=lam_k1, lam_q2=lam_q2, lam_k2=lam_k2, subln=subln, w_a=w_a, w_b=w_b,
             w_out=w_out, mix_post=mix_post, ffn2_pre=ffn2_pre, ffn2_wg=ffn2_wg, ffn2_wu=ffn2_wu,
             ffn2_wd=ffn2_wd, ffn2_post=ffn2_post)
    depth = w_in.shape[0]
    bp, seq, _ = x_prompt.shape
    n_pool = cache_k.shape[1]
    n_pages = page_table.shape[1]
    t_attn = _attn_tile(seq)
    pages = math.gcd(n_pages, 8)
    xp, xs = x_prompt, x_sample
    outs = [[] for _ in range(8)]
    for l in range(depth):
        lw = _layer_weights(l, W)
        ck = cache_k[l].reshape(n_pool, PAGE_SIZE, -1)
        cv = cache_v[l].reshape(n_pool, PAGE_SIZE, -1)

        def attend_prompt(q, k, v, k16, v16):
            return _attn_prompt(q, k16, v16, rel_table, lw["lam_params"], lw["subln"],
                                t=t_attn, lam_init=lw["lam_init"])

        def attend_sample(q, k, v, k16, v16):
            return _attn_sample(q, k, v, ck, cv, page_table, rel_table, lw["lam_params"], lw["subln"],
                                pages=pages, lam_init=lw["lam_init"])

        zero_conv = jnp.zeros((bp, CONV_W - 1, state_conv.shape[-1]), x_prompt.dtype)
        zero_ssm = jnp.zeros((bp,) + state_ssm.shape[2:], state_ssm.dtype)
        xp, k1, v1, c1, s1 = _decoder_layer(xp, lw, zero_conv, zero_ssm, attend_prompt)
        xs, k2, v2, c2, s2 = _decoder_layer(xs, lw, state_conv[l], state_ssm[l], attend_sample)
        for o, a in zip(outs, (k1, v1, c1, s1, k2, v2, c2, s2)):
            o.append(a)
    return (xp, xs) + tuple(jnp.stack(o) for o in outs)
```

```python
import functools
import math

import jax
import jax.numpy as jnp
from jax import lax
from jax.experimental import pallas as pl
from jax.experimental.pallas import tpu as pltpu

F32 = jnp.float32
BF16 = jnp.bfloat16

H_A = 4
DK_A = 128
DV_A = 128
CONV_W = 4
CHUNK = 64
H_B = 4
DQK_B = 64
DV_B = 2 * DQK_B
PAGE_SIZE = 128
NUM_BUCKETS = 32
MAX_DISTANCE = 128
EPS = 1e-6
NEG_INF = -1e30

LANES = 128
SUBLANES = 8
VMEM_LIMIT = 56 * 1024 * 1024

HIGHEST = lax.Precision.HIGHEST


def _dot(a, b):
    return jnp.dot(a, b, preferred_element_type=F32)


def _dot_nt(a, b, precision=None):
    return lax.dot_general(a, b, (((1,), (1,)), ((), ())), precision=precision,
                           preferred_element_type=F32)


def _rms(x, g):
    return x * lax.rsqrt(jnp.mean(x * x, axis=-1, keepdims=True) + EPS) * g


def _silu(x):
    return x * jax.nn.sigmoid(x)


def _const_spec(shape):
    nd = len(shape)
    return pl.BlockSpec(shape, lambda *_: (0,) * nd, pipeline_mode=pl.Buffered(1))


def _params(sem):
    return pltpu.CompilerParams(dimension_semantics=sem, vmem_limit_bytes=VMEM_LIMIT)


def _ffn_body(x_ref, pre_ref, wg_ref, wu_ref, wd_ref, post_ref, o_ref, *, f_chunk):
    x = x_ref[...]
    h = _rms(x, pre_ref[...]).astype(BF16)
    d_ff = wg_ref.shape[1]
    acc = jnp.zeros(x.shape, F32)
    for c in range(d_ff // f_chunk):
        sl = slice(c * f_chunk, (c + 1) * f_chunk)
        g = _dot(h, wg_ref[:, sl])
        u = _dot(h, wu_ref[:, sl])
        acc = acc + _dot((_silu(g) * u).astype(BF16), wd_ref[sl, :])
    o_ref[...] = x + 0.5 * _rms(acc, post_ref[...])


def _ffn(x, pre, wg, wu, wd, post, *, tm):
    t, d = x.shape
    d_ff = wg.shape[1]
    f_chunk = 256 if d_ff % 256 == 0 else d_ff
    return pl.pallas_call(
        functools.partial(_ffn_body, f_chunk=f_chunk),
        grid=(t // tm,),
        in_specs=[pl.BlockSpec((tm, d), lambda i: (i, 0)),
                  _const_spec((1, d)), _const_spec((d, d_ff)), _const_spec((d, d_ff)),
                  _const_spec((d_ff, d)), _const_spec((1, d))],
        out_specs=pl.BlockSpec((tm, d), lambda i: (i, 0)),
        out_shape=jax.ShapeDtypeStruct((t, d), F32),
        compiler_params=_params(("parallel",)),
        name="ffn",
    )(x, pre, wg, wu, wd, post)


_PROJ_OUT = (("qkv", H_A * (2 * DK_A + DV_A), F32), ("z", H_A * DV_A, F32),
             ("q", H_B * 2 * DQK_B, BF16), ("k", H_B * 2 * DQK_B, F32), ("v", H_B * DV_B, F32),
             ("ga", None, F32), ("gb", None, F32), ("ba", LANES, F32))


def _proj_body(x_ref, g_ref, w_ref, *o_refs, groups):
    h = _rms(x_ref[...], g_ref[...]).astype(BF16)
    refs = dict(zip([n for n, _, _ in groups] + ["k16", "v16"], o_refs))
    off = 0
    for name, w, _ in groups:
        y = _dot(h, w_ref[:, off:off + w])
        refs[name][...] = y.astype(refs[name].dtype)
        if name in ("k", "v"):
            refs[name + "16"][...] = y.astype(BF16)
        off += w


def _proj(x, g, w_r, groups, *, tm):
    t, d = x.shape
    outs = [(n, w, dt) for n, w, dt in groups] + [("k16", groups[3][1], BF16), ("v16", groups[4][1], BF16)]
    res = pl.pallas_call(
        functools.partial(_proj_body, groups=groups),
        grid=(t // tm,),
        in_specs=[pl.BlockSpec((tm, d), lambda i: (i, 0)), _const_spec((1, d)), _const_spec(w_r.shape)],
        out_specs=[pl.BlockSpec((tm, w), lambda i: (i, 0)) for _, w, _ in outs],
        out_shape=[jax.ShapeDtypeStruct((t, w), dt) for _, w, dt in outs],
        compiler_params=_params(("parallel",)),
        name="proj",
    )(x, g, w_r)
    return dict(zip([n for n, _, _ in outs], res))


def _gdn_body(qkv_ref, cst_ref, z_ref, ba_ref, bat_ref, s0_ref, cw_ref, prow_ref, pcol_ref, gn_ref,
              ya_ref, s_ref, ext_ref, *, chunk, l_valid):
    c = chunk
    j = pl.program_id(1)

    @pl.when(j == 0)
    def _():
        ext_ref[0:SUBLANES, :] = cst_ref[0]
        s_ref[0] = s0_ref[0]

    u_raw = qkv_ref[0]
    ext_ref[SUBLANES:SUBLANES + c, :] = u_raw
    cw = cw_ref[...]
    conv = u_raw * cw[CONV_W - 1:CONV_W, :]
    for k in range(1, CONV_W):
        conv = conv + ext_ref[SUBLANES - k:SUBLANES - k + c, :] * cw[CONV_W - 1 - k:CONV_W - k, :]
    ext_ref[0:SUBLANES, :] = ext_ref[c:c + SUBLANES, :]
    act = _silu(conv)

    ba = ba_ref[0]
    beta_c = jax.nn.sigmoid(ba)
    g_c = -jnp.exp(prow_ref[0:1, :]) * jax.nn.softplus(ba + prow_ref[1:2, :])
    bat = bat_ref[0, 0]
    g_r = -jnp.exp(pcol_ref[0][:, :c]) * jax.nn.softplus(bat + pcol_ref[1][:, :c])
    if l_valid < c:
        g_c = jnp.where(lax.broadcasted_iota(jnp.int32, g_c.shape, 0) < l_valid, g_c, 0.0)
        beta_c = jnp.where(lax.broadcasted_iota(jnp.int32, beta_c.shape, 0) < l_valid, beta_c, 0.0)
        g_r = jnp.where(lax.broadcasted_iota(jnp.int32, g_r.shape, 1) < l_valid, g_r, 0.0)

    ii = lax.broadcasted_iota(jnp.int32, (c, c), 0)
    jj = lax.broadcasted_iota(jnp.int32, (c, c), 1)
    causal = ii >= jj
    strict = ii > jj
    ltri = causal.astype(F32)
    eye = (ii == jj).astype(F32)
    gcum_c = jnp.dot(ltri, g_c, precision=HIGHEST, preferred_element_type=F32)
    gcum_r = _dot_nt(g_r, ltri, precision=HIGHEST)

    hk = H_A * DK_A
    for h in range(H_A):
        q_raw = act[:, h * DK_A:(h + 1) * DK_A]
        k_raw = act[:, hk + h * DK_A:hk + (h + 1) * DK_A]
        v = act[:, 2 * hk + h * DV_A:2 * hk + (h + 1) * DV_A]
        q = q_raw * lax.rsqrt(jnp.sum(q_raw * q_raw, axis=-1, keepdims=True) + EPS) * (DK_A ** -0.5)
        k = k_raw * lax.rsqrt(jnp.sum(k_raw * k_raw, axis=-1, keepdims=True) + EPS)
        beta = beta_c[:, h:h + 1]
        gc = gcum_c[:, H_A + h:H_A + h + 1]
        gr = gcum_r[H_A + h:H_A + h + 1, :]
        decay = jnp.where(causal, jnp.exp(jnp.where(causal, gc - gr, 0.0)), 0.0)
        kb = k * beta
        k16 = k.astype(BF16)
        a = jnp.where(strict, _dot_nt(kb.astype(BF16), k16) * decay, 0.0)
        p = -a
        t_inv = eye + p
        n = 2
        while n < c:
            p = jnp.dot(p, p, precision=HIGHEST, preferred_element_type=F32)
            t_inv = t_inv + jnp.dot(t_inv, p, precision=HIGHEST, preferred_element_type=F32)
            n *= 2
        t16 = t_inv.astype(BF16)
        eg = jnp.exp(gc)
        u = _dot(t16, (v * beta).astype(BF16))
        w = _dot(t16, (kb * eg).astype(BF16))
        qk = _dot_nt(q.astype(BF16), k16) * decay
        s_old = s_ref[0, h]
        s16 = s_old.astype(BF16)
        v_new = u - _dot(w.astype(BF16), s16)
        v_new16 = v_new.astype(BF16)
        o = _dot((q * eg).astype(BF16), s16) + _dot(qk.astype(BF16), v_new16)
        g_last = gc[c - 1:c, :]
        k_dec = k * jnp.exp(g_last - gc)
        s_ref[0, h] = s_old * jnp.exp(g_last) + lax.dot_general(
            k_dec.astype(BF16), v_new16, (((0,), (0,)), ((), ())), preferred_element_type=F32)
        zh = z_ref[0, :, h * DV_A:(h + 1) * DV_A]
        ya_ref[0, :, h * DV_A:(h + 1) * DV_A] = (_rms(o, gn_ref[...]) * _silu(zh)).astype(ya_ref.dtype)


def _gdn(qkv, conv_state, z, ba, s0, conv_w, a_log, dt_bias, g_norm, *, chunk, l_valid):
    b, l, cc = qkv.shape
    n = l // chunk
    assert l_valid == chunk or n == 1
    cst = jnp.pad(conv_state, ((0, 0), (SUBLANES - (CONV_W - 1), 0), (0, 0)))
    bat = ba[..., :2 * H_A].reshape(b, n, chunk, 2 * H_A).transpose(0, 1, 3, 2)
    zeros_h = jnp.zeros((H_A,), F32)
    lane_pad = jnp.zeros((LANES - 2 * H_A,), F32)
    prow = jnp.stack([jnp.concatenate([zeros_h, a_log, lane_pad]),
                      jnp.concatenate([zeros_h, dt_bias, lane_pad])])
    pcol = jnp.broadcast_to(prow[:, :2 * H_A, None], (2, 2 * H_A, LANES))
    return pl.pallas_call(
        functools.partial(_gdn_body, chunk=chunk, l_valid=l_valid),
        grid=(b, n),
        in_specs=[pl.BlockSpec((1, chunk, cc), lambda i, j: (i, j, 0)),
                  pl.BlockSpec((1, SUBLANES, cc), lambda i, j: (i, 0, 0)),
                  pl.BlockSpec((1, chunk, H_A * DV_A), lambda i, j: (i, j, 0)),
                  pl.BlockSpec((1, chunk, LANES), lambda i, j: (i, j, 0)),
                  pl.BlockSpec((1, 1, 2 * H_A, chunk), lambda i, j: (i, j, 0, 0)),
                  pl.BlockSpec((1, H_A, DK_A, DV_A), lambda i, j: (i, 0, 0, 0)),
                  _const_spec((CONV_W, cc)), _const_spec((2, LANES)),
                  _const_spec((2, 2 * H_A, LANES)), _const_spec((1, DV_A))],
        out_specs=[pl.BlockSpec((1, chunk, H_A * DV_A), lambda i, j: (i, j, 0)),
                   pl.BlockSpec((1, H_A, DK_A, DV_A), lambda i, j: (i, 0, 0, 0))],
        out_shape=[jax.ShapeDtypeStruct((b, l, H_A * DV_A), BF16),
                   jax.ShapeDtypeStruct((b, H_A, DK_A, DV_A), F32)],
        scratch_shapes=[pltpu.VMEM((SUBLANES + chunk, cc), F32)],
        compiler_params=_params(("parallel", "arbitrary")),
        name="gdn",
    )(qkv, cst, z, ba, bat, s0, conv_w, prow, pcol, g_norm.reshape(1, DV_A))


def _rel_bucket(n):
    n = jnp.maximum(n, 0)
    max_exact = NUM_BUCKETS // 2
    large = max_exact + (jnp.log(jnp.maximum(n, 1).astype(F32) / max_exact)
                         / math.log(MAX_DISTANCE / max_exact) * (NUM_BUCKETS - max_exact)).astype(jnp.int32)
    large = jnp.minimum(large, NUM_BUCKETS - 1)
    return jnp.where(n < max_exact, n, large)


def _bias_rows(rel, tab_rows):
    onehot = _rel_bucket(rel)[..., None] == jnp.arange(NUM_BUCKETS)
    b = jnp.sum(jnp.where(onehot, tab_rows[..., None, :], 0.0), axis=-1)
    return jnp.where(rel >= 0, b, NEG_INF)


def _lam_of(lp_ref, lam_init):
    lp = lp_ref[...]
    s1 = jnp.sum(lp[0:1] * lp[1:2], axis=-1, keepdims=True)
    s2 = jnp.sum(lp[2:3] * lp[3:4], axis=-1, keepdims=True)
    return jnp.exp(s1) - jnp.exp(s2) + lam_init


def _attn_prompt_body(q_ref, k_ref, v_ref, tiles_ref, lp_ref, sub_ref, o_ref,
                      qq_ref, m_ref, l_ref, acc_ref, *, t, row_block, lam_init):
    i = pl.program_id(2)
    q = q_ref[0]
    lane = lax.broadcasted_iota(jnp.int32, q.shape, 1)
    scale = jnp.asarray(DQK_B ** -0.5, BF16)
    zero = jnp.zeros_like(q)
    qq_ref[0:t, :] = jnp.where(lane < DQK_B, q * scale, zero)
    qq_ref[t:2 * t, :] = jnp.where(lane >= DQK_B, q * scale, zero)
    m_ref[...] = jnp.full(m_ref.shape, NEG_INF, F32)
    l_ref[...] = jnp.zeros(l_ref.shape, F32)
    acc_ref[...] = jnp.zeros(acc_ref.shape, F32)
    n_col = t // LANES

    def step(j, tile):
        start = pl.multiple_of(j * t, t)
        k = k_ref[0, pl.ds(start, t), :]
        v = v_ref[0, pl.ds(start, t), :]
        for r0 in range(0, 2 * t, row_block):
            rows = slice(r0, r0 + row_block)
            s = _dot_nt(qq_ref[rows, :], k)
            if tile is not None:
                s = s + tiles_ref[0, tile, r0 % t:r0 % t + row_block, :]
            cols = [s[:, c * LANES:(c + 1) * LANES] for c in range(n_col)]
            m_prev = m_ref[rows, :]
            m_new = jnp.maximum(m_prev, jnp.max(functools.reduce(jnp.maximum, cols), axis=-1, keepdims=True))
            alpha = jnp.exp(m_prev - m_new)
            ps = [jnp.exp(c - m_new) for c in cols]
            l_ref[rows, :] = alpha * l_ref[rows, :] + functools.reduce(jnp.add, ps)
            p16 = jnp.concatenate([p.astype(BF16) for p in ps], axis=-1)
            acc_ref[rows, :] = alpha * acc_ref[rows, :] + _dot(p16, v)
            m_ref[rows, :] = m_new

    def far(j, carry):
        step(j, None)
        return carry

    lax.fori_loop(0, jnp.maximum(i - 1, 0), far, 0)

    @pl.when(i >= 1)
    def _():
        step(i - 1, 1)

    step(i, 0)

    lam = _lam_of(lp_ref, lam_init)
    acc = acc_ref[...] / jnp.sum(l_ref[...], axis=-1, keepdims=True)
    o = acc[0:t] - lam * acc[t:2 * t]
    o_ref[0] = (_rms(o, sub_ref[...]) * (1.0 - lam_init)).astype(o_ref.dtype)


def _attn_prompt(q, k, v, rel_table, lam_params, subln, *, t, lam_init):
    b, l, _ = q.shape
    nq = l // t
    assert t >= MAX_DISTANCE
    table = rel_table.astype(F32)
    d = jnp.arange(t)[:, None] - jnp.arange(t)[None, :]
    rel = jnp.broadcast_to(jnp.stack([d, d + t]), (H_B, 2, t, t))
    tab_rows = jnp.broadcast_to(table.T[:, None, None, :], (H_B, 2, t, NUM_BUCKETS))
    tiles = _bias_rows(rel, tab_rows) - table[NUM_BUCKETS - 1][:, None, None, None]
    hw = 2 * DQK_B
    return pl.pallas_call(
        functools.partial(_attn_prompt_body, t=t, row_block=min(256, t), lam_init=lam_init),
        grid=(b, H_B, nq),
        in_specs=[pl.BlockSpec((1, t, hw), lambda bi, h, i: (bi, i, h)),
                  pl.BlockSpec((1, l, hw), lambda bi, h, i: (bi, 0, h)),
                  pl.BlockSpec((1, l, DV_B), lambda bi, h, i: (bi, 0, h)),
                  pl.BlockSpec((1, 2, t, t), lambda bi, h, i: (h, 0, 0, 0)),
                  _const_spec(lam_params.shape), _const_spec((1, DV_B))],
        out_specs=pl.BlockSpec((1, t, DV_B), lambda bi, h, i: (bi, i, h)),
        out_shape=jax.ShapeDtypeStruct((b, l, H_B * DV_B), BF16),
        scratch_shapes=[pltpu.VMEM((2 * t, hw), BF16), pltpu.VMEM((2 * t, LANES), F32),
                        pltpu.VMEM((2 * t, LANES), F32), pltpu.VMEM((2 * t, DV_B), F32)],
        compiler_params=_params(("parallel", "parallel", "arbitrary")),
        name="attn_prompt",
    )(q, k, v, tiles, lam_params, subln.reshape(1, DV_B))


def _attn_sample_body(pt_ref, q_ref, kn_ref, vn_ref, bpast_ref, bself_ref, lp_ref, sub_ref, *rest,
                      pages, lam_init):
    k_refs = rest[:pages]
    v_refs = rest[pages:2 * pages]
    o_ref, qf_ref, m_ref, l_ref, acc_ref = rest[2 * pages:]
    j = pl.program_id(1)
    rows = qf_ref.shape[0]
    half = rows // 2
    page_cols = PAGE_SIZE * H_B

    @pl.when(j == 0)
    def _():
        q = q_ref[0].astype(F32)
        r = lax.broadcasted_iota(jnp.int32, q.shape, 0)
        lane = lax.broadcasted_iota(jnp.int32, q.shape, 1)
        qf = jnp.where(lane // DQK_B == r // half, q * (DQK_B ** -0.5), 0.0)
        qf_ref[...] = qf
        kn = kn_ref[0]
        vn = vn_ref[0]
        n_self = kn.shape[0]
        s_self = [jnp.sum(qf * kn[c:c + 1], axis=-1, keepdims=True) + bself_ref[:, c:c + 1]
                  for c in range(n_self)]
        m0 = functools.reduce(jnp.maximum, s_self)
        p_self = [jnp.exp(s - m0) for s in s_self]
        m_ref[...] = m0
        l_ref[...] = functools.reduce(jnp.add, p_self)
        acc_ref[...] = functools.reduce(jnp.add, [p * vn[c:c + 1] for c, p in enumerate(p_self)])

    q16 = qf_ref[...].astype(BF16)
    s = jnp.concatenate([_dot_nt(q16, kr[0].astype(BF16)) for kr in k_refs], axis=-1) + bpast_ref[j]
    m_prev = m_ref[...]
    m_new = jnp.maximum(m_prev, jnp.max(s, axis=-1, keepdims=True))
    alpha = jnp.exp(m_prev - m_new)
    p = jnp.exp(s - m_new)
    l_ref[...] = alpha * l_ref[...] + jnp.sum(p, axis=-1, keepdims=True)
    p16 = p.astype(BF16)
    pv = functools.reduce(jnp.add, [_dot(p16[:, c * page_cols:(c + 1) * page_cols], vr[0].astype(BF16))
                                    for c, vr in enumerate(v_refs)])
    acc_ref[...] = alpha * acc_ref[...] + pv
    m_ref[...] = m_new

    @pl.when(j == pl.num_programs(1) - 1)
    def _():
        lam = _lam_of(lp_ref, lam_init)
        an = acc_ref[...] / l_ref[...]
        o = an[0:half] - lam * an[half:rows]
        o_ref[0] = _rms(o, sub_ref[...]) * (1.0 - lam_init)


def _attn_sample(q, k_new, v_new, cache_k, cache_v, page_base, page_table, rel_table, lam_params, subln,
                 *, pages, lam_init):
    bd, l_new, width = q.shape
    n_pages = page_table.shape[1]
    past = n_pages * PAGE_SIZE
    half = l_new * H_B
    rows = 2 * half
    page_cols = PAGE_SIZE * H_B
    table = rel_table.astype(F32)
    t_of = (jnp.arange(rows) % half) // H_B
    h_of = jnp.arange(rows) % H_B
    tab_rows = table.T[h_of]
    far = table[NUM_BUCKETS - 1][h_of][:, None]
    own = h_of[:, None, None] == jnp.arange(H_B)[None, None, :]
    rel_past = past + t_of[:, None] - jnp.arange(past)[None, :]
    bpast = jnp.where(own, (_bias_rows(rel_past, tab_rows) - far)[:, :, None], NEG_INF)
    n_steps = n_pages // pages
    bpast = bpast.reshape(rows, n_steps, pages * page_cols).transpose(1, 0, 2)
    n_self = l_new * H_B
    t_key = jnp.arange(LANES) // H_B
    rel_self = jnp.where((jnp.arange(LANES)[None, :] < n_self) & (h_of[:, None] == jnp.arange(LANES)[None, :] % H_B),
                         t_of[:, None] - t_key[None, :], -1)
    bself = _bias_rows(rel_self, tab_rows) - far

    hd = width // H_B
    q_rows = q.reshape(bd, half, hd)
    q_rows = jnp.concatenate([q_rows, q_rows], axis=1)
    kn = k_new.reshape(bd, n_self, hd)
    vn = v_new.reshape(bd, n_self, hd)

    def page_spec(c):
        return pl.BlockSpec((1, page_cols, hd),
                            lambda b, j, pt: (page_base + pt[b * n_pages + j * pages + c], 0, 0))

    seq_spec = lambda r: pl.BlockSpec((1, r, hd), lambda b, j, pt: (b, 0, 0))
    whole = lambda shape: pl.BlockSpec(shape, lambda b, j, pt: (0,) * len(shape), pipeline_mode=pl.Buffered(1))
    grid_spec = pltpu.PrefetchScalarGridSpec(
        num_scalar_prefetch=1,
        grid=(bd, n_pages // pages),
        in_specs=[seq_spec(rows), seq_spec(n_self), seq_spec(n_self),
                  whole(bpast.shape), whole(bself.shape), whole(lam_params.shape), whole((1, DV_B))]
                 + [page_spec(c) for c in range(pages)] + [page_spec(c) for c in range(pages)],
        out_specs=seq_spec(half),
        scratch_shapes=[pltpu.VMEM((rows, hd), F32), pltpu.VMEM((rows, 1), F32),
                        pltpu.VMEM((rows, 1), F32), pltpu.VMEM((rows, DV_B), F32)])
    out = pl.pallas_call(
        functools.partial(_attn_sample_body, pages=pages, lam_init=lam_init),
        grid_spec=grid_spec,
        out_shape=jax.ShapeDtypeStruct((bd, half, DV_B), F32),
        compiler_params=_params(("parallel", "arbitrary")),
        name="attn_sample",
    )(page_table.reshape(-1), q_rows, kn, vn, bpast, bself, lam_params, subln.reshape(1, DV_B),
      *([cache_k] * pages), *([cache_v] * pages))
    return out.reshape(bd, l_new, width)


def _merge_body(x_ref, ya_ref, yb_ref, ga_ref, gb_ref, wa_ref, wb_ref, wo_ref, post_ref, o_ref):
    merged = (jax.nn.sigmoid(ga_ref[...]) * _dot(ya_ref[...], wa_ref[...])
              + jax.nn.sigmoid(gb_ref[...]) * _dot(yb_ref[...], wb_ref[...]))
    o_ref[...] = x_ref[...] + _rms(_dot(merged.astype(BF16), wo_ref[...]), post_ref[...])


def _merge(x, ya, yb, ga, gb, wa, wb, wo, post, *, tm):
    t, d = x.shape
    row = lambda w: pl.BlockSpec((tm, w), lambda i: (i, 0))
    return pl.pallas_call(
        _merge_body,
        grid=(t // tm,),
        in_specs=[row(d), row(ya.shape[1]), row(yb.shape[1]), row(d), row(d),
                  _const_spec(wa.shape), _const_spec(wb.shape), _const_spec(wo.shape), _const_spec((1, d))],
        out_specs=row(d),
        out_shape=jax.ShapeDtypeStruct((t, d), F32),
        compiler_params=_params(("parallel",)),
        name="merge",
    )(x, ya, yb, ga, gb, wa, wb, wo, post)


def _lambda_init(layer):
    return 0.8 - 0.6 * math.exp(-0.3 * layer)


def _token_tile(t):
    tm = 512
    while t % tm:
        tm //= 2
    return tm


def _attn_tile(seq):
    return min(512, seq)


def _layer_weights(l, W):
    d = W["w_in"].shape[1]
    bf = lambda a: a.astype(BF16)
    row = lambda a: a.reshape(1, -1).astype(F32)
    widths = {"qkv": H_A * (2 * DK_A + DV_A), "z": H_A * DV_A, "beta": H_A, "alpha": H_A,
              "q": H_B * 2 * DQK_B, "k": H_B * 2 * DQK_B, "v": H_B * DV_B, "ga": d, "gb": d}
    order = ("qkv", "z", "beta", "alpha", "q", "k", "v", "ga", "gb")
    offs, off = {}, 0
    for n in order:
        offs[n] = off
        off += widths[n]
    w_in = W["w_in"][l]
    col = lambda n: w_in[:, offs[n]:offs[n] + widths[n]]
    pad = jnp.zeros((d, LANES - 2 * H_A), w_in.dtype)
    w_r = jnp.concatenate([col("qkv"), col("z"), col("q"), col("k"), col("v"), col("ga"), col("gb"),
                           col("beta"), col("alpha"), pad], axis=1)
    groups = tuple((n, (d if w is None else w), dt) for n, w, dt in _PROJ_OUT)
    lam_params = jnp.stack([W["lam_q1"][l], W["lam_k1"][l], W["lam_q2"][l], W["lam_k2"][l]]).astype(F32)
    return dict(
        ffn1=(row(W["ffn1_pre"][l]), bf(W["ffn1_wg"][l]), bf(W["ffn1_wu"][l]), bf(W["ffn1_wd"][l]),
              row(W["ffn1_post"][l])),
        ffn2=(row(W["ffn2_pre"][l]), bf(W["ffn2_wg"][l]), bf(W["ffn2_wu"][l]), bf(W["ffn2_wd"][l]),
              row(W["ffn2_post"][l])),
        mix_pre=row(W["mix_pre"][l]), w_r=bf(w_r), groups=groups,
        conv_w=W["conv_w"][l].astype(F32), a_log=W["a_log"][l].astype(F32), dt_bias=W["dt_bias"][l].astype(F32),
        gdn_norm=W["gdn_norm"][l].astype(F32), lam_params=lam_params, subln=W["subln"][l].astype(F32),
        w_a=bf(W["w_a"][l]), w_b=bf(W["w_b"][l]), w_out=bf(W["w_out"][l]), mix_post=row(W["mix_post"][l]),
        lam_init=_lambda_init(l))


def _decoder_layer(x, lw, conv_state, ssm_state, attend):
    b, l, d = x.shape
    t = b * l
    tm = _token_tile(t)
    x2 = _ffn(x.reshape(t, d), *lw["ffn1"], tm=tm)
    c = _proj(x2, lw["mix_pre"], lw["w_r"], lw["groups"], tm=tm)
    c3 = {n: a.reshape(b, l, a.shape[-1]) for n, a in c.items()}

    chunk = min(CHUNK, l)
    if chunk % SUBLANES:
        chunk = -(-chunk // SUBLANES) * SUBLANES
    lp = -(-l // chunk) * chunk
    padl = lambda a: jnp.pad(a, ((0, 0), (0, lp - l), (0, 0))) if lp != l else a
    ya, ssm_new = _gdn(padl(c3["qkv"]), conv_state, padl(c3["z"]), padl(c3["ba"]), ssm_state,
                       lw["conv_w"], lw["a_log"], lw["dt_bias"], lw["gdn_norm"],
                       chunk=chunk, l_valid=min(l - (lp - chunk), chunk))
    ya = ya[:, :l]
    if l >= CONV_W - 1:
        conv_new = c3["qkv"][:, l - (CONV_W - 1):]
    else:
        conv_new = jnp.concatenate([conv_state.astype(F32), c3["qkv"]], axis=1)[:, -(CONV_W - 1):]

    yb = attend(c3["q"], c3["k"], c3["v"], c3["k16"], c3["v16"])
    y = _merge(x2, ya.reshape(t, -1), yb.reshape(t, -1).astype(BF16), c["ga"], c["gb"],
               lw["w_a"], lw["w_b"], lw["w_out"], lw["mix_post"], tm=tm)
    y = _ffn(y, *lw["ffn2"], tm=tm)
    return (y.reshape(b, l, d), c3["k"].reshape(b, l, H_B, 2 * DQK_B), c3["v"].reshape(b, l, H_B, DV_B),
            conv_new, ssm_new)


def kernel(x_prompt, x_sample, cache_k, cache_v, state_conv, state_ssm, page_table, rel_table,
           ffn1_pre, ffn1_wg, ffn1_wu, ffn1_wd, ffn1_post, mix_pre, w_in, conv_w, a_log, dt_bias,
           gdn_norm, lam_q1, lam_k1, lam_q2, lam_k2, subln, w_a, w_b, w_out, mix_post,
           ffn2_pre, ffn2_wg, ffn2_wu, ffn2_wd, ffn2_post):
    W = dict(ffn1_pre=ffn1_pre, ffn1_wg=ffn1_wg, ffn1_wu=ffn1_wu, ffn1_wd=ffn1_wd, ffn1_post=ffn1_post,
             mix_pre=mix_pre, w_in=w_in, conv_w=conv_w, a_log=a_log, dt_bias=dt_bias, gdn_norm=gdn_norm,
             lam_q1=lam_q1, lam_k1=lam_k1, lam_q2=lam_q2, lam_k2=lam_k2, subln=subln, w_a=w_a, w_b=w_b,
             w_out=w_out, mix_post=mix_post, ffn2_pre=ffn2_pre, ffn2_wg=ffn2_wg, ffn2_wu=ffn2_wu,
             ffn2_wd=ffn2_wd, ffn2_post=ffn2_post)
    depth = w_in.shape[0]
    bp, seq, _ = x_prompt.shape
    n_pool = cache_k.shape[1]
    n_pages = page_table.shape[1]
    t_attn = _attn_tile(seq)
    pages = math.gcd(n_pages, 8)
    ck = cache_k.reshape(depth * n_pool, PAGE_SIZE * H_B, -1)
    cv = cache_v.reshape(depth * n_pool, PAGE_SIZE * H_B, -1)
    xp, xs = x_prompt, x_sample
    outs = [[] for _ in range(8)]
    for l in range(depth):
        lw = _layer_weights(l, W)

        def attend_prompt(q, k, v, k16, v16):
            return _attn_prompt(q, k16, v16, rel_table, lw["lam_params"], lw["subln"],
                                t=t_attn, lam_init=lw["lam_init"])

        def attend_sample(q, k, v, k16, v16):
            return _attn_sample(q, k, v, ck, cv, l * n_pool, page_table, rel_table, lw["lam_params"],
                                lw["subln"], pages=pages, lam_init=lw["lam_init"])

        zero_conv = jnp.zeros((bp, CONV_W - 1, state_conv.shape[-1]), x_prompt.dtype)
        zero_ssm = jnp.zeros((bp,) + state_ssm.shape[2:], state_ssm.dtype)
        xp, k1, v1, c1, s1 = _decoder_layer(xp, lw, zero_conv, zero_ssm, attend_prompt)
        xs, k2, v2, c2, s2 = _decoder_layer(xs, lw, state_conv[l], state_ssm[l], attend_sample)
        for o, a in zip(outs, (k1, v1, c1, s1, k2, v2, c2, s2)):
            o.append(a)
    return (xp, xs) + tuple(jnp.stack(o) for o in outs)
```

```python
import functools
import math

import jax
import jax.numpy as jnp
from jax import lax
from jax.experimental import pallas as pl
from jax.experimental.pallas import tpu as pltpu

F32 = jnp.float32
BF16 = jnp.bfloat16

H_A = 4
DK_A = 128
DV_A = 128
CONV_W = 4
CHUNK = 64
H_B = 4
DQK_B = 64
DV_B = 2 * DQK_B
PAGE_SIZE = 128
NUM_BUCKETS = 32
MAX_DISTANCE = 128
EPS = 1e-6
NEG_INF = -1e30

LANES = 128
SUBLANES = 8
VMEM_LIMIT = 56 * 1024 * 1024

HIGHEST = lax.Precision.HIGHEST

GDN_CHUNKS_PER_STEP = 4
GDN_SEQS_PER_STEP = 8
ATTN_ROW_BLOCK = 256
ATTN_LOOKAHEAD = 2


def _dot(a, b):
    return jnp.dot(a, b, preferred_element_type=F32)


def _dot_nt(a, b, precision=None):
    return lax.dot_general(a, b, (((1,), (1,)), ((), ())), precision=precision,
                           preferred_element_type=F32)


def _rms(x, g):
    return x * lax.rsqrt(jnp.mean(x * x, axis=-1, keepdims=True) + EPS) * g


def _silu(x):
    return x * jax.nn.sigmoid(x)


def _const_spec(shape):
    nd = len(shape)
    return pl.BlockSpec(shape, lambda *_: (0,) * nd, pipeline_mode=pl.Buffered(1))


def _params(sem):
    return pltpu.CompilerParams(dimension_semantics=sem, vmem_limit_bytes=VMEM_LIMIT)


def _ffn_body(x_ref, pre_ref, wg_ref, wu_ref, wd_ref, post_ref, o_ref, *, f_chunk):
    x = x_ref[...]
    h = _rms(x, pre_ref[...]).astype(BF16)
    d_ff = wg_ref.shape[1]
    acc = jnp.zeros(x.shape, F32)
    for c in range(d_ff // f_chunk):
        sl = slice(c * f_chunk, (c + 1) * f_chunk)
        g = _dot(h, wg_ref[:, sl])
        u = _dot(h, wu_ref[:, sl])
        acc = acc + _dot((_silu(g) * u).astype(BF16), wd_ref[sl, :])
    o_ref[...] = x + 0.5 * _rms(acc, post_ref[...])


def _ffn(x, pre, wg, wu, wd, post, *, tm):
    t, d = x.shape
    d_ff = wg.shape[1]
    f_chunk = 256 if d_ff % 256 == 0 else d_ff
    return pl.pallas_call(
        functools.partial(_ffn_body, f_chunk=f_chunk),
        grid=(t // tm,),
        in_specs=[pl.BlockSpec((tm, d), lambda i: (i, 0)),
                  _const_spec((1, d)), _const_spec((d, d_ff)), _const_spec((d, d_ff)),
                  _const_spec((d_ff, d)), _const_spec((1, d))],
        out_specs=pl.BlockSpec((tm, d), lambda i: (i, 0)),
        out_shape=jax.ShapeDtypeStruct((t, d), F32),
        compiler_params=_params(("parallel",)),
        name="ffn",
    )(x, pre, wg, wu, wd, post)


_PROJ_OUT = (("qkv", H_A * (2 * DK_A + DV_A), F32), ("z", H_A * DV_A, F32),
             ("q", H_B * 2 * DQK_B, BF16), ("k", H_B * 2 * DQK_B, F32), ("v", H_B * DV_B, F32),
             ("ga", None, F32), ("gb", None, F32), ("ba", LANES, F32))


def _proj_body(x_ref, g_ref, w_ref, *o_refs, groups):
    h = _rms(x_ref[...], g_ref[...]).astype(BF16)
    refs = dict(zip([n for n, _, _ in groups] + ["k16", "v16"], o_refs))
    off = 0
    for name, w, _ in groups:
        y = _dot(h, w_ref[:, off:off + w])
        refs[name][...] = y.astype(refs[name].dtype)
        if name in ("k", "v"):
            refs[name + "16"][...] = y.astype(BF16)
        off += w


def _proj(x, g, w_r, groups, *, tm):
    t, d = x.shape
    outs = [(n, w, dt) for n, w, dt in groups] + [("k16", groups[3][1], BF16), ("v16", groups[4][1], BF16)]
    res = pl.pallas_call(
        functools.partial(_proj_body, groups=groups),
        grid=(t // tm,),
        in_specs=[pl.BlockSpec((tm, d), lambda i: (i, 0)), _const_spec((1, d)), _const_spec(w_r.shape)],
        out_specs=[pl.BlockSpec((tm, w), lambda i: (i, 0)) for _, w, _ in outs],
        out_shape=[jax.ShapeDtypeStruct((t, w), dt) for _, w, dt in outs],
        compiler_params=_params(("parallel",)),
        name="proj",
    )(x, g, w_r)
    return dict(zip([n for n, _, _ in outs], res))


def _gdn_body_v1(qkv_ref, cst_ref, z_ref, ba_ref, bat_ref, s0_ref, cw_ref, prow_ref, pcol_ref, gn_ref,
              ya_ref, s_ref, ext_ref, *, chunk, l_valid):
    c = chunk
    j = pl.program_id(1)

    @pl.when(j == 0)
    def _():
        ext_ref[0:SUBLANES, :] = cst_ref[0]
        s_ref[0] = s0_ref[0]

    u_raw = qkv_ref[0]
    ext_ref[SUBLANES:SUBLANES + c, :] = u_raw
    cw = cw_ref[...]
    conv = u_raw * cw[CONV_W - 1:CONV_W, :]
    for k in range(1, CONV_W):
        conv = conv + ext_ref[SUBLANES - k:SUBLANES - k + c, :] * cw[CONV_W - 1 - k:CONV_W - k, :]
    ext_ref[0:SUBLANES, :] = ext_ref[c:c + SUBLANES, :]
    act = _silu(conv)

    ba = ba_ref[0]
    beta_c = jax.nn.sigmoid(ba)
    g_c = -jnp.exp(prow_ref[0:1, :]) * jax.nn.softplus(ba + prow_ref[1:2, :])
    bat = bat_ref[0, 0]
    g_r = -jnp.exp(pcol_ref[0][:, :c]) * jax.nn.softplus(bat + pcol_ref[1][:, :c])
    if l_valid < c:
        g_c = jnp.where(lax.broadcasted_iota(jnp.int32, g_c.shape, 0) < l_valid, g_c, 0.0)
        beta_c = jnp.where(lax.broadcasted_iota(jnp.int32, beta_c.shape, 0) < l_valid, beta_c, 0.0)
        g_r = jnp.where(lax.broadcasted_iota(jnp.int32, g_r.shape, 1) < l_valid, g_r, 0.0)

    ii = lax.broadcasted_iota(jnp.int32, (c, c), 0)
    jj = lax.broadcasted_iota(jnp.int32, (c, c), 1)
    causal = ii >= jj
    strict = ii > jj
    ltri = causal.astype(F32)
    eye = (ii == jj).astype(F32)
    gcum_c = jnp.dot(ltri, g_c, precision=HIGHEST, preferred_element_type=F32)
    gcum_r = _dot_nt(g_r, ltri, precision=HIGHEST)

    hk = H_A * DK_A
    for h in range(H_A):
        q_raw = act[:, h * DK_A:(h + 1) * DK_A]
        k_raw = act[:, hk + h * DK_A:hk + (h + 1) * DK_A]
        v = act[:, 2 * hk + h * DV_A:2 * hk + (h + 1) * DV_A]
        q = q_raw * lax.rsqrt(jnp.sum(q_raw * q_raw, axis=-1, keepdims=True) + EPS) * (DK_A ** -0.5)
        k = k_raw * lax.rsqrt(jnp.sum(k_raw * k_raw, axis=-1, keepdims=True) + EPS)
        beta = beta_c[:, h:h + 1]
        gc = gcum_c[:, H_A + h:H_A + h + 1]
        gr = gcum_r[H_A + h:H_A + h + 1, :]
        decay = jnp.where(causal, jnp.exp(jnp.where(causal, gc - gr, 0.0)), 0.0)
        kb = k * beta
        k16 = k.astype(BF16)
        a = jnp.where(strict, _dot_nt(kb.astype(BF16), k16) * decay, 0.0)
        p = -a
        t_inv = eye + p
        n = 2
        while n < c:
            p = jnp.dot(p, p, precision=HIGHEST, preferred_element_type=F32)
            t_inv = t_inv + jnp.dot(t_inv, p, precision=HIGHEST, preferred_element_type=F32)
            n *= 2
        t16 = t_inv.astype(BF16)
        eg = jnp.exp(gc)
        u = _dot(t16, (v * beta).astype(BF16))
        w = _dot(t16, (kb * eg).astype(BF16))
        qk = _dot_nt(q.astype(BF16), k16) * decay
        s_old = s_ref[0, h]
        s16 = s_old.astype(BF16)
        v_new = u - _dot(w.astype(BF16), s16)
        v_new16 = v_new.astype(BF16)
        o = _dot((q * eg).astype(BF16), s16) + _dot(qk.astype(BF16), v_new16)
        g_last = gc[c - 1:c, :]
        k_dec = k * jnp.exp(g_last - gc)
        s_ref[0, h] = s_old * jnp.exp(g_last) + lax.dot_general(
            k_dec.astype(BF16), v_new16, (((0,), (0,)), ((), ())), preferred_element_type=F32)
        zh = z_ref[0, :, h * DV_A:(h + 1) * DV_A]
        ya_ref[0, :, h * DV_A:(h + 1) * DV_A] = (_rms(o, gn_ref[...]) * _silu(zh)).astype(ya_ref.dtype)


def _gdn_v1(qkv, conv_state, z, ba, s0, conv_w, a_log, dt_bias, g_norm, *, chunk, l_valid):
    b, l, cc = qkv.shape
    n = l // chunk
    assert l_valid == chunk or n == 1
    cst = jnp.pad(conv_state, ((0, 0), (SUBLANES - (CONV_W - 1), 0), (0, 0)))
    bat = ba[..., :2 * H_A].reshape(b, n, chunk, 2 * H_A).transpose(0, 1, 3, 2)
    zeros_h = jnp.zeros((H_A,), F32)
    lane_pad = jnp.zeros((LANES - 2 * H_A,), F32)
    prow = jnp.stack([jnp.concatenate([zeros_h, a_log, lane_pad]),
                      jnp.concatenate([zeros_h, dt_bias, lane_pad])])
    pcol = jnp.broadcast_to(prow[:, :2 * H_A, None], (2, 2 * H_A, LANES))
    return pl.pallas_call(
        functools.partial(_gdn_body, chunk=chunk, l_valid=l_valid),
        grid=(b, n),
        in_specs=[pl.BlockSpec((1, chunk, cc), lambda i, j: (i, j, 0)),
                  pl.BlockSpec((1, SUBLANES, cc), lambda i, j: (i, 0, 0)),
                  pl.BlockSpec((1, chunk, H_A * DV_A), lambda i, j: (i, j, 0)),
                  pl.BlockSpec((1, chunk, LANES), lambda i, j: (i, j, 0)),
                  pl.BlockSpec((1, 1, 2 * H_A, chunk), lambda i, j: (i, j, 0, 0)),
                  pl.BlockSpec((1, H_A, DK_A, DV_A), lambda i, j: (i, 0, 0, 0)),
                  _const_spec((CONV_W, cc)), _const_spec((2, LANES)),
                  _const_spec((2, 2 * H_A, LANES)), _const_spec((1, DV_A))],
        out_specs=[pl.BlockSpec((1, chunk, H_A * DV_A), lambda i, j: (i, j, 0)),
                   pl.BlockSpec((1, H_A, DK_A, DV_A), lambda i, j: (i, 0, 0, 0))],
        out_shape=[jax.ShapeDtypeStruct((b, l, H_A * DV_A), BF16),
                   jax.ShapeDtypeStruct((b, H_A, DK_A, DV_A), F32)],
        scratch_shapes=[pltpu.VMEM((SUBLANES + chunk, cc), F32)],
        compiler_params=_params(("parallel", "arbitrary")),
        name="gdn",
    )(qkv, cst, z, ba, bat, s0, conv_w, prow, pcol, g_norm.reshape(1, DV_A))


def _split2(x):
    hi = x.astype(BF16)
    return hi, (x - hi.astype(F32)).astype(BF16)


def _split3(x):
    hi = x.astype(BF16)
    r = x - hi.astype(F32)
    mid = r.astype(BF16)
    return hi, mid, (r - mid.astype(F32)).astype(BF16)


def _mm3(x, y):
    x_hi, x_lo = _split2(x)
    y_hi, y_lo = _split2(y)
    return _dot(x_hi, y_hi) + _dot(x_lo, y_hi) + _dot(x_hi, y_lo)


def _neumann_level(p, t_inv, c):
    if c % 16 == 0:
        x_hi, x_lo = _split2(jnp.concatenate([p, t_inv], axis=0))
        p_hi, p_lo = x_hi[:c], x_lo[:c]
        y = _dot(jnp.concatenate([x_hi, x_lo], axis=0), p_hi)
        y = y[:2 * c] + y[2 * c:] + _dot(x_hi, p_lo)
        return y[:c], t_inv + y[c:]
    p_hi, p_lo = _split2(p)
    t_hi, t_lo = _split2(t_inv)
    pp = _dot(p_hi, p_hi) + _dot(p_lo, p_hi) + _dot(p_hi, p_lo)
    tp = _dot(t_hi, p_hi) + _dot(t_lo, p_hi) + _dot(t_hi, p_lo)
    return pp, t_inv + tp


def _gdn_body(qkv_ref, cst_ref, z_ref, ba_ref, bat_ref, s0_ref, cw_ref, prow_ref, pcol_ref, gn_ref,
              ya_ref, s_ref, ext_ref, *, chunk, n_chunk, n_seq, l_valid):
    c = chunk
    tb = n_chunk * c
    j = pl.program_id(1)

    @pl.when(j == 0)
    def _():
        ext_ref[:, 0:SUBLANES, :] = cst_ref[...]
        s_ref[...] = s0_ref[...]

    ii = lax.broadcasted_iota(jnp.int32, (c, c), 0)
    jj = lax.broadcasted_iota(jnp.int32, (c, c), 1)
    causal = ii >= jj
    strict = ii > jj
    ltri16 = causal.astype(BF16)
    eye = (ii == jj).astype(F32)
    cw = cw_ref[...]
    hk = H_A * DK_A

    rows = lambda n: slice(n * c, (n + 1) * c)
    act, beta_c, g_c = [], [], []
    for s in range(n_seq):
        u_raw = qkv_ref[s]
        ext_ref[s, SUBLANES:SUBLANES + tb, :] = u_raw
        conv = u_raw * cw[CONV_W - 1:CONV_W, :]
        for k in range(1, CONV_W):
            conv = conv + ext_ref[s, SUBLANES - k:SUBLANES - k + tb, :] * cw[CONV_W - 1 - k:CONV_W - k, :]
        ext_ref[s, 0:SUBLANES, :] = ext_ref[s, tb:tb + SUBLANES, :]
        act.append(_silu(conv))
        ba = ba_ref[s]
        beta_s = jax.nn.sigmoid(ba)
        g_s = -jnp.exp(prow_ref[0:1, :]) * jax.nn.softplus(ba + prow_ref[1:2, :])
        if l_valid < c:
            valid = lax.broadcasted_iota(jnp.int32, g_s.shape, 0) < l_valid
            g_s = jnp.where(valid, g_s, 0.0)
            beta_s = jnp.where(valid, beta_s, 0.0)
        beta_c.append(beta_s)
        g_c.append(g_s)

    blocks = [(s, n) for s in range(n_seq) for n in range(n_chunk)]
    chains = [(s, n, h) for s, n in blocks for h in range(H_A)]

    gcum_c, gcum_r = {}, {}
    for s, n in blocks:
        g_r = -jnp.exp(pcol_ref[0][:, :c]) * jax.nn.softplus(bat_ref[s, n] + pcol_ref[1][:, :c])
        if l_valid < c:
            g_r = jnp.where(lax.broadcasted_iota(jnp.int32, g_r.shape, 1) < l_valid, g_r, 0.0)
        gcum_c[s, n] = functools.reduce(jnp.add, [_dot(ltri16, x) for x in _split3(g_c[s][rows(n)])])
        gcum_r[s, n] = functools.reduce(jnp.add, [_dot_nt(x, ltri16) for x in _split3(g_r)])

    st = {}
    for ch in chains:
        s, n, h = ch
        rs = rows(n)
        q_raw = act[s][rs, h * DK_A:(h + 1) * DK_A]
        k_raw = act[s][rs, hk + h * DK_A:hk + (h + 1) * DK_A]
        v = act[s][rs, 2 * hk + h * DV_A:2 * hk + (h + 1) * DV_A]
        q = q_raw * lax.rsqrt(jnp.sum(q_raw * q_raw, axis=-1, keepdims=True) + EPS) * (DK_A ** -0.5)
        k = k_raw * lax.rsqrt(jnp.sum(k_raw * k_raw, axis=-1, keepdims=True) + EPS)
        beta = beta_c[s][rs, h:h + 1]
        gc = gcum_c[s, n][:, H_A + h:H_A + h + 1]
        gr = gcum_r[s, n][H_A + h:H_A + h + 1, :]
        decay = jnp.where(causal, jnp.exp(jnp.where(causal, gc - gr, 0.0)), 0.0)
        kb = k * beta
        kq = _dot_nt(jnp.concatenate([kb, q], axis=0).astype(BF16), k.astype(BF16))
        eg = jnp.exp(gc)
        g_last = gc[c - 1:c, :]
        p = -jnp.where(strict, kq[:c] * decay, 0.0)
        st[ch] = dict(p=p, t=eye + p, qk=(kq[c:] * decay).astype(BF16),
                      rhs=jnp.concatenate([v * beta, kb * eg], axis=1).astype(BF16), qe=q * eg,
                      k_dec=(k * jnp.exp(g_last - gc)).astype(BF16), dec=jnp.exp(g_last))

    n_factor = c.bit_length() - 1
    if n_factor >= 2:
        for e in st.values():
            e["p"] = _mm3(e["p"], e["p"])
        for _ in range(n_factor - 2):
            for e in st.values():
                e["p"], e["t"] = _neumann_level(e["p"], e["t"], c)
        for e in st.values():
            e["t"] = e["t"] + _mm3(e["t"], e["p"])
    for e in st.values():
        uw = _dot(e["t"].astype(BF16), e["rhs"])
        e["u"] = uw[:, :DV_A]
        e["wq"] = jnp.concatenate([uw[:, DV_A:], e["qe"]], axis=0).astype(BF16)

    state = {(s, h): s_ref[s, h] for s in range(n_seq) for h in range(H_A)}
    for n in range(n_chunk):
        grp = [(s, h) for s in range(n_seq) for h in range(H_A)]
        ws = {g: _dot(st[g[0], n, g[1]]["wq"], state[g].astype(BF16)) for g in grp}
        v_new = {g: (st[g[0], n, g[1]]["u"] - ws[g][:c]).astype(BF16) for g in grp}
        for g in grp:
            s, h = g
            e = st[s, n, h]
            o = ws[g][c:] + _dot(e["qk"], v_new[g])
            state[g] = state[g] * e["dec"] + lax.dot_general(
                e["k_dec"], v_new[g], (((0,), (0,)), ((), ())), preferred_element_type=F32)
            zh = z_ref[s, rows(n), h * DV_A:(h + 1) * DV_A]
            ya_ref[s, rows(n), h * DV_A:(h + 1) * DV_A] = (_rms(o, gn_ref[...]) * _silu(zh)).astype(ya_ref.dtype)
    for (s, h), val in state.items():
        s_ref[s, h] = val


def _gdn(qkv, conv_state, z, ba, s0, conv_w, a_log, dt_bias, g_norm, *, chunk, n_chunk, n_seq, l_valid):
    b, l, cc = qkv.shape
    n = l // chunk
    tb = n_chunk * chunk
    assert l_valid == chunk or n == 1
    assert l % tb == 0 and b % n_seq == 0
    assert chunk & (chunk - 1) == 0
    cst = jnp.pad(conv_state, ((0, 0), (SUBLANES - (CONV_W - 1), 0), (0, 0)))
    bat = ba[..., :2 * H_A].reshape(b, n, chunk, 2 * H_A).transpose(0, 1, 3, 2)
    zeros_h = jnp.zeros((H_A,), F32)
    lane_pad = jnp.zeros((LANES - 2 * H_A,), F32)
    prow = jnp.stack([jnp.concatenate([zeros_h, a_log, lane_pad]),
                      jnp.concatenate([zeros_h, dt_bias, lane_pad])])
    pcol = jnp.broadcast_to(prow[:, :2 * H_A, None], (2, 2 * H_A, LANES))
    return pl.pallas_call(
        functools.partial(_gdn_body, chunk=chunk, n_chunk=n_chunk, n_seq=n_seq, l_valid=l_valid),
        grid=(b // n_seq, l // tb),
        in_specs=[pl.BlockSpec((n_seq, tb, cc), lambda i, j: (i, j, 0)),
                  pl.BlockSpec((n_seq, SUBLANES, cc), lambda i, j: (i, 0, 0)),
                  pl.BlockSpec((n_seq, tb, H_A * DV_A), lambda i, j: (i, j, 0)),
                  pl.BlockSpec((n_seq, tb, LANES), lambda i, j: (i, j, 0)),
                  pl.BlockSpec((n_seq, n_chunk, 2 * H_A, chunk), lambda i, j: (i, j, 0, 0)),
                  pl.BlockSpec((n_seq, H_A, DK_A, DV_A), lambda i, j: (i, 0, 0, 0)),
                  _const_spec((CONV_W, cc)), _const_spec((2, LANES)),
                  _const_spec((2, 2 * H_A, LANES)), _const_spec((1, DV_A))],
        out_specs=[pl.BlockSpec((n_seq, tb, H_A * DV_A), lambda i, j: (i, j, 0)),
                   pl.BlockSpec((n_seq, H_A, DK_A, DV_A), lambda i, j: (i, 0, 0, 0))],
        out_shape=[jax.ShapeDtypeStruct((b, l, H_A * DV_A), BF16),
                   jax.ShapeDtypeStruct((b, H_A, DK_A, DV_A), F32)],
        scratch_shapes=[pltpu.VMEM((n_seq, SUBLANES + tb, cc), F32)],
        compiler_params=_params(("parallel", "arbitrary")),
        name="gdn",
    )(qkv, cst, z, ba, bat, s0, conv_w, prow, pcol, g_norm.reshape(1, DV_A))


def _rel_bucket(n):
    n = jnp.maximum(n, 0)
    max_exact = NUM_BUCKETS // 2
    large = max_exact + (jnp.log(jnp.maximum(n, 1).astype(F32) / max_exact)
                         / math.log(MAX_DISTANCE / max_exact) * (NUM_BUCKETS - max_exact)).astype(jnp.int32)
    large = jnp.minimum(large, NUM_BUCKETS - 1)
    return jnp.where(n < max_exact, n, large)


def _bias_rows(rel, tab_rows):
    onehot = _rel_bucket(rel)[..., None] == jnp.arange(NUM_BUCKETS)
    b = jnp.sum(jnp.where(onehot, tab_rows[..., None, :], 0.0), axis=-1)
    return jnp.where(rel >= 0, b, NEG_INF)


def _lam_of(lp_ref, lam_init):
    lp = lp_ref[...]
    s1 = jnp.sum(lp[0:1] * lp[1:2], axis=-1, keepdims=True)
    s2 = jnp.sum(lp[2:3] * lp[3:4], axis=-1, keepdims=True)
    return jnp.exp(s1) - jnp.exp(s2) + lam_init


def _attn_prompt_body(q_ref, k_ref, v_ref, tiles_ref, lp_ref, sub_ref, o_ref,
                      qq_ref, m_ref, l_ref, acc_ref, *, t, row_block, lam_init):
    i = pl.program_id(2)
    q = q_ref[0]
    lane = lax.broadcasted_iota(jnp.int32, q.shape, 1)
    scale = jnp.asarray(DQK_B ** -0.5, BF16)
    zero = jnp.zeros_like(q)
    qq_ref[0:t, :] = jnp.where(lane < DQK_B, q * scale, zero)
    qq_ref[t:2 * t, :] = jnp.where(lane >= DQK_B, q * scale, zero)
    m_ref[...] = jnp.full(m_ref.shape, NEG_INF, F32)
    l_ref[...] = jnp.zeros(l_ref.shape, F32)
    acc_ref[...] = jnp.zeros(acc_ref.shape, F32)
    n_col = t // LANES

    def step(j, tile):
        start = pl.multiple_of(j * t, t)
        k = k_ref[0, pl.ds(start, t), :]
        v = v_ref[0, pl.ds(start, t), :]
        starts = list(range(0, 2 * t, row_block))
        qk = lambda r: _dot_nt(qq_ref[r:r + row_block, :], k)
        scores = {r: qk(r) for r in starts[:ATTN_LOOKAHEAD]}
        for idx, r0 in enumerate(starts):
            rows = slice(r0, r0 + row_block)
            if idx + ATTN_LOOKAHEAD < len(starts):
                scores[starts[idx + ATTN_LOOKAHEAD]] = qk(starts[idx + ATTN_LOOKAHEAD])
            s = scores.pop(r0)
            if tile is not None:
                s = s + tiles_ref[0, tile, r0 % t:r0 % t + row_block, :]
            cols = [s[:, c * LANES:(c + 1) * LANES] for c in range(n_col)]
            m_prev = m_ref[rows, :]
            m_new = jnp.maximum(m_prev, jnp.max(functools.reduce(jnp.maximum, cols), axis=-1, keepdims=True))
            alpha = jnp.exp(m_prev - m_new)
            ps = [jnp.exp(c - m_new) for c in cols]
            l_ref[rows, :] = alpha * l_ref[rows, :] + functools.reduce(jnp.add, ps)
            p16 = jnp.concatenate([p.astype(BF16) for p in ps], axis=-1)
            acc_ref[rows, :] = alpha * acc_ref[rows, :] + _dot(p16, v)
            m_ref[rows, :] = m_new

    def far(j, carry):
        step(j, None)
        return carry

    lax.fori_loop(0, jnp.maximum(i - 1, 0), far, 0)

    @pl.when(i >= 1)
    def _():
        step(i - 1, 1)

    step(i, 0)

    lam = _lam_of(lp_ref, lam_init)
    acc = acc_ref[...] / jnp.sum(l_ref[...], axis=-1, keepdims=True)
    o = acc[0:t] - lam * acc[t:2 * t]
    o_ref[0] = (_rms(o, sub_ref[...]) * (1.0 - lam_init)).astype(o_ref.dtype)


def _attn_prompt(q, k, v, rel_table, lam_params, subln, *, t, lam_init):
    b, l, _ = q.shape
    nq = l // t
    assert t >= MAX_DISTANCE
    table = rel_table.astype(F32)
    d = jnp.arange(t)[:, None] - jnp.arange(t)[None, :]
    rel = jnp.broadcast_to(jnp.stack([d, d + t]), (H_B, 2, t, t))
    tab_rows = jnp.broadcast_to(table.T[:, None, None, :], (H_B, 2, t, NUM_BUCKETS))
    tiles = _bias_rows(rel, tab_rows) - table[NUM_BUCKETS - 1][:, None, None, None]
    hw = 2 * DQK_B
    return pl.pallas_call(
        functools.partial(_attn_prompt_body, t=t, row_block=min(ATTN_ROW_BLOCK, t), lam_init=lam_init),
        grid=(b, H_B, nq),
        in_specs=[pl.BlockSpec((1, t, hw), lambda bi, h, i: (bi, i, h)),
                  pl.BlockSpec((1, l, hw), lambda bi, h, i: (bi, 0, h)),
                  pl.BlockSpec((1, l, DV_B), lambda bi, h, i: (bi, 0, h)),
                  pl.BlockSpec((1, 2, t, t), lambda bi, h, i: (h, 0, 0, 0)),
                  _const_spec(lam_params.shape), _const_spec((1, DV_B))],
        out_specs=pl.BlockSpec((1, t, DV_B), lambda bi, h, i: (bi, i, h)),
        out_shape=jax.ShapeDtypeStruct((b, l, H_B * DV_B), BF16),
        scratch_shapes=[pltpu.VMEM((2 * t, hw), BF16), pltpu.VMEM((2 * t, LANES), F32),
                        pltpu.VMEM((2 * t, LANES), F32), pltpu.VMEM((2 * t, DV_B), F32)],
        compiler_params=_params(("parallel", "parallel", "arbitrary")),
        name="attn_prompt",
    )(q, k, v, tiles, lam_params, subln.reshape(1, DV_B))


def _attn_sample_body(pt_ref, q_ref, kn_ref, vn_ref, bpast_ref, bself_ref, lp_ref, sub_ref, *rest,
                      pages, lam_init):
    k_refs = rest[:pages]
    v_refs = rest[pages:2 * pages]
    o_ref, qf_ref, m_ref, l_ref, acc_ref = rest[2 * pages:]
    j = pl.program_id(1)
    rows = qf_ref.shape[0]
    half = rows // 2
    page_cols = PAGE_SIZE * H_B

    @pl.when(j == 0)
    def _():
        q = q_ref[0].astype(F32)
        r = lax.broadcasted_iota(jnp.int32, q.shape, 0)
        lane = lax.broadcasted_iota(jnp.int32, q.shape, 1)
        qf = jnp.where(lane // DQK_B == r // half, q * (DQK_B ** -0.5), 0.0)
        qf_ref[...] = qf
        kn = kn_ref[0]
        vn = vn_ref[0]
        n_self = kn.shape[0]
        s_self = [jnp.sum(qf * kn[c:c + 1], axis=-1, keepdims=True) + bself_ref[:, c:c + 1]
                  for c in range(n_self)]
        m0 = functools.reduce(jnp.maximum, s_self)
        p_self = [jnp.exp(s - m0) for s in s_self]
        m_ref[...] = m0
        l_ref[...] = functools.reduce(jnp.add, p_self)
        acc_ref[...] = functools.reduce(jnp.add, [p * vn[c:c + 1] for c, p in enumerate(p_self)])

    q16 = qf_ref[...].astype(BF16)
    s = jnp.concatenate([_dot_nt(q16, kr[0].astype(BF16)) for kr in k_refs], axis=-1) + bpast_ref[j]
    m_prev = m_ref[...]
    m_new = jnp.maximum(m_prev, jnp.max(s, axis=-1, keepdims=True))
    alpha = jnp.exp(m_prev - m_new)
    p = jnp.exp(s - m_new)
    l_ref[...] = alpha * l_ref[...] + jnp.sum(p, axis=-1, keepdims=True)
    p16 = p.astype(BF16)
    pv = functools.reduce(jnp.add, [_dot(p16[:, c * page_cols:(c + 1) * page_cols], vr[0].astype(BF16))
                                    for c, vr in enumerate(v_refs)])
    acc_ref[...] = alpha * acc_ref[...] + pv
    m_ref[...] = m_new

    @pl.when(j == pl.num_programs(1) - 1)
    def _():
        lam = _lam_of(lp_ref, lam_init)
        an = acc_ref[...] / l_ref[...]
        o = an[0:half] - lam * an[half:rows]
        o_ref[0] = _rms(o, sub_ref[...]) * (1.0 - lam_init)


def _attn_sample(q, k_new, v_new, cache_k, cache_v, page_base, page_table, rel_table, lam_params, subln,
                 *, pages, lam_init):
    bd, l_new, width = q.shape
    n_pages = page_table.shape[1]
    past = n_pages * PAGE_SIZE
    half = l_new * H_B
    rows = 2 * half
    page_cols = PAGE_SIZE * H_B
    table = rel_table.astype(F32)
    t_of = (jnp.arange(rows) % half) // H_B
    h_of = jnp.arange(rows) % H_B
    tab_rows = table.T[h_of]
    far = table[NUM_BUCKETS - 1][h_of][:, None]
    own = h_of[:, None, None] == jnp.arange(H_B)[None, None, :]
    rel_past = past + t_of[:, None] - jnp.arange(past)[None, :]
    bpast = jnp.where(own, (_bias_rows(rel_past, tab_rows) - far)[:, :, None], NEG_INF)
    n_steps = n_pages // pages
    bpast = bpast.reshape(rows, n_steps, pages * page_cols).transpose(1, 0, 2)
    n_self = l_new * H_B
    t_key = jnp.arange(LANES) // H_B
    rel_self = jnp.where((jnp.arange(LANES)[None, :] < n_self) & (h_of[:, None] == jnp.arange(LANES)[None, :] % H_B),
                         t_of[:, None] - t_key[None, :], -1)
    bself = _bias_rows(rel_self, tab_rows) - far

    hd = width // H_B
    q_rows = q.reshape(bd, half, hd)
    q_rows = jnp.concatenate([q_rows, q_rows], axis=1)
    kn = k_new.reshape(bd, n_self, hd)
    vn = v_new.reshape(bd, n_self, hd)

    def page_spec(c):
        return pl.BlockSpec((1, page_cols, hd),
                            lambda b, j, pt: (page_base + pt[b * n_pages + j * pages + c], 0, 0))

    seq_spec = lambda r: pl.BlockSpec((1, r, hd), lambda b, j, pt: (b, 0, 0))
    whole = lambda shape: pl.BlockSpec(shape, lambda b, j, pt: (0,) * len(shape), pipeline_mode=pl.Buffered(1))
    grid_spec = pltpu.PrefetchScalarGridSpec(
        num_scalar_prefetch=1,
        grid=(bd, n_pages // pages),
        in_specs=[seq_spec(rows), seq_spec(n_self), seq_spec(n_self),
                  whole(bpast.shape), whole(bself.shape), whole(lam_params.shape), whole((1, DV_B))]
                 + [page_spec(c) for c in range(pages)] + [page_spec(c) for c in range(pages)],
        out_specs=seq_spec(half),
        scratch_shapes=[pltpu.VMEM((rows, hd), F32), pltpu.VMEM((rows, 1), F32),
                        pltpu.VMEM((rows, 1), F32), pltpu.VMEM((rows, DV_B), F32)])
    out = pl.pallas_call(
        functools.partial(_attn_sample_body, pages=pages, lam_init=lam_init),
        grid_spec=grid_spec,
        out_shape=jax.ShapeDtypeStruct((bd, half, DV_B), F32),
        compiler_params=_params(("parallel", "arbitrary")),
        name="attn_sample",
    )(page_table.reshape(-1), q_rows, kn, vn, bpast, bself, lam_params, subln.reshape(1, DV_B),
      *([cache_k] * pages), *([cache_v] * pages))
    return out.reshape(bd, l_new, width)


def _merge_body(x_ref, ya_ref, yb_ref, ga_ref, gb_ref, wa_ref, wb_ref, wo_ref, post_ref, o_ref):
    merged = (jax.nn.sigmoid(ga_ref[...]) * _dot(ya_ref[...], wa_ref[...])
              + jax.nn.sigmoid(gb_ref[...]) * _dot(yb_ref[...], wb_ref[...]))
    o_ref[...] = x_ref[...] + _rms(_dot(merged.astype(BF16), wo_ref[...]), post_ref[...])


def _merge(x, ya, yb, ga, gb, wa, wb, wo, post, *, tm):
    t, d = x.shape
    row = lambda w: pl.BlockSpec((tm, w), lambda i: (i, 0))
    return pl.pallas_call(
        _merge_body,
        grid=(t // tm,),
        in_specs=[row(d), row(ya.shape[1]), row(yb.shape[1]), row(d), row(d),
                  _const_spec(wa.shape), _const_spec(wb.shape), _const_spec(wo.shape), _const_spec((1, d))],
        out_specs=row(d),
        out_shape=jax.ShapeDtypeStruct((t, d), F32),
        compiler_params=_params(("parallel",)),
        name="merge",
    )(x, ya, yb, ga, gb, wa, wb, wo, post)


def _lambda_init(layer):
    return 0.8 - 0.6 * math.exp(-0.3 * layer)


def _token_tile(t):
    tm = 512
    while t % tm:
        tm //= 2
    return tm


def _attn_tile(seq):
    return min(512, seq)


def _layer_weights(l, W):
    d = W["w_in"].shape[1]
    bf = lambda a: a.astype(BF16)
    row = lambda a: a.reshape(1, -1).astype(F32)
    widths = {"qkv": H_A * (2 * DK_A + DV_A), "z": H_A * DV_A, "beta": H_A, "alpha": H_A,
              "q": H_B * 2 * DQK_B, "k": H_B * 2 * DQK_B, "v": H_B * DV_B, "ga": d, "gb": d}
    order = ("qkv", "z", "beta", "alpha", "q", "k", "v", "ga", "gb")
    offs, off = {}, 0
    for n in order:
        offs[n] = off
        off += widths[n]
    w_in = W["w_in"][l]
    col = lambda n: w_in[:, offs[n]:offs[n] + widths[n]]
    pad = jnp.zeros((d, LANES - 2 * H_A), w_in.dtype)
    w_r = jnp.concatenate([col("qkv"), col("z"), col("q"), col("k"), col("v"), col("ga"), col("gb"),
                           col("beta"), col("alpha"), pad], axis=1)
    groups = tuple((n, (d if w is None else w), dt) for n, w, dt in _PROJ_OUT)
    lam_params = jnp.stack([W["lam_q1"][l], W["lam_k1"][l], W["lam_q2"][l], W["lam_k2"][l]]).astype(F32)
    return dict(
        ffn1=(row(W["ffn1_pre"][l]), bf(W["ffn1_wg"][l]), bf(W["ffn1_wu"][l]), bf(W["ffn1_wd"][l]),
              row(W["ffn1_post"][l])),
        ffn2=(row(W["ffn2_pre"][l]), bf(W["ffn2_wg"][l]), bf(W["ffn2_wu"][l]), bf(W["ffn2_wd"][l]),
              row(W["ffn2_post"][l])),
        mix_pre=row(W["mix_pre"][l]), w_r=bf(w_r), groups=groups,
        conv_w=W["conv_w"][l].astype(F32), a_log=W["a_log"][l].astype(F32), dt_bias=W["dt_bias"][l].astype(F32),
        gdn_norm=W["gdn_norm"][l].astype(F32), lam_params=lam_params, subln=W["subln"][l].astype(F32),
        w_a=bf(W["w_a"][l]), w_b=bf(W["w_b"][l]), w_out=bf(W["w_out"][l]), mix_post=row(W["mix_post"][l]),
        lam_init=_lambda_init(l))


def _decoder_layer(x, lw, conv_state, ssm_state, attend):
    b, l, d = x.shape
    t = b * l
    tm = _token_tile(t)
    x2 = _ffn(x.reshape(t, d), *lw["ffn1"], tm=tm)
    c = _proj(x2, lw["mix_pre"], lw["w_r"], lw["groups"], tm=tm)
    c3 = {n: a.reshape(b, l, a.shape[-1]) for n, a in c.items()}

    chunk = min(CHUNK, l)
    if chunk % SUBLANES:
        chunk = -(-chunk // SUBLANES) * SUBLANES
    lp = -(-l // chunk) * chunk
    padl = lambda a: jnp.pad(a, ((0, 0), (0, lp - l), (0, 0))) if lp != l else a
    ya, ssm_new = _gdn(padl(c3["qkv"]), conv_state, padl(c3["z"]), padl(c3["ba"]), ssm_state,
                       lw["conv_w"], lw["a_log"], lw["dt_bias"], lw["gdn_norm"],
                       chunk=chunk, n_chunk=math.gcd(lp // chunk, GDN_CHUNKS_PER_STEP),
                       n_seq=math.gcd(b, GDN_SEQS_PER_STEP) if lp == chunk else 1,
                       l_valid=min(l - (lp - chunk), chunk))
    ya = ya[:, :l]
    if l >= CONV_W - 1:
        conv_new = c3["qkv"][:, l - (CONV_W - 1):]
    else:
        conv_new = jnp.concatenate([conv_state.astype(F32), c3["qkv"]], axis=1)[:, -(CONV_W - 1):]

    yb = attend(c3["q"], c3["k"], c3["v"], c3["k16"], c3["v16"])
    y = _merge(x2, ya.reshape(t, -1), yb.reshape(t, -1).astype(BF16), c["ga"], c["gb"],
               lw["w_a"], lw["w_b"], lw["w_out"], lw["mix_post"], tm=tm)
    y = _ffn(y, *lw["ffn2"], tm=tm)
    return (y.reshape(b, l, d), c3["k"].reshape(b, l, H_B, 2 * DQK_B), c3["v"].reshape(b, l, H_B, DV_B),
            conv_new, ssm_new)


def kernel(x_prompt, x_sample, cache_k, cache_v, state_conv, state_ssm, page_table, rel_table,
           ffn1_pre, ffn1_wg, ffn1_wu, ffn1_wd, ffn1_post, mix_pre, w_in, conv_w, a_log, dt_bias,
           gdn_norm, lam_q1, lam_k1, lam_q2, lam_k2, subln, w_a, w_b, w_out, mix_post,
           ffn2_pre, ffn2_wg, ffn2_wu, ffn2_wd, ffn2_post):
    W = dict(ffn1_pre=ffn1_pre, ffn1_wg=ffn1_wg, ffn1_wu=ffn1_wu, ffn1_wd=ffn1_wd, ffn1_post=ffn1_post,
             mix_pre=mix_pre, w_in=w_in, conv_w=conv_w, a_log=a_log, dt_bias=dt_bias, gdn_norm=gdn_norm,
             lam_q1=lam_q1, lam_k1=lam_k1, lam_q2=lam_q2, lam_k2=lam_k2, subln=subln, w_a=w_a, w_b=w_b,
             w_out=w_out, mix_post=mix_post, ffn2_pre=ffn2_pre, ffn2_wg=ffn2_wg, ffn2_wu=ffn2_wu,
             ffn2_wd=ffn2_wd, ffn2_post=ffn2_post)
    depth = w_in.shape[0]
    bp, seq, _ = x_prompt.shape
    n_pool = cache_k.shape[1]
    n_pages = page_table.shape[1]
    t_attn = _attn_tile(seq)
    pages = math.gcd(n_pages, 8)
    ck = cache_k.reshape(depth * n_pool, PAGE_SIZE * H_B, -1)
    cv = cache_v.reshape(depth * n_pool, PAGE_SIZE * H_B, -1)
    xp, xs = x_prompt, x_sample
    outs = [[] for _ in range(8)]
    for l in range(depth):
        lw = _layer_weights(l, W)

        def attend_prompt(q, k, v, k16, v16):
            return _attn_prompt(q, k16, v16, rel_table, lw["lam_params"], lw["subln"],
                                t=t_attn, lam_init=lw["lam_init"])

        def attend_sample(q, k, v, k16, v16):
            return _attn_sample(q, k, v, ck, cv, l * n_pool, page_table, rel_table, lw["lam_params"],
                                lw["subln"], pages=pages, lam_init=lw["lam_init"])

        zero_conv = jnp.zeros((bp, CONV_W - 1, state_conv.shape[-1]), x_prompt.dtype)
        zero_ssm = jnp.zeros((bp,) + state_ssm.shape[2:], state_ssm.dtype)
        xp, k1, v1, c1, s1 = _decoder_layer(xp, lw, zero_conv, zero_ssm, attend_prompt)
        xs, k2, v2, c2, s2 = _decoder_layer(xs, lw, state_conv[l], state_ssm[l], attend_sample)
        for o, a in zip(outs, (k1, v1, c1, s1, k2, v2, c2, s2)):
            o.append(a)
    return (xp, xs) + tuple(jnp.stack(o) for o in outs)
```

```python
import functools
import math

import jax
import jax.numpy as jnp
from jax import lax
from jax.experimental import pallas as pl
from jax.experimental.pallas import tpu as pltpu

F32 = jnp.float32
BF16 = jnp.bfloat16

H_A = 4
DK_A = 128
DV_A = 128
CONV_W = 4
CHUNK = 64
H_B = 4
DQK_B = 64
DV_B = 2 * DQK_B
PAGE_SIZE = 128
NUM_BUCKETS = 32
MAX_DISTANCE = 128
EPS = 1e-6
NEG_INF = -1e30

LANES = 128
SUBLANES = 8
VMEM_LIMIT = 56 * 1024 * 1024

HIGHEST = lax.Precision.HIGHEST

GDN_CHUNKS_PER_STEP = 4
GDN_SEQS_PER_STEP = 8
ATTN_ROW_BLOCK = 256
ATTN_COL_BLOCK = 512
ATTN_KEY_CHUNK = 64
ATTN_LOOKAHEAD = 2
SAMPLE_PAGES_PER_STEP = 16
SAMPLE_PAGE_GROUP = 4


def _dot(a, b):
    return jnp.dot(a, b, preferred_element_type=F32)


def _dot_nt(a, b, precision=None):
    return lax.dot_general(a, b, (((1,), (1,)), ((), ())), precision=precision,
                           preferred_element_type=F32)


def _rms(x, g):
    return x * lax.rsqrt(jnp.mean(x * x, axis=-1, keepdims=True) + EPS) * g


def _silu(x):
    return x * jax.nn.sigmoid(x)


def _const_spec(shape):
    nd = len(shape)
    return pl.BlockSpec(shape, lambda *_: (0,) * nd, pipeline_mode=pl.Buffered(1))


def _params(sem):
    return pltpu.CompilerParams(dimension_semantics=sem, vmem_limit_bytes=VMEM_LIMIT)


def _ffn_body(x_ref, pre_ref, wg_ref, wu_ref, wd_ref, post_ref, o_ref, *, f_chunk):
    x = x_ref[...]
    h = _rms(x, pre_ref[...]).astype(BF16)
    d_ff = wg_ref.shape[1]
    acc = jnp.zeros(x.shape, F32)
    for c in range(d_ff // f_chunk):
        sl = slice(c * f_chunk, (c + 1) * f_chunk)
        g = _dot(h, wg_ref[:, sl])
        u = _dot(h, wu_ref[:, sl])
        acc = acc + _dot((_silu(g) * u).astype(BF16), wd_ref[sl, :])
    o_ref[...] = x + 0.5 * _rms(acc, post_ref[...])


def _ffn(x, pre, wg, wu, wd, post, *, tm):
    t, d = x.shape
    d_ff = wg.shape[1]
    f_chunk = 256 if d_ff % 256 == 0 else d_ff
    return pl.pallas_call(
        functools.partial(_ffn_body, f_chunk=f_chunk),
        grid=(t // tm,),
        in_specs=[pl.BlockSpec((tm, d), lambda i: (i, 0)),
                  _const_spec((1, d)), _const_spec((d, d_ff)), _const_spec((d, d_ff)),
                  _const_spec((d_ff, d)), _const_spec((1, d))],
        out_specs=pl.BlockSpec((tm, d), lambda i: (i, 0)),
        out_shape=jax.ShapeDtypeStruct((t, d), F32),
        compiler_params=_params(("parallel",)),
        name="ffn",
    )(x, pre, wg, wu, wd, post)


_PROJ_OUT = (("qkv", H_A * (2 * DK_A + DV_A), F32), ("z", H_A * DV_A, F32),
             ("q", H_B * 2 * DQK_B, BF16), ("k", H_B * 2 * DQK_B, F32), ("v", H_B * DV_B, F32),
             ("ga", None, F32), ("gb", None, F32), ("ba", LANES, F32))


def _proj_body(x_ref, g_ref, w_ref, *o_refs, groups):
    h = _rms(x_ref[...], g_ref[...]).astype(BF16)
    refs = dict(zip([n for n, _, _ in groups] + ["k16", "v16"], o_refs))
    off = 0
    for name, w, _ in groups:
        y = _dot(h, w_ref[:, off:off + w])
        refs[name][...] = y.astype(refs[name].dtype)
        if name in ("k", "v"):
            refs[name + "16"][...] = y.astype(BF16)
        off += w


def _proj(x, g, w_r, groups, *, tm):
    t, d = x.shape
    outs = [(n, w, dt) for n, w, dt in groups] + [("k16", groups[3][1], BF16), ("v16", groups[4][1], BF16)]
    res = pl.pallas_call(
        functools.partial(_proj_body, groups=groups),
        grid=(t // tm,),
        in_specs=[pl.BlockSpec((tm, d), lambda i: (i, 0)), _const_spec((1, d)), _const_spec(w_r.shape)],
        out_specs=[pl.BlockSpec((tm, w), lambda i: (i, 0)) for _, w, _ in outs],
        out_shape=[jax.ShapeDtypeStruct((t, w), dt) for _, w, dt in outs],
        compiler_params=_params(("parallel",)),
        name="proj",
    )(x, g, w_r)
    return dict(zip([n for n, _, _ in outs], res))


def _gdn_body_v1(qkv_ref, cst_ref, z_ref, ba_ref, bat_ref, s0_ref, cw_ref, prow_ref, pcol_ref, gn_ref,
              ya_ref, s_ref, ext_ref, *, chunk, l_valid):
    c = chunk
    j = pl.program_id(1)

    @pl.when(j == 0)
    def _():
        ext_ref[0:SUBLANES, :] = cst_ref[0]
        s_ref[0] = s0_ref[0]

    u_raw = qkv_ref[0]
    ext_ref[SUBLANES:SUBLANES + c, :] = u_raw
    cw = cw_ref[...]
    conv = u_raw * cw[CONV_W - 1:CONV_W, :]
    for k in range(1, CONV_W):
        conv = conv + ext_ref[SUBLANES - k:SUBLANES - k + c, :] * cw[CONV_W - 1 - k:CONV_W - k, :]
    ext_ref[0:SUBLANES, :] = ext_ref[c:c + SUBLANES, :]
    act = _silu(conv)

    ba = ba_ref[0]
    beta_c = jax.nn.sigmoid(ba)
    g_c = -jnp.exp(prow_ref[0:1, :]) * jax.nn.softplus(ba + prow_ref[1:2, :])
    bat = bat_ref[0, 0]
    g_r = -jnp.exp(pcol_ref[0][:, :c]) * jax.nn.softplus(bat + pcol_ref[1][:, :c])
    if l_valid < c:
        g_c = jnp.where(lax.broadcasted_iota(jnp.int32, g_c.shape, 0) < l_valid, g_c, 0.0)
        beta_c = jnp.where(lax.broadcasted_iota(jnp.int32, beta_c.shape, 0) < l_valid, beta_c, 0.0)
        g_r = jnp.where(lax.broadcasted_iota(jnp.int32, g_r.shape, 1) < l_valid, g_r, 0.0)

    ii = lax.broadcasted_iota(jnp.int32, (c, c), 0)
    jj = lax.broadcasted_iota(jnp.int32, (c, c), 1)
    causal = ii >= jj
    strict = ii > jj
    ltri = causal.astype(F32)
    eye = (ii == jj).astype(F32)
    gcum_c = jnp.dot(ltri, g_c, precision=HIGHEST, preferred_element_type=F32)
    gcum_r = _dot_nt(g_r, ltri, precision=HIGHEST)

    hk = H_A * DK_A
    for h in range(H_A):
        q_raw = act[:, h * DK_A:(h + 1) * DK_A]
        k_raw = act[:, hk + h * DK_A:hk + (h + 1) * DK_A]
        v = act[:, 2 * hk + h * DV_A:2 * hk + (h + 1) * DV_A]
        q = q_raw * lax.rsqrt(jnp.sum(q_raw * q_raw, axis=-1, keepdims=True) + EPS) * (DK_A ** -0.5)
        k = k_raw * lax.rsqrt(jnp.sum(k_raw * k_raw, axis=-1, keepdims=True) + EPS)
        beta = beta_c[:, h:h + 1]
        gc = gcum_c[:, H_A + h:H_A + h + 1]
        gr = gcum_r[H_A + h:H_A + h + 1, :]
        decay = jnp.where(causal, jnp.exp(jnp.where(causal, gc - gr, 0.0)), 0.0)
        kb = k * beta
        k16 = k.astype(BF16)
        a = jnp.where(strict, _dot_nt(kb.astype(BF16), k16) * decay, 0.0)
        p = -a
        t_inv = eye + p
        n = 2
        while n < c:
            p = jnp.dot(p, p, precision=HIGHEST, preferred_element_type=F32)
            t_inv = t_inv + jnp.dot(t_inv, p, precision=HIGHEST, preferred_element_type=F32)
            n *= 2
        t16 = t_inv.astype(BF16)
        eg = jnp.exp(gc)
        u = _dot(t16, (v * beta).astype(BF16))
        w = _dot(t16, (kb * eg).astype(BF16))
        qk = _dot_nt(q.astype(BF16), k16) * decay
        s_old = s_ref[0, h]
        s16 = s_old.astype(BF16)
        v_new = u - _dot(w.astype(BF16), s16)
        v_new16 = v_new.astype(BF16)
        o = _dot((q * eg).astype(BF16), s16) + _dot(qk.astype(BF16), v_new16)
        g_last = gc[c - 1:c, :]
        k_dec = k * jnp.exp(g_last - gc)
        s_ref[0, h] = s_old * jnp.exp(g_last) + lax.dot_general(
            k_dec.astype(BF16), v_new16, (((0,), (0,)), ((), ())), preferred_element_type=F32)
        zh = z_ref[0, :, h * DV_A:(h + 1) * DV_A]
        ya_ref[0, :, h * DV_A:(h + 1) * DV_A] = (_rms(o, gn_ref[...]) * _silu(zh)).astype(ya_ref.dtype)


def _gdn_v1(qkv, conv_state, z, ba, s0, conv_w, a_log, dt_bias, g_norm, *, chunk, l_valid):
    b, l, cc = qkv.shape
    n = l // chunk
    assert l_valid == chunk or n == 1
    cst = jnp.pad(conv_state, ((0, 0), (SUBLANES - (CONV_W - 1), 0), (0, 0)))
    bat = ba[..., :2 * H_A].reshape(b, n, chunk, 2 * H_A).transpose(0, 1, 3, 2)
    zeros_h = jnp.zeros((H_A,), F32)
    lane_pad = jnp.zeros((LANES - 2 * H_A,), F32)
    prow = jnp.stack([jnp.concatenate([zeros_h, a_log, lane_pad]),
                      jnp.concatenate([zeros_h, dt_bias, lane_pad])])
    pcol = jnp.broadcast_to(prow[:, :2 * H_A, None], (2, 2 * H_A, LANES))
    return pl.pallas_call(
        functools.partial(_gdn_body, chunk=chunk, l_valid=l_valid),
        grid=(b, n),
        in_specs=[pl.BlockSpec((1, chunk, cc), lambda i, j: (i, j, 0)),
                  pl.BlockSpec((1, SUBLANES, cc), lambda i, j: (i, 0, 0)),
                  pl.BlockSpec((1, chunk, H_A * DV_A), lambda i, j: (i, j, 0)),
                  pl.BlockSpec((1, chunk, LANES), lambda i, j: (i, j, 0)),
                  pl.BlockSpec((1, 1, 2 * H_A, chunk), lambda i, j: (i, j, 0, 0)),
                  pl.BlockSpec((1, H_A, DK_A, DV_A), lambda i, j: (i, 0, 0, 0)),
                  _const_spec((CONV_W, cc)), _const_spec((2, LANES)),
                  _const_spec((2, 2 * H_A, LANES)), _const_spec((1, DV_A))],
        out_specs=[pl.BlockSpec((1, chunk, H_A * DV_A), lambda i, j: (i, j, 0)),
                   pl.BlockSpec((1, H_A, DK_A, DV_A), lambda i, j: (i, 0, 0, 0))],
        out_shape=[jax.ShapeDtypeStruct((b, l, H_A * DV_A), BF16),
                   jax.ShapeDtypeStruct((b, H_A, DK_A, DV_A), F32)],
        scratch_shapes=[pltpu.VMEM((SUBLANES + chunk, cc), F32)],
        compiler_params=_params(("parallel", "arbitrary")),
        name="gdn",
    )(qkv, cst, z, ba, bat, s0, conv_w, prow, pcol, g_norm.reshape(1, DV_A))


def _split2(x):
    hi = x.astype(BF16)
    return hi, (x - hi.astype(F32)).astype(BF16)


def _split3(x):
    hi = x.astype(BF16)
    r = x - hi.astype(F32)
    mid = r.astype(BF16)
    return hi, mid, (r - mid.astype(F32)).astype(BF16)


def _mm3(x, y):
    x_hi, x_lo = _split2(x)
    y_hi, y_lo = _split2(y)
    return _dot(x_hi, y_hi) + _dot(x_lo, y_hi) + _dot(x_hi, y_lo)


def _neumann_level(p, t_inv, c):
    if c % 16 == 0:
        x_hi, x_lo = _split2(jnp.concatenate([p, t_inv], axis=0))
        p_hi, p_lo = x_hi[:c], x_lo[:c]
        y = _dot(jnp.concatenate([x_hi, x_lo], axis=0), p_hi)
        y = y[:2 * c] + y[2 * c:] + _dot(x_hi, p_lo)
        return y[:c], t_inv + y[c:]
    p_hi, p_lo = _split2(p)
    t_hi, t_lo = _split2(t_inv)
    pp = _dot(p_hi, p_hi) + _dot(p_lo, p_hi) + _dot(p_hi, p_lo)
    tp = _dot(t_hi, p_hi) + _dot(t_lo, p_hi) + _dot(t_hi, p_lo)
    return pp, t_inv + tp


def _gdn_body(qkv_ref, cst_ref, z_ref, ba_ref, bat_ref, s0_ref, cw_ref, prow_ref, pcol_ref, gn_ref,
              ya_ref, s_ref, ext_ref, *, chunk, n_chunk, n_seq, l_valid):
    c = chunk
    tb = n_chunk * c
    j = pl.program_id(1)

    @pl.when(j == 0)
    def _():
        ext_ref[:, 0:SUBLANES, :] = cst_ref[...]
        s_ref[...] = s0_ref[...]

    ii = lax.broadcasted_iota(jnp.int32, (c, c), 0)
    jj = lax.broadcasted_iota(jnp.int32, (c, c), 1)
    causal = ii >= jj
    strict = ii > jj
    ltri16 = causal.astype(BF16)
    eye = (ii == jj).astype(F32)
    cw = cw_ref[...]
    hk = H_A * DK_A

    rows = lambda n: slice(n * c, (n + 1) * c)
    act, beta_c, g_c = [], [], []
    for s in range(n_seq):
        u_raw = qkv_ref[s]
        ext_ref[s, SUBLANES:SUBLANES + tb, :] = u_raw
        conv = u_raw * cw[CONV_W - 1:CONV_W, :]
        for k in range(1, CONV_W):
            conv = conv + ext_ref[s, SUBLANES - k:SUBLANES - k + tb, :] * cw[CONV_W - 1 - k:CONV_W - k, :]
        ext_ref[s, 0:SUBLANES, :] = ext_ref[s, tb:tb + SUBLANES, :]
        act.append(_silu(conv))
        ba = ba_ref[s]
        beta_s = jax.nn.sigmoid(ba)
        g_s = -jnp.exp(prow_ref[0:1, :]) * jax.nn.softplus(ba + prow_ref[1:2, :])
        if l_valid < c:
            valid = lax.broadcasted_iota(jnp.int32, g_s.shape, 0) < l_valid
            g_s = jnp.where(valid, g_s, 0.0)
            beta_s = jnp.where(valid, beta_s, 0.0)
        beta_c.append(beta_s)
        g_c.append(g_s)

    blocks = [(s, n) for s in range(n_seq) for n in range(n_chunk)]
    chains = [(s, n, h) for s, n in blocks for h in range(H_A)]

    gcum_c, gcum_r = {}, {}
    for s, n in blocks:
        g_r = -jnp.exp(pcol_ref[0][:, :c]) * jax.nn.softplus(bat_ref[s, n] + pcol_ref[1][:, :c])
        if l_valid < c:
            g_r = jnp.where(lax.broadcasted_iota(jnp.int32, g_r.shape, 1) < l_valid, g_r, 0.0)
        gcum_c[s, n] = functools.reduce(jnp.add, [_dot(ltri16, x) for x in _split3(g_c[s][rows(n)])])
        gcum_r[s, n] = functools.reduce(jnp.add, [_dot_nt(x, ltri16) for x in _split3(g_r)])

    st = {}
    for ch in chains:
        s, n, h = ch
        rs = rows(n)
        q_raw = act[s][rs, h * DK_A:(h + 1) * DK_A]
        k_raw = act[s][rs, hk + h * DK_A:hk + (h + 1) * DK_A]
        v = act[s][rs, 2 * hk + h * DV_A:2 * hk + (h + 1) * DV_A]
        q = q_raw * lax.rsqrt(jnp.sum(q_raw * q_raw, axis=-1, keepdims=True) + EPS) * (DK_A ** -0.5)
        k = k_raw * lax.rsqrt(jnp.sum(k_raw * k_raw, axis=-1, keepdims=True) + EPS)
        beta = beta_c[s][rs, h:h + 1]
        gc = gcum_c[s, n][:, H_A + h:H_A + h + 1]
        gr = gcum_r[s, n][H_A + h:H_A + h + 1, :]
        decay = jnp.where(causal, jnp.exp(jnp.where(causal, gc - gr, 0.0)), 0.0)
        kb = k * beta
        kq = _dot_nt(jnp.concatenate([kb, q], axis=0).astype(BF16), k.astype(BF16))
        eg = jnp.exp(gc)
        g_last = gc[c - 1:c, :]
        p = -jnp.where(strict, kq[:c] * decay, 0.0)
        st[ch] = dict(p=p, t=eye + p, qk=(kq[c:] * decay).astype(BF16),
                      rhs=jnp.concatenate([v * beta, kb * eg], axis=1).astype(BF16), qe=q * eg,
                      k_dec=(k * jnp.exp(g_last - gc)).astype(BF16), dec=jnp.exp(g_last))

    n_factor = c.bit_length() - 1
    if n_factor >= 2:
        for e in st.values():
            e["p"] = _mm3(e["p"], e["p"])
        for _ in range(n_factor - 2):
            for e in st.values():
                e["p"], e["t"] = _neumann_level(e["p"], e["t"], c)
        for e in st.values():
            e["t"] = e["t"] + _mm3(e["t"], e["p"])
    for e in st.values():
        uw = _dot(e["t"].astype(BF16), e["rhs"])
        e["u"] = uw[:, :DV_A]
        e["wq"] = jnp.concatenate([uw[:, DV_A:], e["qe"]], axis=0).astype(BF16)

    state = {(s, h): s_ref[s, h] for s in range(n_seq) for h in range(H_A)}
    for n in range(n_chunk):
        grp = [(s, h) for s in range(n_seq) for h in range(H_A)]
        ws = {g: _dot(st[g[0], n, g[1]]["wq"], state[g].astype(BF16)) for g in grp}
        v_new = {g: (st[g[0], n, g[1]]["u"] - ws[g][:c]).astype(BF16) for g in grp}
        for g in grp:
            s, h = g
            e = st[s, n, h]
            o = ws[g][c:] + _dot(e["qk"], v_new[g])
            state[g] = state[g] * e["dec"] + lax.dot_general(
                e["k_dec"], v_new[g], (((0,), (0,)), ((), ())), preferred_element_type=F32)
            zh = z_ref[s, rows(n), h * DV_A:(h + 1) * DV_A]
            ya_ref[s, rows(n), h * DV_A:(h + 1) * DV_A] = (_rms(o, gn_ref[...]) * _silu(zh)).astype(ya_ref.dtype)
    for (s, h), val in state.items():
        s_ref[s, h] = val


def _gdn(qkv, conv_state, z, ba, s0, conv_w, a_log, dt_bias, g_norm, *, chunk, n_chunk, n_seq, l_valid):
    b, l, cc = qkv.shape
    n = l // chunk
    tb = n_chunk * chunk
    assert l_valid == chunk or n == 1
    assert l % tb == 0 and b % n_seq == 0
    assert chunk & (chunk - 1) == 0
    cst = jnp.pad(conv_state, ((0, 0), (SUBLANES - (CONV_W - 1), 0), (0, 0)))
    bat = ba[..., :2 * H_A].reshape(b, n, chunk, 2 * H_A).transpose(0, 1, 3, 2)
    zeros_h = jnp.zeros((H_A,), F32)
    lane_pad = jnp.zeros((LANES - 2 * H_A,), F32)
    prow = jnp.stack([jnp.concatenate([zeros_h, a_log, lane_pad]),
                      jnp.concatenate([zeros_h, dt_bias, lane_pad])])
    pcol = jnp.broadcast_to(prow[:, :2 * H_A, None], (2, 2 * H_A, LANES))
    return pl.pallas_call(
        functools.partial(_gdn_body, chunk=chunk, n_chunk=n_chunk, n_seq=n_seq, l_valid=l_valid),
        grid=(b // n_seq, l // tb),
        in_specs=[pl.BlockSpec((n_seq, tb, cc), lambda i, j: (i, j, 0)),
                  pl.BlockSpec((n_seq, SUBLANES, cc), lambda i, j: (i, 0, 0)),
                  pl.BlockSpec((n_seq, tb, H_A * DV_A), lambda i, j: (i, j, 0)),
                  pl.BlockSpec((n_seq, tb, LANES), lambda i, j: (i, j, 0)),
                  pl.BlockSpec((n_seq, n_chunk, 2 * H_A, chunk), lambda i, j: (i, j, 0, 0)),
                  pl.BlockSpec((n_seq, H_A, DK_A, DV_A), lambda i, j: (i, 0, 0, 0)),
                  _const_spec((CONV_W, cc)), _const_spec((2, LANES)),
                  _const_spec((2, 2 * H_A, LANES)), _const_spec((1, DV_A))],
        out_specs=[pl.BlockSpec((n_seq, tb, H_A * DV_A), lambda i, j: (i, j, 0)),
                   pl.BlockSpec((n_seq, H_A, DK_A, DV_A), lambda i, j: (i, 0, 0, 0))],
        out_shape=[jax.ShapeDtypeStruct((b, l, H_A * DV_A), BF16),
                   jax.ShapeDtypeStruct((b, H_A, DK_A, DV_A), F32)],
        scratch_shapes=[pltpu.VMEM((n_seq, SUBLANES + tb, cc), F32)],
        compiler_params=_params(("parallel", "arbitrary")),
        name="gdn",
    )(qkv, cst, z, ba, bat, s0, conv_w, prow, pcol, g_norm.reshape(1, DV_A))


def _rel_bucket(n):
    n = jnp.maximum(n, 0)
    max_exact = NUM_BUCKETS // 2
    large = max_exact + (jnp.log(jnp.maximum(n, 1).astype(F32) / max_exact)
                         / math.log(MAX_DISTANCE / max_exact) * (NUM_BUCKETS - max_exact)).astype(jnp.int32)
    large = jnp.minimum(large, NUM_BUCKETS - 1)
    return jnp.where(n < max_exact, n, large)


def _bias_rows(rel, tab_rows):
    onehot = _rel_bucket(rel)[..., None] == jnp.arange(NUM_BUCKETS)
    b = jnp.sum(jnp.where(onehot, tab_rows[..., None, :], 0.0), axis=-1)
    return jnp.where(rel >= 0, b, NEG_INF)


def _lam_of(lp_ref, lam_init):
    lp = lp_ref[...]
    s1 = jnp.sum(lp[0:1] * lp[1:2], axis=-1, keepdims=True)
    s2 = jnp.sum(lp[2:3] * lp[3:4], axis=-1, keepdims=True)
    return jnp.exp(s1) - jnp.exp(s2) + lam_init


def _attn_prompt_body(q_ref, k_ref, v_ref, tiles_ref, lp_ref, sub_ref, o_ref,
                      qq_ref, m_ref, l_ref, acc_ref, *, t, row_block, lam_init):
    i = pl.program_id(2)
    q = q_ref[0]
    lane = lax.broadcasted_iota(jnp.int32, q.shape, 1)
    scale = jnp.asarray(DQK_B ** -0.5, BF16)
    zero = jnp.zeros_like(q)
    qq_ref[0:t, :] = jnp.where(lane < DQK_B, q * scale, zero)
    qq_ref[t:2 * t, :] = jnp.where(lane >= DQK_B, q * scale, zero)
    m_ref[...] = jnp.full(m_ref.shape, NEG_INF, F32)
    l_ref[...] = jnp.zeros(l_ref.shape, F32)
    acc_ref[...] = jnp.zeros(acc_ref.shape, F32)
    n_col = t // LANES

    def step(j, tile):
        start = pl.multiple_of(j * t, t)
        k = k_ref[0, pl.ds(start, t), :]
        v = v_ref[0, pl.ds(start, t), :]
        starts = list(range(0, 2 * t, row_block))
        qk = lambda r: _dot_nt(qq_ref[r:r + row_block, :], k)
        scores = {r: qk(r) for r in starts[:ATTN_LOOKAHEAD]}
        for idx, r0 in enumerate(starts):
            rows = slice(r0, r0 + row_block)
            if idx + ATTN_LOOKAHEAD < len(starts):
                scores[starts[idx + ATTN_LOOKAHEAD]] = qk(starts[idx + ATTN_LOOKAHEAD])
            s = scores.pop(r0)
            if tile is not None:
                s = s + tiles_ref[0, tile, r0 % t:r0 % t + row_block, :]
            cols = [s[:, c * LANES:(c + 1) * LANES] for c in range(n_col)]
            m_prev = m_ref[rows, :]
            m_new = jnp.maximum(m_prev, jnp.max(functools.reduce(jnp.maximum, cols), axis=-1, keepdims=True))
            alpha = jnp.exp(m_prev - m_new)
            ps = [jnp.exp(c - m_new) for c in cols]
            l_ref[rows, :] = alpha * l_ref[rows, :] + functools.reduce(jnp.add, ps)
            p16 = jnp.concatenate([p.astype(BF16) for p in ps], axis=-1)
            acc_ref[rows, :] = alpha * acc_ref[rows, :] + _dot(p16, v)
            m_ref[rows, :] = m_new

    def far(j, carry):
        step(j, None)
        return carry

    lax.fori_loop(0, jnp.maximum(i - 1, 0), far, 0)

    @pl.when(i >= 1)
    def _():
        step(i - 1, 1)

    step(i, 0)

    lam = _lam_of(lp_ref, lam_init)
    acc = acc_ref[...] / jnp.sum(l_ref[...], axis=-1, keepdims=True)
    o = acc[0:t] - lam * acc[t:2 * t]
    o_ref[0] = (_rms(o, sub_ref[...]) * (1.0 - lam_init)).astype(o_ref.dtype)


def _attn_prompt(q, k, v, rel_table, lam_params, subln, *, t, lam_init):
    b, l, _ = q.shape
    nq = l // t
    assert t >= MAX_DISTANCE
    table = rel_table.astype(F32)
    d = jnp.arange(t)[:, None] - jnp.arange(t)[None, :]
    rel = jnp.broadcast_to(jnp.stack([d, d + t]), (H_B, 2, t, t))
    tab_rows = jnp.broadcast_to(table.T[:, None, None, :], (H_B, 2, t, NUM_BUCKETS))
    tiles = _bias_rows(rel, tab_rows) - table[NUM_BUCKETS - 1][:, None, None, None]
    hw = 2 * DQK_B
    return pl.pallas_call(
        functools.partial(_attn_prompt_body, t=t, row_block=min(ATTN_ROW_BLOCK, t), lam_init=lam_init),
        grid=(b, H_B, nq),
        in_specs=[pl.BlockSpec((1, t, hw), lambda bi, h, i: (bi, i, h)),
                  pl.BlockSpec((1, l, hw), lambda bi, h, i: (bi, 0, h)),
                  pl.BlockSpec((1, l, DV_B), lambda bi, h, i: (bi, 0, h)),
                  pl.BlockSpec((1, 2, t, t), lambda bi, h, i: (h, 0, 0, 0)),
                  _const_spec(lam_params.shape), _const_spec((1, DV_B))],
        out_specs=pl.BlockSpec((1, t, DV_B), lambda bi, h, i: (bi, i, h)),
        out_shape=jax.ShapeDtypeStruct((b, l, H_B * DV_B), BF16),
        scratch_shapes=[pltpu.VMEM((2 * t, hw), BF16), pltpu.VMEM((2 * t, LANES), F32),
                        pltpu.VMEM((2 * t, LANES), F32), pltpu.VMEM((2 * t, DV_B), F32)],
        compiler_params=_params(("parallel", "parallel", "arbitrary")),
        name="attn_prompt",
    )(q, k, v, tiles, lam_params, subln.reshape(1, DV_B))


def _attn_prompt_t_body(q_ref, k_ref, vt_ref, tiles_ref, lp_ref, sub_ref, o_ref,
                        qt_ref, m_ref, l_ref, acc_ref, *, t, col_block, lam_init):
    i = pl.program_id(2)
    n2 = 2 * t
    qt = q_ref[0].astype(F32).T * (DQK_B ** -0.5)
    dim = lax.broadcasted_iota(jnp.int32, qt.shape, 0)
    qt_ref[:, 0:t] = jnp.where(dim < DQK_B, qt, 0.0).astype(BF16)
    qt_ref[:, t:n2] = jnp.where(dim >= DQK_B, qt, 0.0).astype(BF16)
    m_ref[...] = jnp.full(m_ref.shape, NEG_INF, F32)
    l_ref[...] = jnp.zeros(l_ref.shape, F32)
    acc_ref[...] = jnp.zeros(acc_ref.shape, F32)
    groups = t // SUBLANES

    def run(blocks):
        kv = []
        for j, _ in blocks:
            start = pl.multiple_of(j * t, t)
            kv.append((k_ref[0, pl.ds(start, t), :], vt_ref[0, 0, j]))
        items = [(b, c0) for b in range(len(blocks)) for c0 in range(0, n2, col_block)]
        qk = lambda it: _dot(kv[it[0]][0], qt_ref[:, it[1]:it[1] + col_block])
        scores = {it: qk(it) for it in items[:ATTN_LOOKAHEAD]}
        for idx, it in enumerate(items):
            b, c0 = it
            tile = blocks[b][1]
            cols = slice(c0, c0 + col_block)
            if idx + ATTN_LOOKAHEAD < len(items):
                nxt = items[idx + ATTN_LOOKAHEAD]
                scores[nxt] = qk(nxt)
            s = scores.pop(it)
            if tile is not None:
                s = s + tiles_ref[0, tile, :, c0 % t:c0 % t + col_block]
            chunk3 = lambda r0: s[r0:r0 + ATTN_KEY_CHUNK].reshape(ATTN_KEY_CHUNK // SUBLANES, SUBLANES, col_block)
            key_chunks = range(0, t, ATTN_KEY_CHUNK)
            m_prev = m_ref[:, cols]
            mx = functools.reduce(jnp.maximum, [jnp.max(chunk3(r0), axis=0) for r0 in key_chunks])
            m_new = jnp.maximum(m_prev, jnp.max(mx, axis=0, keepdims=True))
            alpha = jnp.exp(m_prev - m_new)
            l_new = alpha * l_ref[:, cols]
            p16 = []
            for r0 in key_chunks:
                p3 = jnp.exp(chunk3(r0) - m_new[None])
                l_new = l_new + jnp.sum(p3, axis=0)
                p16.append(p3.reshape(ATTN_KEY_CHUNK, col_block).astype(BF16))
            l_ref[:, cols] = l_new
            pv = _dot(kv[b][1], jnp.concatenate(p16, axis=0))
            acc = acc_ref[:, cols].reshape(DV_B // SUBLANES, SUBLANES, col_block) * alpha[None]
            acc_ref[:, cols] = acc.reshape(DV_B, col_block) + pv
            m_ref[:, cols] = m_new

    n_far = jnp.maximum(i - 1, 0)

    def far_pair(jj, carry):
        run([(2 * jj, None), (2 * jj + 1, None)])
        return carry

    lax.fori_loop(0, n_far // 2, far_pair, 0)

    @pl.when(n_far % 2 == 1)
    def _():
        run([(n_far - 1, None)])

    @pl.when(i >= 1)
    def _():
        run([(i - 1, 1), (i, 0)])

    @pl.when(i == 0)
    def _():
        run([(i, 0)])

    lam = _lam_of(lp_ref, lam_init)
    acc = acc_ref[...] / jnp.sum(l_ref[...], axis=0, keepdims=True)
    o = (acc[:, 0:t] - lam * acc[:, t:n2]).T
    o_ref[0] = (_rms(o, sub_ref[...]) * (1.0 - lam_init)).astype(o_ref.dtype)


def _attn_prompt_t(q, k, v, rel_table, lam_params, subln, *, t, lam_init):
    b, l, _ = q.shape
    nq = l // t
    assert t >= MAX_DISTANCE
    table = rel_table.astype(F32)
    d = jnp.arange(t)[None, :] - jnp.arange(t)[:, None]
    rel = jnp.broadcast_to(jnp.stack([d, d + t]), (H_B, 2, t, t))
    tab_rows = jnp.broadcast_to(table.T[:, None, None, :], (H_B, 2, t, NUM_BUCKETS))
    tiles = _bias_rows(rel, tab_rows) - table[NUM_BUCKETS - 1][:, None, None, None]
    vt = v.reshape(b, nq, t, H_B, DV_B).transpose(0, 3, 1, 4, 2)
    hw = 2 * DQK_B
    return pl.pallas_call(
        functools.partial(_attn_prompt_t_body, t=t, col_block=min(ATTN_COL_BLOCK, t), lam_init=lam_init),
        grid=(b, H_B, nq),
        in_specs=[pl.BlockSpec((1, t, hw), lambda bi, h, i: (bi, i, h)),
                  pl.BlockSpec((1, l, hw), lambda bi, h, i: (bi, 0, h)),
                  pl.BlockSpec((1, 1, nq, DV_B, t), lambda bi, h, i: (bi, h, 0, 0, 0)),
                  pl.BlockSpec((1, 2, t, t), lambda bi, h, i: (h, 0, 0, 0)),
                  _const_spec(lam_params.shape), _const_spec((1, DV_B))],
        out_specs=pl.BlockSpec((1, t, DV_B), lambda bi, h, i: (bi, i, h)),
        out_shape=jax.ShapeDtypeStruct((b, l, H_B * DV_B), BF16),
        scratch_shapes=[pltpu.VMEM((hw, 2 * t), BF16), pltpu.VMEM((SUBLANES, 2 * t), F32),
                        pltpu.VMEM((SUBLANES, 2 * t), F32), pltpu.VMEM((DV_B, 2 * t), F32)],
        compiler_params=_params(("parallel", "parallel", "arbitrary")),
        name="attn_prompt",
    )(q, k, vt, tiles, lam_params, subln.reshape(1, DV_B))


def _attn_sample_body(pt_ref, q_ref, kn_ref, vn_ref, bpast_ref, bself_ref, lp_ref, sub_ref, *rest,
                      pages, lam_init):
    k_refs = rest[:pages]
    v_refs = rest[pages:2 * pages]
    o_ref, qf_ref, m_ref, l_ref, acc_ref = rest[2 * pages:]
    j = pl.program_id(1)
    rows = qf_ref.shape[0]
    half = rows // 2
    page_cols = PAGE_SIZE * H_B

    @pl.when(j == 0)
    def _():
        q = q_ref[0].astype(F32)
        r = lax.broadcasted_iota(jnp.int32, q.shape, 0)
        lane = lax.broadcasted_iota(jnp.int32, q.shape, 1)
        qf = jnp.where(lane // DQK_B == r // half, q * (DQK_B ** -0.5), 0.0)
        qf_ref[...] = qf
        kn = kn_ref[0]
        vn = vn_ref[0]
        n_self = kn.shape[0]
        s_self = [jnp.sum(qf * kn[c:c + 1], axis=-1, keepdims=True) + bself_ref[:, c:c + 1]
                  for c in range(n_self)]
        m0 = functools.reduce(jnp.maximum, s_self)
        p_self = [jnp.exp(s - m0) for s in s_self]
        m_ref[...] = m0
        l_ref[...] = functools.reduce(jnp.add, p_self)
        acc_ref[...] = functools.reduce(jnp.add, [p * vn[c:c + 1] for c, p in enumerate(p_self)])

    q16 = qf_ref[...].astype(BF16)
    group = math.gcd(pages, SAMPLE_PAGE_GROUP)
    groups = [range(g, g + group) for g in range(0, pages, group)]
    qk = lambda grp: [_dot_nt(q16, k_refs[c][0].astype(BF16)) for c in grp]
    scores = qk(groups[0])
    m_run, l_run, acc = m_ref[...], l_ref[...], acc_ref[...]
    for gi, grp in enumerate(groups):
        cur = scores
        if gi + 1 < len(groups):
            scores = qk(groups[gi + 1])
        s = (jnp.concatenate(cur, axis=-1)
             + bpast_ref[j, :, grp[0] * page_cols:(grp[-1] + 1) * page_cols])
        m_new = jnp.maximum(m_run, jnp.max(s, axis=-1, keepdims=True))
        alpha = jnp.exp(m_run - m_new)
        p = jnp.exp(s - m_new)
        l_run = alpha * l_run + jnp.sum(p, axis=-1, keepdims=True)
        p16 = p.astype(BF16)
        pv = functools.reduce(jnp.add, [_dot(p16[:, i * page_cols:(i + 1) * page_cols], v_refs[c][0].astype(BF16))
                                        for i, c in enumerate(grp)])
        acc = alpha * acc + pv
        m_run = m_new
    m_ref[...], l_ref[...], acc_ref[...] = m_run, l_run, acc

    @pl.when(j == pl.num_programs(1) - 1)
    def _():
        lam = _lam_of(lp_ref, lam_init)
        an = acc_ref[...] / l_ref[...]
        o = an[0:half] - lam * an[half:rows]
        o_ref[0] = _rms(o, sub_ref[...]) * (1.0 - lam_init)


def _attn_sample(q, k_new, v_new, cache_k, cache_v, page_base, page_table, rel_table, lam_params, subln,
                 *, pages, lam_init):
    bd, l_new, width = q.shape
    n_pages = page_table.shape[1]
    past = n_pages * PAGE_SIZE
    half = l_new * H_B
    rows = 2 * half
    page_cols = PAGE_SIZE * H_B
    table = rel_table.astype(F32)
    t_of = (jnp.arange(rows) % half) // H_B
    h_of = jnp.arange(rows) % H_B
    tab_rows = table.T[h_of]
    far = table[NUM_BUCKETS - 1][h_of][:, None]
    own = h_of[:, None, None] == jnp.arange(H_B)[None, None, :]
    rel_past = past + t_of[:, None] - jnp.arange(past)[None, :]
    bpast = jnp.where(own, (_bias_rows(rel_past, tab_rows) - far)[:, :, None], NEG_INF)
    n_steps = n_pages // pages
    bpast = bpast.reshape(rows, n_steps, pages * page_cols).transpose(1, 0, 2)
    n_self = l_new * H_B
    t_key = jnp.arange(LANES) // H_B
    rel_self = jnp.where((jnp.arange(LANES)[None, :] < n_self) & (h_of[:, None] == jnp.arange(LANES)[None, :] % H_B),
                         t_of[:, None] - t_key[None, :], -1)
    bself = _bias_rows(rel_self, tab_rows) - far

    hd = width // H_B
    q_rows = q.reshape(bd, half, hd)
    q_rows = jnp.concatenate([q_rows, q_rows], axis=1)
    kn = k_new.reshape(bd, n_self, hd)
    vn = v_new.reshape(bd, n_self, hd)

    def page_spec(c):
        return pl.BlockSpec((1, page_cols, hd),
                            lambda b, j, pt: (page_base + pt[b * n_pages + j * pages + c], 0, 0))

    seq_spec = lambda r: pl.BlockSpec((1, r, hd), lambda b, j, pt: (b, 0, 0))
    whole = lambda shape: pl.BlockSpec(shape, lambda b, j, pt: (0,) * len(shape), pipeline_mode=pl.Buffered(1))
    grid_spec = pltpu.PrefetchScalarGridSpec(
        num_scalar_prefetch=1,
        grid=(bd, n_pages // pages),
        in_specs=[seq_spec(rows), seq_spec(n_self), seq_spec(n_self),
                  whole(bpast.shape), whole(bself.shape), whole(lam_params.shape), whole((1, DV_B))]
                 + [page_spec(c) for c in range(pages)] + [page_spec(c) for c in range(pages)],
        out_specs=seq_spec(half),
        scratch_shapes=[pltpu.VMEM((rows, hd), F32), pltpu.VMEM((rows, 1), F32),
                        pltpu.VMEM((rows, 1), F32), pltpu.VMEM((rows, DV_B), F32)])
    out = pl.pallas_call(
        functools.partial(_attn_sample_body, pages=pages, lam_init=lam_init),
        grid_spec=grid_spec,
        out_shape=jax.ShapeDtypeStruct((bd, half, DV_B), F32),
        compiler_params=_params(("parallel", "arbitrary")),
        name="attn_sample",
    )(page_table.reshape(-1), q_rows, kn, vn, bpast, bself, lam_params, subln.reshape(1, DV_B),
      *([cache_k] * pages), *([cache_v] * pages))
    return out.reshape(bd, l_new, width)


def _merge_body(x_ref, ya_ref, yb_ref, ga_ref, gb_ref, wa_ref, wb_ref, wo_ref, post_ref, o_ref):
    merged = (jax.nn.sigmoid(ga_ref[...]) * _dot(ya_ref[...], wa_ref[...])
              + jax.nn.sigmoid(gb_ref[...]) * _dot(yb_ref[...], wb_ref[...]))
    o_ref[...] = x_ref[...] + _rms(_dot(merged.astype(BF16), wo_ref[...]), post_ref[...])


def _merge(x, ya, yb, ga, gb, wa, wb, wo, post, *, tm):
    t, d = x.shape
    row = lambda w: pl.BlockSpec((tm, w), lambda i: (i, 0))
    return pl.pallas_call(
        _merge_body,
        grid=(t // tm,),
        in_specs=[row(d), row(ya.shape[1]), row(yb.shape[1]), row(d), row(d),
                  _const_spec(wa.shape), _const_spec(wb.shape), _const_spec(wo.shape), _const_spec((1, d))],
        out_specs=row(d),
        out_shape=jax.ShapeDtypeStruct((t, d), F32),
        compiler_params=_params(("parallel",)),
        name="merge",
    )(x, ya, yb, ga, gb, wa, wb, wo, post)


def _lambda_init(layer):
    return 0.8 - 0.6 * math.exp(-0.3 * layer)


def _token_tile(t):
    tm = 512
    while t % tm:
        tm //= 2
    return tm


def _attn_tile(seq):
    return min(512, seq)


def _layer_weights(l, W):
    d = W["w_in"].shape[1]
    bf = lambda a: a.astype(BF16)
    row = lambda a: a.reshape(1, -1).astype(F32)
    widths = {"qkv": H_A * (2 * DK_A + DV_A), "z": H_A * DV_A, "beta": H_A, "alpha": H_A,
              "q": H_B * 2 * DQK_B, "k": H_B * 2 * DQK_B, "v": H_B * DV_B, "ga": d, "gb": d}
    order = ("qkv", "z", "beta", "alpha", "q", "k", "v", "ga", "gb")
    offs, off = {}, 0
    for n in order:
        offs[n] = off
        off += widths[n]
    w_in = W["w_in"][l]
    col = lambda n: w_in[:, offs[n]:offs[n] + widths[n]]
    pad = jnp.zeros((d, LANES - 2 * H_A), w_in.dtype)
    w_r = jnp.concatenate([col("qkv"), col("z"), col("q"), col("k"), col("v"), col("ga"), col("gb"),
                           col("beta"), col("alpha"), pad], axis=1)
    groups = tuple((n, (d if w is None else w), dt) for n, w, dt in _PROJ_OUT)
    lam_params = jnp.stack([W["lam_q1"][l], W["lam_k1"][l], W["lam_q2"][l], W["lam_k2"][l]]).astype(F32)
    return dict(
        ffn1=(row(W["ffn1_pre"][l]), bf(W["ffn1_wg"][l]), bf(W["ffn1_wu"][l]), bf(W["ffn1_wd"][l]),
              row(W["ffn1_post"][l])),
        ffn2=(row(W["ffn2_pre"][l]), bf(W["ffn2_wg"][l]), bf(W["ffn2_wu"][l]), bf(W["ffn2_wd"][l]),
              row(W["ffn2_post"][l])),
        mix_pre=row(W["mix_pre"][l]), w_r=bf(w_r), groups=groups,
        conv_w=W["conv_w"][l].astype(F32), a_log=W["a_log"][l].astype(F32), dt_bias=W["dt_bias"][l].astype(F32),
        gdn_norm=W["gdn_norm"][l].astype(F32), lam_params=lam_params, subln=W["subln"][l].astype(F32),
        w_a=bf(W["w_a"][l]), w_b=bf(W["w_b"][l]), w_out=bf(W["w_out"][l]), mix_post=row(W["mix_post"][l]),
        lam_init=_lambda_init(l))


def _decoder_layer(x, lw, conv_state, ssm_state, attend):
    b, l, d = x.shape
    t = b * l
    tm = _token_tile(t)
    x2 = _ffn(x.reshape(t, d), *lw["ffn1"], tm=tm)
    c = _proj(x2, lw["mix_pre"], lw["w_r"], lw["groups"], tm=tm)
    c3 = {n: a.reshape(b, l, a.shape[-1]) for n, a in c.items()}

    chunk = min(CHUNK, l)
    if chunk % SUBLANES:
        chunk = -(-chunk // SUBLANES) * SUBLANES
    lp = -(-l // chunk) * chunk
    padl = lambda a: jnp.pad(a, ((0, 0), (0, lp - l), (0, 0))) if lp != l else a
    ya, ssm_new = _gdn(padl(c3["qkv"]), conv_state, padl(c3["z"]), padl(c3["ba"]), ssm_state,
                       lw["conv_w"], lw["a_log"], lw["dt_bias"], lw["gdn_norm"],
                       chunk=chunk, n_chunk=math.gcd(lp // chunk, GDN_CHUNKS_PER_STEP),
                       n_seq=math.gcd(b, GDN_SEQS_PER_STEP) if lp == chunk else 1,
                       l_valid=min(l - (lp - chunk), chunk))
    ya = ya[:, :l]
    if l >= CONV_W - 1:
        conv_new = c3["qkv"][:, l - (CONV_W - 1):]
    else:
        conv_new = jnp.concatenate([conv_state.astype(F32), c3["qkv"]], axis=1)[:, -(CONV_W - 1):]

    yb = attend(c3["q"], c3["k"], c3["v"], c3["k16"], c3["v16"])
    y = _merge(x2, ya.reshape(t, -1), yb.reshape(t, -1).astype(BF16), c["ga"], c["gb"],
               lw["w_a"], lw["w_b"], lw["w_out"], lw["mix_post"], tm=tm)
    y = _ffn(y, *lw["ffn2"], tm=tm)
    return (y.reshape(b, l, d), c3["k"].reshape(b, l, H_B, 2 * DQK_B), c3["v"].reshape(b, l, H_B, DV_B),
            conv_new, ssm_new)


def kernel(x_prompt, x_sample, cache_k, cache_v, state_conv, state_ssm, page_table, rel_table,
           ffn1_pre, ffn1_wg, ffn1_wu, ffn1_wd, ffn1_post, mix_pre, w_in, conv_w, a_log, dt_bias,
           gdn_norm, lam_q1, lam_k1, lam_q2, lam_k2, subln, w_a, w_b, w_out, mix_post,
           ffn2_pre, ffn2_wg, ffn2_wu, ffn2_wd, ffn2_post):
    W = dict(ffn1_pre=ffn1_pre, ffn1_wg=ffn1_wg, ffn1_wu=ffn1_wu, ffn1_wd=ffn1_wd, ffn1_post=ffn1_post,
             mix_pre=mix_pre, w_in=w_in, conv_w=conv_w, a_log=a_log, dt_bias=dt_bias, gdn_norm=gdn_norm,
             lam_q1=lam_q1, lam_k1=lam_k1, lam_q2=lam_q2, lam_k2=lam_k2, subln=subln, w_a=w_a, w_b=w_b,
             w_out=w_out, mix_post=mix_post, ffn2_pre=ffn2_pre, ffn2_wg=ffn2_wg, ffn2_wu=ffn2_wu,
             ffn2_wd=ffn2_wd, ffn2_post=ffn2_post)
    depth = w_in.shape[0]
    bp, seq, _ = x_prompt.shape
    n_pool = cache_k.shape[1]
    n_pages = page_table.shape[1]
    t_attn = _attn_tile(seq)
    pages = math.gcd(n_pages, SAMPLE_PAGES_PER_STEP)
    ck = cache_k.reshape(depth * n_pool, PAGE_SIZE * H_B, -1)
    cv = cache_v.reshape(depth * n_pool, PAGE_SIZE * H_B, -1)
    xp, xs = x_prompt, x_sample
    outs = [[] for _ in range(8)]
    for l in range(depth):
        lw = _layer_weights(l, W)

        def attend_prompt(q, k, v, k16, v16):
            return _attn_prompt_t(q, k16, v16, rel_table, lw["lam_params"], lw["subln"],
                                t=t_attn, lam_init=lw["lam_init"])

        def attend_sample(q, k, v, k16, v16):
            return _attn_sample(q, k, v, ck, cv, l * n_pool, page_table, rel_table, lw["lam_params"],
                                lw["subln"], pages=pages, lam_init=lw["lam_init"])

        zero_conv = jnp.zeros((bp, CONV_W - 1, state_conv.shape[-1]), x_prompt.dtype)
        zero_ssm = jnp.zeros((bp,) + state_ssm.shape[2:], state_ssm.dtype)
        xp, k1, v1, c1, s1 = _decoder_layer(xp, lw, zero_conv, zero_ssm, attend_prompt)
        xs, k2, v2, c2, s2 = _decoder_layer(xs, lw, state_conv[l], state_ssm[l], attend_sample)
        for o, a in zip(outs, (k1, v1, c1, s1, k2, v2, c2, s2)):
            o.append(a)
    return (xp, xs) + tuple(jnp.stack(o) for o in outs)
```

```python
import functools
import math

import jax
import jax.numpy as jnp
from jax import lax
from jax.experimental import pallas as pl
from jax.experimental.pallas import tpu as pltpu

F32 = jnp.float32
BF16 = jnp.bfloat16

H_A = 4
DK_A = 128
DV_A = 128
CONV_W = 4
CHUNK = 64
H_B = 4
DQK_B = 64
DV_B = 2 * DQK_B
PAGE_SIZE = 128
NUM_BUCKETS = 32
MAX_DISTANCE = 128
EPS = 1e-6
NEG_INF = -1e30

LANES = 128
SUBLANES = 8
VMEM_LIMIT = 56 * 1024 * 1024

HIGHEST = lax.Precision.HIGHEST

GDN_CHUNKS_PER_STEP = 4
GDN_SEQS_PER_STEP = 8
ATTN_ROW_BLOCK = 256
ATTN_COL_BLOCK = 512
ATTN_MXU_TILE = 256
ATTN_SCHEDULER_FLAGS = None
ATTN_LOOKAHEAD = 2
ATTN_BLOCKS_PER_TRIP = 4
SAMPLE_PAGES_PER_STEP = 32
SAMPLE_PAGE_GROUP = 4


def _dot(a, b):
    return jnp.dot(a, b, preferred_element_type=F32)


def _dot_nt(a, b, precision=None):
    return lax.dot_general(a, b, (((1,), (1,)), ((), ())), precision=precision,
                           preferred_element_type=F32)


def _rms(x, g):
    return x * lax.rsqrt(jnp.mean(x * x, axis=-1, keepdims=True) + EPS) * g


def _silu(x):
    return x * jax.nn.sigmoid(x)


def _const_spec(shape):
    nd = len(shape)
    return pl.BlockSpec(shape, lambda *_: (0,) * nd, pipeline_mode=pl.Buffered(1))


def _params(sem, flags=None):
    return pltpu.CompilerParams(dimension_semantics=sem, vmem_limit_bytes=VMEM_LIMIT, flags=flags)


def _ffn_half_step(x, pre_ref, wg_ref, wu_ref, wd_ref, post_ref, f_chunk):
    h = _rms(x, pre_ref[...]).astype(BF16)
    d_ff = wg_ref.shape[1]
    acc = jnp.zeros(x.shape, F32)
    for c in range(d_ff // f_chunk):
        sl = slice(c * f_chunk, (c + 1) * f_chunk)
        g = _dot(h, wg_ref[:, sl])
        u = _dot(h, wu_ref[:, sl])
        acc = acc + _dot((_silu(g) * u).astype(BF16), wd_ref[sl, :])
    return x + 0.5 * _rms(acc, post_ref[...])


def _ffn_body(x_ref, pre_ref, wg_ref, wu_ref, wd_ref, post_ref, o_ref, *, f_chunk):
    o_ref[...] = _ffn_half_step(x_ref[...], pre_ref, wg_ref, wu_ref, wd_ref, post_ref, f_chunk)


def _ffn(x, pre, wg, wu, wd, post, *, tm):
    t, d = x.shape
    d_ff = wg.shape[1]
    f_chunk = 256 if d_ff % 256 == 0 else d_ff
    return pl.pallas_call(
        functools.partial(_ffn_body, f_chunk=f_chunk),
        grid=(t // tm,),
        in_specs=[pl.BlockSpec((tm, d), lambda i: (i, 0)),
                  _const_spec((1, d)), _const_spec((d, d_ff)), _const_spec((d, d_ff)),
                  _const_spec((d_ff, d)), _const_spec((1, d))],
        out_specs=pl.BlockSpec((tm, d), lambda i: (i, 0)),
        out_shape=jax.ShapeDtypeStruct((t, d), F32),
        compiler_params=_params(("parallel",)),
        name="ffn",
    )(x, pre, wg, wu, wd, post)


_PROJ_OUT = (("qkv", H_A * (2 * DK_A + DV_A), F32), ("z", H_A * DV_A, F32),
             ("q", H_B * 2 * DQK_B, BF16), ("k", H_B * 2 * DQK_B, F32), ("v", H_B * DV_B, F32),
             ("ga", None, F32), ("gb", None, F32), ("ba", LANES, F32))


def _proj_body(x_ref, g_ref, w_ref, *o_refs, groups, names):
    h = _rms(x_ref[...], g_ref[...]).astype(BF16)
    refs = dict(zip(names, o_refs))
    off = 0
    for name, w, _ in groups:
        y = _dot(h, w_ref[:, off:off + w])
        if name in ("k", "v"):
            hd = w // H_B
            for hh in range(H_B):
                refs[name][:, hh, :] = y[:, hh * hd:(hh + 1) * hd]
            if name == "k" and "k16" in refs:
                refs["k16"][...] = y.astype(BF16)
            if name == "v" and "vt" in refs:
                for hh in range(H_B):
                    refs["vt"][hh, 0] = y[:, hh * hd:(hh + 1) * hd].T.astype(BF16)
        else:
            refs[name][...] = y.astype(refs[name].dtype)
        off += w


def _proj(x, g, w_r, groups, *, tm, attn_operands):
    t, d = x.shape
    row = lambda w: pl.BlockSpec((tm, w), lambda i: (i, 0))
    names, specs, shapes = [], [], []
    for n, w, dt in groups:
        names.append(n)
        if n in ("k", "v"):
            specs.append(pl.BlockSpec((tm, H_B, w // H_B), lambda i: (i, 0, 0)))
            shapes.append(jax.ShapeDtypeStruct((t, H_B, w // H_B), dt))
        else:
            specs.append(row(w))
            shapes.append(jax.ShapeDtypeStruct((t, w), dt))
    if attn_operands:
        wk, wv = groups[3][1], groups[4][1]
        names += ["k16", "vt"]
        specs += [row(wk), pl.BlockSpec((H_B, 1, wv // H_B, tm), lambda i: (0, i, 0, 0))]
        shapes += [jax.ShapeDtypeStruct((t, wk), BF16), jax.ShapeDtypeStruct((H_B, t // tm, wv // H_B, tm), BF16)]
    res = pl.pallas_call(
        functools.partial(_proj_body, groups=groups, names=names),
        grid=(t // tm,),
        in_specs=[row(d), _const_spec((1, d)), _const_spec(w_r.shape)],
        out_specs=specs,
        out_shape=shapes,
        compiler_params=_params(("parallel",)),
        name="proj",
    )(x, g, w_r)
    return dict(zip(names, res))


def _gdn_body_v1(qkv_ref, cst_ref, z_ref, ba_ref, bat_ref, s0_ref, cw_ref, prow_ref, pcol_ref, gn_ref,
              ya_ref, s_ref, ext_ref, *, chunk, l_valid):
    c = chunk
    j = pl.program_id(1)

    @pl.when(j == 0)
    def _():
        ext_ref[0:SUBLANES, :] = cst_ref[0]
        s_ref[0] = s0_ref[0]

    u_raw = qkv_ref[0]
    ext_ref[SUBLANES:SUBLANES + c, :] = u_raw
    cw = cw_ref[...]
    conv = u_raw * cw[CONV_W - 1:CONV_W, :]
    for k in range(1, CONV_W):
        conv = conv + ext_ref[SUBLANES - k:SUBLANES - k + c, :] * cw[CONV_W - 1 - k:CONV_W - k, :]
    ext_ref[0:SUBLANES, :] = ext_ref[c:c + SUBLANES, :]
    act = _silu(conv)

    ba = ba_ref[0]
    beta_c = jax.nn.sigmoid(ba)
    g_c = -jnp.exp(prow_ref[0:1, :]) * jax.nn.softplus(ba + prow_ref[1:2, :])
    bat = bat_ref[0, 0]
    g_r = -jnp.exp(pcol_ref[0][:, :c]) * jax.nn.softplus(bat + pcol_ref[1][:, :c])
    if l_valid < c:
        g_c = jnp.where(lax.broadcasted_iota(jnp.int32, g_c.shape, 0) < l_valid, g_c, 0.0)
        beta_c = jnp.where(lax.broadcasted_iota(jnp.int32, beta_c.shape, 0) < l_valid, beta_c, 0.0)
        g_r = jnp.where(lax.broadcasted_iota(jnp.int32, g_r.shape, 1) < l_valid, g_r, 0.0)

    ii = lax.broadcasted_iota(jnp.int32, (c, c), 0)
    jj = lax.broadcasted_iota(jnp.int32, (c, c), 1)
    causal = ii >= jj
    strict = ii > jj
    ltri = causal.astype(F32)
    eye = (ii == jj).astype(F32)
    gcum_c = jnp.dot(ltri, g_c, precision=HIGHEST, preferred_element_type=F32)
    gcum_r = _dot_nt(g_r, ltri, precision=HIGHEST)

    hk = H_A * DK_A
    for h in range(H_A):
        q_raw = act[:, h * DK_A:(h + 1) * DK_A]
        k_raw = act[:, hk + h * DK_A:hk + (h + 1) * DK_A]
        v = act[:, 2 * hk + h * DV_A:2 * hk + (h + 1) * DV_A]
        q = q_raw * lax.rsqrt(jnp.sum(q_raw * q_raw, axis=-1, keepdims=True) + EPS) * (DK_A ** -0.5)
        k = k_raw * lax.rsqrt(jnp.sum(k_raw * k_raw, axis=-1, keepdims=True) + EPS)
        beta = beta_c[:, h:h + 1]
        gc = gcum_c[:, H_A + h:H_A + h + 1]
        gr = gcum_r[H_A + h:H_A + h + 1, :]
        decay = jnp.where(causal, jnp.exp(jnp.where(causal, gc - gr, 0.0)), 0.0)
        kb = k * beta
        k16 = k.astype(BF16)
        a = jnp.where(strict, _dot_nt(kb.astype(BF16), k16) * decay, 0.0)
        p = -a
        t_inv = eye + p
        n = 2
        while n < c:
            p = jnp.dot(p, p, precision=HIGHEST, preferred_element_type=F32)
            t_inv = t_inv + jnp.dot(t_inv, p, precision=HIGHEST, preferred_element_type=F32)
            n *= 2
        t16 = t_inv.astype(BF16)
        eg = jnp.exp(gc)
        u = _dot(t16, (v * beta).astype(BF16))
        w = _dot(t16, (kb * eg).astype(BF16))
        qk = _dot_nt(q.astype(BF16), k16) * decay
        s_old = s_ref[0, h]
        s16 = s_old.astype(BF16)
        v_new = u - _dot(w.astype(BF16), s16)
        v_new16 = v_new.astype(BF16)
        o = _dot((q * eg).astype(BF16), s16) + _dot(qk.astype(BF16), v_new16)
        g_last = gc[c - 1:c, :]
        k_dec = k * jnp.exp(g_last - gc)
        s_ref[0, h] = s_old * jnp.exp(g_last) + lax.dot_general(
            k_dec.astype(BF16), v_new16, (((0,), (0,)), ((), ())), preferred_element_type=F32)
        zh = z_ref[0, :, h * DV_A:(h + 1) * DV_A]
        ya_ref[0, :, h * DV_A:(h + 1) * DV_A] = (_rms(o, gn_ref[...]) * _silu(zh)).astype(ya_ref.dtype)


def _gdn_v1(qkv, conv_state, z, ba, s0, conv_w, a_log, dt_bias, g_norm, *, chunk, l_valid):
    b, l, cc = qkv.shape
    n = l // chunk
    assert l_valid == chunk or n == 1
    cst = jnp.pad(conv_state, ((0, 0), (SUBLANES - (CONV_W - 1), 0), (0, 0)))
    bat = ba[..., :2 * H_A].reshape(b, n, chunk, 2 * H_A).transpose(0, 1, 3, 2)
    zeros_h = jnp.zeros((H_A,), F32)
    lane_pad = jnp.zeros((LANES - 2 * H_A,), F32)
    prow = jnp.stack([jnp.concatenate([zeros_h, a_log, lane_pad]),
                      jnp.concatenate([zeros_h, dt_bias, lane_pad])])
    pcol = jnp.broadcast_to(prow[:, :2 * H_A, None], (2, 2 * H_A, LANES))
    return pl.pallas_call(
        functools.partial(_gdn_body, chunk=chunk, l_valid=l_valid),
        grid=(b, n),
        in_specs=[pl.BlockSpec((1, chunk, cc), lambda i, j: (i, j, 0)),
                  pl.BlockSpec((1, SUBLANES, cc), lambda i, j: (i, 0, 0)),
                  pl.BlockSpec((1, chunk, H_A * DV_A), lambda i, j: (i, j, 0)),
                  pl.BlockSpec((1, chunk, LANES), lambda i, j: (i, j, 0)),
                  pl.BlockSpec((1, 1, 2 * H_A, chunk), lambda i, j: (i, j, 0, 0)),
                  pl.BlockSpec((1, H_A, DK_A, DV_A), lambda i, j: (i, 0, 0, 0)),
                  _const_spec((CONV_W, cc)), _const_spec((2, LANES)),
                  _const_spec((2, 2 * H_A, LANES)), _const_spec((1, DV_A))],
        out_specs=[pl.BlockSpec((1, chunk, H_A * DV_A), lambda i, j: (i, j, 0)),
                   pl.BlockSpec((1, H_A, DK_A, DV_A), lambda i, j: (i, 0, 0, 0))],
        out_shape=[jax.ShapeDtypeStruct((b, l, H_A * DV_A), BF16),
                   jax.ShapeDtypeStruct((b, H_A, DK_A, DV_A), F32)],
        scratch_shapes=[pltpu.VMEM((SUBLANES + chunk, cc), F32)],
        compiler_params=_params(("parallel", "arbitrary")),
        name="gdn",
    )(qkv, cst, z, ba, bat, s0, conv_w, prow, pcol, g_norm.reshape(1, DV_A))


def _split2(x):
    hi = x.astype(BF16)
    return hi, (x - hi.astype(F32)).astype(BF16)


def _split3(x):
    hi = x.astype(BF16)
    r = x - hi.astype(F32)
    mid = r.astype(BF16)
    return hi, mid, (r - mid.astype(F32)).astype(BF16)


def _mm3(x, y):
    x_hi, x_lo = _split2(x)
    y_hi, y_lo = _split2(y)
    return _dot(x_hi, y_hi) + _dot(x_lo, y_hi) + _dot(x_hi, y_lo)


def _neumann_level(p, t_inv, c):
    if c % 16 == 0:
        x_hi, x_lo = _split2(jnp.concatenate([p, t_inv], axis=0))
        p_hi, p_lo = x_hi[:c], x_lo[:c]
        y = _dot(jnp.concatenate([x_hi, x_lo], axis=0), p_hi)
        y = y[:2 * c] + y[2 * c:] + _dot(x_hi, p_lo)
        return y[:c], t_inv + y[c:]
    p_hi, p_lo = _split2(p)
    t_hi, t_lo = _split2(t_inv)
    pp = _dot(p_hi, p_hi) + _dot(p_lo, p_hi) + _dot(p_hi, p_lo)
    tp = _dot(t_hi, p_hi) + _dot(t_lo, p_hi) + _dot(t_hi, p_lo)
    return pp, t_inv + tp


def _gdn_body(qkv_ref, cst_ref, z_ref, ba_ref, bat_ref, s0_ref, cw_ref, prow_ref, pcol_ref, gn_ref,
              ya_ref, s_ref, ext_ref, *, chunk, n_chunk, n_seq, l_valid):
    c = chunk
    tb = n_chunk * c
    j = pl.program_id(1)

    @pl.when(j == 0)
    def _():
        ext_ref[:, 0:SUBLANES, :] = cst_ref[...]
        s_ref[...] = s0_ref[...]

    ii = lax.broadcasted_iota(jnp.int32, (c, c), 0)
    jj = lax.broadcasted_iota(jnp.int32, (c, c), 1)
    causal = ii >= jj
    strict = ii > jj
    ltri16 = causal.astype(BF16)
    eye = (ii == jj).astype(F32)
    cw = cw_ref[...]
    hk = H_A * DK_A

    rows = lambda n: slice(n * c, (n + 1) * c)
    act, beta_c, g_c = [], [], []
    for s in range(n_seq):
        u_raw = qkv_ref[s]
        ext_ref[s, SUBLANES:SUBLANES + tb, :] = u_raw
        conv = u_raw * cw[CONV_W - 1:CONV_W, :]
        for k in range(1, CONV_W):
            conv = conv + ext_ref[s, SUBLANES - k:SUBLANES - k + tb, :] * cw[CONV_W - 1 - k:CONV_W - k, :]
        ext_ref[s, 0:SUBLANES, :] = ext_ref[s, tb:tb + SUBLANES, :]
        act.append(_silu(conv))
        ba = ba_ref[s]
        beta_s = jax.nn.sigmoid(ba)
        g_s = -jnp.exp(prow_ref[0:1, :]) * jax.nn.softplus(ba + prow_ref[1:2, :])
        if l_valid < c:
            valid = lax.broadcasted_iota(jnp.int32, g_s.shape, 0) < l_valid
            g_s = jnp.where(valid, g_s, 0.0)
            beta_s = jnp.where(valid, beta_s, 0.0)
        beta_c.append(beta_s)
        g_c.append(g_s)

    blocks = [(s, n) for s in range(n_seq) for n in range(n_chunk)]
    chains = [(s, n, h) for s, n in blocks for h in range(H_A)]

    gcum_c, gcum_r = {}, {}
    for s, n in blocks:
        g_r = -jnp.exp(pcol_ref[0][:, :c]) * jax.nn.softplus(bat_ref[s, n] + pcol_ref[1][:, :c])
        if l_valid < c:
            g_r = jnp.where(lax.broadcasted_iota(jnp.int32, g_r.shape, 1) < l_valid, g_r, 0.0)
        gcum_c[s, n] = functools.reduce(jnp.add, [_dot(ltri16, x) for x in _split3(g_c[s][rows(n)])])
        gcum_r[s, n] = functools.reduce(jnp.add, [_dot_nt(x, ltri16) for x in _split3(g_r)])

    st = {}
    for ch in chains:
        s, n, h = ch
        rs = rows(n)
        q_raw = act[s][rs, h * DK_A:(h + 1) * DK_A]
        k_raw = act[s][rs, hk + h * DK_A:hk + (h + 1) * DK_A]
        v = act[s][rs, 2 * hk + h * DV_A:2 * hk + (h + 1) * DV_A]
        q = q_raw * lax.rsqrt(jnp.sum(q_raw * q_raw, axis=-1, keepdims=True) + EPS) * (DK_A ** -0.5)
        k = k_raw * lax.rsqrt(jnp.sum(k_raw * k_raw, axis=-1, keepdims=True) + EPS)
        beta = beta_c[s][rs, h:h + 1]
        gc = gcum_c[s, n][:, H_A + h:H_A + h + 1]
        gr = gcum_r[s, n][H_A + h:H_A + h + 1, :]
        decay = jnp.where(causal, jnp.exp(jnp.where(causal, gc - gr, 0.0)), 0.0)
        kb = k * beta
        kq = _dot_nt(jnp.concatenate([kb, q], axis=0).astype(BF16), k.astype(BF16))
        eg = jnp.exp(gc)
        g_last = gc[c - 1:c, :]
        p = -jnp.where(strict, kq[:c] * decay, 0.0)
        st[ch] = dict(p=p, t=eye + p, qk=(kq[c:] * decay).astype(BF16),
                      rhs=jnp.concatenate([v * beta, kb * eg], axis=1).astype(BF16), qe=q * eg,
                      k_dec=(k * jnp.exp(g_last - gc)).astype(BF16), dec=jnp.exp(g_last))

    n_factor = c.bit_length() - 1
    if n_factor >= 2:
        for e in st.values():
            e["p"] = _mm3(e["p"], e["p"])
        for _ in range(n_factor - 2):
            for e in st.values():
                e["p"], e["t"] = _neumann_level(e["p"], e["t"], c)
        for e in st.values():
            e["t"] = e["t"] + _mm3(e["t"], e["p"])
    for e in st.values():
        uw = _dot(e["t"].astype(BF16), e["rhs"])
        e["u"] = uw[:, :DV_A]
        e["wq"] = jnp.concatenate([uw[:, DV_A:], e["qe"]], axis=0).astype(BF16)

    state = {(s, h): s_ref[s, h] for s in range(n_seq) for h in range(H_A)}
    for n in range(n_chunk):
        grp = [(s, h) for s in range(n_seq) for h in range(H_A)]
        ws = {g: _dot(st[g[0], n, g[1]]["wq"], state[g].astype(BF16)) for g in grp}
        v_new = {g: (st[g[0], n, g[1]]["u"] - ws[g][:c]).astype(BF16) for g in grp}
        for g in grp:
            s, h = g
            e = st[s, n, h]
            o = ws[g][c:] + _dot(e["qk"], v_new[g])
            state[g] = state[g] * e["dec"] + lax.dot_general(
                e["k_dec"], v_new[g], (((0,), (0,)), ((), ())), preferred_element_type=F32)
            zh = z_ref[s, rows(n), h * DV_A:(h + 1) * DV_A]
            ya_ref[s, rows(n), h * DV_A:(h + 1) * DV_A] = (_rms(o, gn_ref[...]) * _silu(zh)).astype(ya_ref.dtype)
    for (s, h), val in state.items():
        s_ref[s, h] = val


def _gdn(qkv, conv_state, z, ba, s0, conv_w, a_log, dt_bias, g_norm, *, chunk, n_chunk, n_seq, l_valid):
    b, l, cc = qkv.shape
    n = l // chunk
    tb = n_chunk * chunk
    assert l_valid == chunk or n == 1
    assert l % tb == 0 and b % n_seq == 0
    assert chunk & (chunk - 1) == 0
    cst = jnp.pad(conv_state, ((0, 0), (SUBLANES - (CONV_W - 1), 0), (0, 0)))
    bat = ba[..., :2 * H_A].reshape(b, n, chunk, 2 * H_A).transpose(0, 1, 3, 2)
    zeros_h = jnp.zeros((H_A,), F32)
    lane_pad = jnp.zeros((LANES - 2 * H_A,), F32)
    prow = jnp.stack([jnp.concatenate([zeros_h, a_log, lane_pad]),
                      jnp.concatenate([zeros_h, dt_bias, lane_pad])])
    pcol = jnp.broadcast_to(prow[:, :2 * H_A, None], (2, 2 * H_A, LANES))
    return pl.pallas_call(
        functools.partial(_gdn_body, chunk=chunk, n_chunk=n_chunk, n_seq=n_seq, l_valid=l_valid),
        grid=(b // n_seq, l // tb),
        in_specs=[pl.BlockSpec((n_seq, tb, cc), lambda i, j: (i, j, 0)),
                  pl.BlockSpec((n_seq, SUBLANES, cc), lambda i, j: (i, 0, 0)),
                  pl.BlockSpec((n_seq, tb, H_A * DV_A), lambda i, j: (i, j, 0)),
                  pl.BlockSpec((n_seq, tb, LANES), lambda i, j: (i, j, 0)),
                  pl.BlockSpec((n_seq, n_chunk, 2 * H_A, chunk), lambda i, j: (i, j, 0, 0)),
                  pl.BlockSpec((n_seq, H_A, DK_A, DV_A), lambda i, j: (i, 0, 0, 0)),
                  _const_spec((CONV_W, cc)), _const_spec((2, LANES)),
                  _const_spec((2, 2 * H_A, LANES)), _const_spec((1, DV_A))],
        out_specs=[pl.BlockSpec((n_seq, tb, H_A * DV_A), lambda i, j: (i, j, 0)),
                   pl.BlockSpec((n_seq, H_A, DK_A, DV_A), lambda i, j: (i, 0, 0, 0))],
        out_shape=[jax.ShapeDtypeStruct((b, l, H_A * DV_A), BF16),
                   jax.ShapeDtypeStruct((b, H_A, DK_A, DV_A), F32)],
        scratch_shapes=[pltpu.VMEM((n_seq, SUBLANES + tb, cc), F32)],
        compiler_params=_params(("parallel", "arbitrary")),
        name="gdn",
    )(qkv, cst, z, ba, bat, s0, conv_w, prow, pcol, g_norm.reshape(1, DV_A))


def _rel_bucket(n):
    n = jnp.maximum(n, 0)
    max_exact = NUM_BUCKETS // 2
    large = max_exact + (jnp.log(jnp.maximum(n, 1).astype(F32) / max_exact)
                         / math.log(MAX_DISTANCE / max_exact) * (NUM_BUCKETS - max_exact)).astype(jnp.int32)
    large = jnp.minimum(large, NUM_BUCKETS - 1)
    return jnp.where(n < max_exact, n, large)


def _bias_rows(rel, tab_rows):
    onehot = _rel_bucket(rel)[..., None] == jnp.arange(NUM_BUCKETS)
    b = jnp.sum(jnp.where(onehot, tab_rows[..., None, :], 0.0), axis=-1)
    return jnp.where(rel >= 0, b, NEG_INF)


def _lam_of(lp_ref, lam_init):
    lp = lp_ref[...]
    s1 = jnp.sum(lp[0:1] * lp[1:2], axis=-1, keepdims=True)
    s2 = jnp.sum(lp[2:3] * lp[3:4], axis=-1, keepdims=True)
    return jnp.exp(s1) - jnp.exp(s2) + lam_init


def _attn_prompt_body(q_ref, k_ref, v_ref, tiles_ref, lp_ref, sub_ref, o_ref,
                      qq_ref, m_ref, l_ref, acc_ref, *, t, row_block, lam_init):
    i = pl.program_id(2)
    q = q_ref[0]
    lane = lax.broadcasted_iota(jnp.int32, q.shape, 1)
    scale = jnp.asarray(DQK_B ** -0.5, BF16)
    zero = jnp.zeros_like(q)
    qq_ref[0:t, :] = jnp.where(lane < DQK_B, q * scale, zero)
    qq_ref[t:2 * t, :] = jnp.where(lane >= DQK_B, q * scale, zero)
    m_ref[...] = jnp.full(m_ref.shape, NEG_INF, F32)
    l_ref[...] = jnp.zeros(l_ref.shape, F32)
    acc_ref[...] = jnp.zeros(acc_ref.shape, F32)
    n_col = t // LANES

    def step(j, tile):
        start = pl.multiple_of(j * t, t)
        k = k_ref[0, pl.ds(start, t), :]
        v = v_ref[0, pl.ds(start, t), :]
        starts = list(range(0, 2 * t, row_block))
        qk = lambda r: _dot_nt(qq_ref[r:r + row_block, :], k)
        scores = {r: qk(r) for r in starts[:ATTN_LOOKAHEAD]}
        for idx, r0 in enumerate(starts):
            rows = slice(r0, r0 + row_block)
            if idx + ATTN_LOOKAHEAD < len(starts):
                scores[starts[idx + ATTN_LOOKAHEAD]] = qk(starts[idx + ATTN_LOOKAHEAD])
            s = scores.pop(r0)
            if tile is not None:
                s = s + tiles_ref[0, tile, r0 % t:r0 % t + row_block, :]
            cols = [s[:, c * LANES:(c + 1) * LANES] for c in range(n_col)]
            m_prev = m_ref[rows, :]
            m_new = jnp.maximum(m_prev, jnp.max(functools.reduce(jnp.maximum, cols), axis=-1, keepdims=True))
            alpha = jnp.exp(m_prev - m_new)
            ps = [jnp.exp(c - m_new) for c in cols]
            l_ref[rows, :] = alpha * l_ref[rows, :] + functools.reduce(jnp.add, ps)
            p16 = jnp.concatenate([p.astype(BF16) for p in ps], axis=-1)
            acc_ref[rows, :] = alpha * acc_ref[rows, :] + _dot(p16, v)
            m_ref[rows, :] = m_new

    def far(j, carry):
        step(j, None)
        return carry

    lax.fori_loop(0, jnp.maximum(i - 1, 0), far, 0)

    @pl.when(i >= 1)
    def _():
        step(i - 1, 1)

    step(i, 0)

    lam = _lam_of(lp_ref, lam_init)
    acc = acc_ref[...] / jnp.sum(l_ref[...], axis=-1, keepdims=True)
    o = acc[0:t] - lam * acc[t:2 * t]
    o_ref[0] = (_rms(o, sub_ref[...]) * (1.0 - lam_init)).astype(o_ref.dtype)


def _attn_prompt(q, k, v, rel_table, lam_params, subln, *, t, lam_init):
    b, l, _ = q.shape
    nq = l // t
    assert t >= MAX_DISTANCE
    table = rel_table.astype(F32)
    d = jnp.arange(t)[:, None] - jnp.arange(t)[None, :]
    rel = jnp.broadcast_to(jnp.stack([d, d + t]), (H_B, 2, t, t))
    tab_rows = jnp.broadcast_to(table.T[:, None, None, :], (H_B, 2, t, NUM_BUCKETS))
    tiles = _bias_rows(rel, tab_rows) - table[NUM_BUCKETS - 1][:, None, None, None]
    hw = 2 * DQK_B
    return pl.pallas_call(
        functools.partial(_attn_prompt_body, t=t, row_block=min(ATTN_ROW_BLOCK, t), lam_init=lam_init),
        grid=(b, H_B, nq),
        in_specs=[pl.BlockSpec((1, t, hw), lambda bi, h, i: (bi, i, h)),
                  pl.BlockSpec((1, l, hw), lambda bi, h, i: (bi, 0, h)),
                  pl.BlockSpec((1, l, DV_B), lambda bi, h, i: (bi, 0, h)),
                  pl.BlockSpec((1, 2, t, t), lambda bi, h, i: (h, 0, 0, 0)),
                  _const_spec(lam_params.shape), _const_spec((1, DV_B))],
        out_specs=pl.BlockSpec((1, t, DV_B), lambda bi, h, i: (bi, i, h)),
        out_shape=jax.ShapeDtypeStruct((b, l, H_B * DV_B), BF16),
        scratch_shapes=[pltpu.VMEM((2 * t, hw), BF16), pltpu.VMEM((2 * t, LANES), F32),
                        pltpu.VMEM((2 * t, LANES), F32), pltpu.VMEM((2 * t, DV_B), F32)],
        compiler_params=_params(("parallel", "parallel", "arbitrary")),
        name="attn_prompt",
    )(q, k, v, tiles, lam_params, subln.reshape(1, DV_B))


def _attn_prompt_t_body(q_ref, k_ref, vt_ref, tiles_ref, lp_ref, sub_ref, o_ref,
                        qt_ref, m_ref, l_ref, acc_ref, s_ref, *, t, col_block, lam_init):
    i = pl.program_id(2)
    n2 = 2 * t
    qt = q_ref[0].astype(F32).T * (DQK_B ** -0.5)
    dim = lax.broadcasted_iota(jnp.int32, qt.shape, 0)
    qt_ref[:, 0:t] = jnp.where(dim < DQK_B, qt, 0.0).astype(BF16)
    qt_ref[:, t:n2] = jnp.where(dim >= DQK_B, qt, 0.0).astype(BF16)
    m_ref[...] = jnp.full(m_ref.shape, NEG_INF, F32)
    l_ref[...] = jnp.zeros(l_ref.shape, F32)
    acc_ref[...] = jnp.zeros(acc_ref.shape, F32)
    groups = t // SUBLANES

    def run(blocks):
        kv = []
        for j, _ in blocks:
            start = pl.multiple_of(j * t, t)
            kv.append((k_ref[0, pl.ds(start, t), :], vt_ref[0, j]))
        items = [(b, c0) for b in range(len(blocks)) for c0 in range(0, n2, col_block)]

        def qk(n):
            b, c0 = items[n]
            s = _dot(kv[b][0], qt_ref[:, c0:c0 + col_block])
            if blocks[b][1] is not None:
                s = s + tiles_ref[0, blocks[b][1], :, c0 % t:c0 % t + col_block]
            s_ref[n % (ATTN_LOOKAHEAD + 1)] = s
            return jnp.max(s.reshape(groups, SUBLANES, col_block), axis=0)

        part_max = {n: qk(n) for n in range(min(ATTN_LOOKAHEAD, len(items)))}
        mt = min(ATTN_MXU_TILE, t, col_block)
        for idx, it in enumerate(items):
            b, c0 = it
            cols = slice(c0, c0 + col_block)
            if idx + ATTN_LOOKAHEAD < len(items):
                part_max[idx + ATTN_LOOKAHEAD] = qk(idx + ATTN_LOOKAHEAD)
            s = s_ref.at[idx % (ATTN_LOOKAHEAD + 1)]
            m_prev = m_ref[:, cols]
            m_new = jnp.maximum(m_prev, jnp.max(part_max.pop(idx), axis=0, keepdims=True))
            alpha = jnp.exp(m_prev - m_new)
            l_new, pv = [], []
            for n0 in range(0, col_block, mt):
                l_n = alpha[:, n0:n0 + mt] * l_ref[:, c0 + n0:c0 + n0 + mt]
                pv_n = None
                for k0 in range(0, t, mt):
                    p3 = jnp.exp(s[k0:k0 + mt, n0:n0 + mt].reshape(mt // SUBLANES, SUBLANES, mt)
                                 - m_new[None, :, n0:n0 + mt])
                    l_n = l_n + jnp.sum(p3, axis=0)
                    d = _dot(kv[b][1][:, k0:k0 + mt], p3.reshape(mt, mt).astype(BF16))
                    pv_n = d if pv_n is None else pv_n + d
                l_new.append(l_n)
                pv.append(pv_n)
            l_ref[:, cols] = jnp.concatenate(l_new, axis=1)
            acc = acc_ref[:, cols].reshape(DV_B // SUBLANES, SUBLANES, col_block) * alpha[None]
            acc_ref[:, cols] = acc.reshape(DV_B, col_block) + jnp.concatenate(pv, axis=1)
            m_ref[:, cols] = m_new

    n_far = jnp.maximum(i - 1, 0)
    per_trip = ATTN_BLOCKS_PER_TRIP

    def far_group(jj, carry):
        run([(per_trip * jj + u, None) for u in range(per_trip)])
        return carry

    def far_single(j, carry):
        run([(j, None)])
        return carry

    lax.fori_loop(0, n_far // per_trip, far_group, 0)
    lax.fori_loop(n_far - n_far % per_trip, n_far, far_single, 0)

    @pl.when(i >= 1)
    def _():
        run([(i - 1, 1), (i, 0)])

    @pl.when(i == 0)
    def _():
        run([(i, 0)])

    lam = _lam_of(lp_ref, lam_init)
    acc = acc_ref[...] / jnp.sum(l_ref[...], axis=0, keepdims=True)
    o = (acc[:, 0:t] - lam * acc[:, t:n2]).T
    o_ref[0] = (_rms(o, sub_ref[...]) * (1.0 - lam_init)).astype(o_ref.dtype)


def _attn_prompt_t(q, k, vt, rel_table, lam_params, subln, *, t, lam_init):
    b, l, _ = q.shape
    nq = l // t
    assert t >= MAX_DISTANCE
    assert vt.shape == (H_B, b * nq, DV_B, t)
    table = rel_table.astype(F32)
    dist = jnp.arange(-t + 1, 2 * t)[None, None, :]
    f = _bias_rows(jnp.broadcast_to(dist, (H_B, 1, 3 * t - 1)), table.T[:, None, :])[:, 0]
    f = f - table[NUM_BUCKETS - 1][:, None]

    def toeplitz(w):
        wp = jnp.pad(w, ((0, 0), (0, 1)))
        g = jnp.tile(wp, (1, t))[:, :t * (2 * t - 1)].reshape(H_B, t, 2 * t - 1)
        return g[:, :, t - 1:]

    tiles = jnp.stack([toeplitz(f[:, :2 * t - 1]), toeplitz(f[:, t:])], axis=1)
    hw = 2 * DQK_B
    return pl.pallas_call(
        functools.partial(_attn_prompt_t_body, t=t, col_block=min(ATTN_COL_BLOCK, t), lam_init=lam_init),
        grid=(b, H_B, nq),
        in_specs=[pl.BlockSpec((1, t, hw), lambda bi, h, i: (bi, i, h)),
                  pl.BlockSpec((1, l, hw), lambda bi, h, i: (bi, 0, h)),
                  pl.BlockSpec((1, nq, DV_B, t), lambda bi, h, i: (h, bi, 0, 0)),
                  pl.BlockSpec((1, 2, t, t), lambda bi, h, i: (h, 0, 0, 0)),
                  _const_spec(lam_params.shape), _const_spec((1, DV_B))],
        out_specs=pl.BlockSpec((1, t, DV_B), lambda bi, h, i: (bi, i, h)),
        out_shape=jax.ShapeDtypeStruct((b, l, H_B * DV_B), BF16),
        scratch_shapes=[pltpu.VMEM((hw, 2 * t), BF16), pltpu.VMEM((SUBLANES, 2 * t), F32),
                        pltpu.VMEM((SUBLANES, 2 * t), F32), pltpu.VMEM((DV_B, 2 * t), F32),
                        pltpu.VMEM((ATTN_LOOKAHEAD + 1, t, min(ATTN_COL_BLOCK, t)), F32)],
        compiler_params=_params(("parallel", "parallel", "arbitrary"), flags=ATTN_SCHEDULER_FLAGS),
        name="attn_prompt",
    )(q, k, vt, tiles, lam_params, subln.reshape(1, DV_B))


def _attn_sample_body(pt_ref, q_ref, kn_ref, vn_ref, bpast_ref, bself_ref, lp_ref, sub_ref, *rest,
                      pages, lam_init):
    k_refs = rest[:pages]
    v_refs = rest[pages:2 * pages]
    o_ref, qf_ref, m_ref, l_ref, acc_ref = rest[2 * pages:]
    j = pl.program_id(1)
    rows = qf_ref.shape[0]
    half = rows // 2
    page_cols = PAGE_SIZE * H_B

    @pl.when(j == 0)
    def _():
        q = q_ref[0].astype(F32)
        r = lax.broadcasted_iota(jnp.int32, q.shape, 0)
        lane = lax.broadcasted_iota(jnp.int32, q.shape, 1)
        qf = jnp.where(lane // DQK_B == r // half, q * (DQK_B ** -0.5), 0.0)
        qf_ref[...] = qf
        kn = kn_ref[0]
        vn = vn_ref[0]
        n_self = kn.shape[0]
        s_self = [jnp.sum(qf * kn[c:c + 1], axis=-1, keepdims=True) + bself_ref[:, c:c + 1]
                  for c in range(n_self)]
        m0 = functools.reduce(jnp.maximum, s_self)
        p_self = [jnp.exp(s - m0) for s in s_self]
        m_ref[...] = m0
        l_ref[...] = functools.reduce(jnp.add, p_self)
        acc_ref[...] = functools.reduce(jnp.add, [p * vn[c:c + 1] for c, p in enumerate(p_self)])

    q16 = qf_ref[...].astype(BF16)
    group = math.gcd(pages, SAMPLE_PAGE_GROUP)
    groups = [range(g, g + group) for g in range(0, pages, group)]
    qk = lambda grp: [_dot_nt(q16, k_refs[c][0].astype(BF16)) for c in grp]
    scores = qk(groups[0])
    m_run, l_run, acc = m_ref[...], l_ref[...], acc_ref[...]
    for gi, grp in enumerate(groups):
        cur = scores
        if gi + 1 < len(groups):
            scores = qk(groups[gi + 1])
        s = (jnp.concatenate(cur, axis=-1)
             + bpast_ref[j, :, grp[0] * page_cols:(grp[-1] + 1) * page_cols])
        m_new = jnp.maximum(m_run, jnp.max(s, axis=-1, keepdims=True))
        alpha = jnp.exp(m_run - m_new)
        p = jnp.exp(s - m_new)
        l_run = alpha * l_run + jnp.sum(p, axis=-1, keepdims=True)
        p16 = p.astype(BF16)
        pv = functools.reduce(jnp.add, [_dot(p16[:, i * page_cols:(i + 1) * page_cols], v_refs[c][0].astype(BF16))
                                        for i, c in enumerate(grp)])
        acc = alpha * acc + pv
        m_run = m_new
    m_ref[...], l_ref[...], acc_ref[...] = m_run, l_run, acc

    @pl.when(j == pl.num_programs(1) - 1)
    def _():
        lam = _lam_of(lp_ref, lam_init)
        an = acc_ref[...] / l_ref[...]
        o = an[0:half] - lam * an[half:rows]
        o_ref[0] = _rms(o, sub_ref[...]) * (1.0 - lam_init)


def _attn_sample(q, k_new, v_new, cache_k, cache_v, page_base, page_table, rel_table, lam_params, subln,
                 *, pages, lam_init):
    bd, l_new, width = q.shape
    n_pages = page_table.shape[1]
    past = n_pages * PAGE_SIZE
    half = l_new * H_B
    rows = 2 * half
    page_cols = PAGE_SIZE * H_B
    table = rel_table.astype(F32)
    t_of = (jnp.arange(rows) % half) // H_B
    h_of = jnp.arange(rows) % H_B
    tab_rows = table.T[h_of]
    far = table[NUM_BUCKETS - 1][h_of][:, None]
    own = h_of[:, None, None] == jnp.arange(H_B)[None, None, :]
    rel_past = past + t_of[:, None] - jnp.arange(past)[None, :]
    bpast = jnp.where(own, (_bias_rows(rel_past, tab_rows) - far)[:, :, None], NEG_INF)
    n_steps = n_pages // pages
    bpast = bpast.reshape(rows, n_steps, pages * page_cols).transpose(1, 0, 2)
    n_self = l_new * H_B
    t_key = jnp.arange(LANES) // H_B
    rel_self = jnp.where((jnp.arange(LANES)[None, :] < n_self) & (h_of[:, None] == jnp.arange(LANES)[None, :] % H_B),
                         t_of[:, None] - t_key[None, :], -1)
    bself = _bias_rows(rel_self, tab_rows) - far

    hd = width // H_B
    q_rows = q.reshape(bd, half, hd)
    q_rows = jnp.concatenate([q_rows, q_rows], axis=1)
    kn = k_new.reshape(bd, n_self, hd)
    vn = v_new.reshape(bd, n_self, hd)

    def page_spec(c):
        return pl.BlockSpec((1, page_cols, hd),
                            lambda b, j, pt: (page_base + pt[b * n_pages + j * pages + c], 0, 0))

    seq_spec = lambda r: pl.BlockSpec((1, r, hd), lambda b, j, pt: (b, 0, 0))
    whole = lambda shape: pl.BlockSpec(shape, lambda b, j, pt: (0,) * len(shape), pipeline_mode=pl.Buffered(1))
    grid_spec = pltpu.PrefetchScalarGridSpec(
        num_scalar_prefetch=1,
        grid=(bd, n_pages // pages),
        in_specs=[seq_spec(rows), seq_spec(n_self), seq_spec(n_self),
                  whole(bpast.shape), whole(bself.shape), whole(lam_params.shape), whole((1, DV_B))]
                 + [page_spec(c) for c in range(pages)] + [page_spec(c) for c in range(pages)],
        out_specs=seq_spec(half),
        scratch_shapes=[pltpu.VMEM((rows, hd), F32), pltpu.VMEM((rows, 1), F32),
                        pltpu.VMEM((rows, 1), F32), pltpu.VMEM((rows, DV_B), F32)])
    out = pl.pallas_call(
        functools.partial(_attn_sample_body, pages=pages, lam_init=lam_init),
        grid_spec=grid_spec,
        out_shape=jax.ShapeDtypeStruct((bd, half, DV_B), F32),
        compiler_params=_params(("parallel", "arbitrary")),
        name="attn_sample",
    )(page_table.reshape(-1), q_rows, kn, vn, bpast, bself, lam_params, subln.reshape(1, DV_B),
      *([cache_k] * pages), *([cache_v] * pages))
    return out.reshape(bd, l_new, width)


def _merge_ffn_body(x_ref, ya_ref, yb_ref, ga_ref, gb_ref, wa_ref, wb_ref, wo_ref, post_ref,
                    pre2_ref, wg_ref, wu_ref, wd_ref, post2_ref, o_ref, *, f_chunk):
    merged = (jax.nn.sigmoid(ga_ref[...]) * _dot(ya_ref[...], wa_ref[...])
              + jax.nn.sigmoid(gb_ref[...]) * _dot(yb_ref[...], wb_ref[...]))
    x = x_ref[...] + _rms(_dot(merged.astype(BF16), wo_ref[...]), post_ref[...])
    o_ref[...] = _ffn_half_step(x, pre2_ref, wg_ref, wu_ref, wd_ref, post2_ref, f_chunk)


def _merge_ffn(x, ya, yb, ga, gb, wa, wb, wo, post, pre2, wg, wu, wd, post2, *, tm):
    t, d = x.shape
    d_ff = wg.shape[1]
    f_chunk = 256 if d_ff % 256 == 0 else d_ff
    row = lambda w: pl.BlockSpec((tm, w), lambda i: (i, 0))
    weights = (wa, wb, wo, post, pre2, wg, wu, wd, post2)
    return pl.pallas_call(
        functools.partial(_merge_ffn_body, f_chunk=f_chunk),
        grid=(t // tm,),
        in_specs=[row(d), row(ya.shape[1]), row(yb.shape[1]), row(d), row(d)]
                 + [_const_spec(w.shape) for w in weights],
        out_specs=row(d),
        out_shape=jax.ShapeDtypeStruct((t, d), F32),
        compiler_params=_params(("parallel",)),
        name="merge_ffn",
    )(x, ya, yb, ga, gb, *weights)


def _lambda_init(layer):
    return 0.8 - 0.6 * math.exp(-0.3 * layer)


def _token_tile(t):
    tm = 512
    while t % tm:
        tm //= 2
    return tm


def _attn_tile(seq):
    return min(512, seq)


def _layer_weights(l, W):
    d = W["w_in"].shape[1]
    bf = lambda a: a.astype(BF16)
    row = lambda a: a.reshape(1, -1).astype(F32)
    widths = {"qkv": H_A * (2 * DK_A + DV_A), "z": H_A * DV_A, "beta": H_A, "alpha": H_A,
              "q": H_B * 2 * DQK_B, "k": H_B * 2 * DQK_B, "v": H_B * DV_B, "ga": d, "gb": d}
    order = ("qkv", "z", "beta", "alpha", "q", "k", "v", "ga", "gb")
    offs, off = {}, 0
    for n in order:
        offs[n] = off
        off += widths[n]
    w_in = W["w_in"][l]
    col = lambda n: w_in[:, offs[n]:offs[n] + widths[n]]
    pad = jnp.zeros((d, LANES - 2 * H_A), w_in.dtype)
    w_r = jnp.concatenate([col("qkv"), col("z"), col("q"), col("k"), col("v"), col("ga"), col("gb"),
                           col("beta"), col("alpha"), pad], axis=1)
    groups = tuple((n, (d if w is None else w), dt) for n, w, dt in _PROJ_OUT)
    lam_params = jnp.stack([W["lam_q1"][l], W["lam_k1"][l], W["lam_q2"][l], W["lam_k2"][l]]).astype(F32)
    return dict(
        ffn1=(row(W["ffn1_pre"][l]), bf(W["ffn1_wg"][l]), bf(W["ffn1_wu"][l]), bf(W["ffn1_wd"][l]),
              row(W["ffn1_post"][l])),
        ffn2=(row(W["ffn2_pre"][l]), bf(W["ffn2_wg"][l]), bf(W["ffn2_wu"][l]), bf(W["ffn2_wd"][l]),
              row(W["ffn2_post"][l])),
        mix_pre=row(W["mix_pre"][l]), w_r=bf(w_r), groups=groups,
        conv_w=W["conv_w"][l].astype(F32), a_log=W["a_log"][l].astype(F32), dt_bias=W["dt_bias"][l].astype(F32),
        gdn_norm=W["gdn_norm"][l].astype(F32), lam_params=lam_params, subln=W["subln"][l].astype(F32),
        w_a=bf(W["w_a"][l]), w_b=bf(W["w_b"][l]), w_out=bf(W["w_out"][l]), mix_post=row(W["mix_post"][l]),
        lam_init=_lambda_init(l))


def _decoder_layer(x, lw, conv_state, ssm_state, attend, attn_tile=None):
    b, l, d = x.shape
    t = b * l
    tm = attn_tile or _token_tile(t)
    x2 = _ffn(x.reshape(t, d), *lw["ffn1"], tm=tm)
    c = _proj(x2, lw["mix_pre"], lw["w_r"], lw["groups"], tm=tm, attn_operands=attn_tile is not None)
    c3 = {n: c[n].reshape(b, l, c[n].shape[-1]) for n in ("qkv", "z", "ba")}

    chunk = min(CHUNK, l)
    if chunk % SUBLANES:
        chunk = -(-chunk // SUBLANES) * SUBLANES
    lp = -(-l // chunk) * chunk
    padl = lambda a: jnp.pad(a, ((0, 0), (0, lp - l), (0, 0))) if lp != l else a
    ya, ssm_new = _gdn(padl(c3["qkv"]), conv_state, padl(c3["z"]), padl(c3["ba"]), ssm_state,
                       lw["conv_w"], lw["a_log"], lw["dt_bias"], lw["gdn_norm"],
                       chunk=chunk, n_chunk=math.gcd(lp // chunk, GDN_CHUNKS_PER_STEP),
                       n_seq=math.gcd(b, GDN_SEQS_PER_STEP) if lp == chunk else 1,
                       l_valid=min(l - (lp - chunk), chunk))
    ya = ya[:, :l]
    if l >= CONV_W - 1:
        conv_new = c3["qkv"][:, l - (CONV_W - 1):]
    else:
        conv_new = jnp.concatenate([conv_state.astype(F32), c3["qkv"]], axis=1)[:, -(CONV_W - 1):]

    yb = attend(c, b, l)
    y = _merge_ffn(x2, ya.reshape(t, -1), yb.reshape(t, -1).astype(BF16), c["ga"], c["gb"],
                   lw["w_a"], lw["w_b"], lw["w_out"], lw["mix_post"], *lw["ffn2"], tm=tm)
    return (y.reshape(b, l, d), c["k"].reshape(b, l, H_B, 2 * DQK_B), c["v"].reshape(b, l, H_B, DV_B),
            conv_new, ssm_new)


def kernel(x_prompt, x_sample, cache_k, cache_v, state_conv, state_ssm, page_table, rel_table,
           ffn1_pre, ffn1_wg, ffn1_wu, ffn1_wd, ffn1_post, mix_pre, w_in, conv_w, a_log, dt_bias,
           gdn_norm, lam_q1, lam_k1, lam_q2, lam_k2, subln, w_a, w_b, w_out, mix_post,
           ffn2_pre, ffn2_wg, ffn2_wu, ffn2_wd, ffn2_post):
    W = dict(ffn1_pre=ffn1_pre, ffn1_wg=ffn1_wg, ffn1_wu=ffn1_wu, ffn1_wd=ffn1_wd, ffn1_post=ffn1_post,
             mix_pre=mix_pre, w_in=w_in, conv_w=conv_w, a_log=a_log, dt_bias=dt_bias, gdn_norm=gdn_norm,
             lam_q1=lam_q1, lam_k1=lam_k1, lam_q2=lam_q2, lam_k2=lam_k2, subln=subln, w_a=w_a, w_b=w_b,
             w_out=w_out, mix_post=mix_post, ffn2_pre=ffn2_pre, ffn2_wg=ffn2_wg, ffn2_wu=ffn2_wu,
             ffn2_wd=ffn2_wd, ffn2_post=ffn2_post)
    depth = w_in.shape[0]
    bp, seq, _ = x_prompt.shape
    n_pool = cache_k.shape[1]
    n_pages = page_table.shape[1]
    t_attn = _attn_tile(seq)
    pages = math.gcd(n_pages, SAMPLE_PAGES_PER_STEP)
    ck = cache_k.reshape(depth * n_pool, PAGE_SIZE * H_B, -1)
    cv = cache_v.reshape(depth * n_pool, PAGE_SIZE * H_B, -1)
    xp, xs = x_prompt, x_sample
    outs = [[] for _ in range(8)]
    for l in range(depth):
        lw = _layer_weights(l, W)

        def attend_prompt(c, b, n):
            return _attn_prompt_t(c["q"].reshape(b, n, -1), c["k16"].reshape(b, n, -1), c["vt"], rel_table,
                                  lw["lam_params"], lw["subln"], t=t_attn, lam_init=lw["lam_init"])

        def attend_sample(c, b, n):
            return _attn_sample(c["q"].reshape(b, n, -1), c["k"], c["v"], ck, cv, l * n_pool, page_table,
                                rel_table, lw["lam_params"], lw["subln"], pages=pages, lam_init=lw["lam_init"])

        zero_conv = jnp.zeros((bp, CONV_W - 1, state_conv.shape[-1]), x_prompt.dtype)
        zero_ssm = jnp.zeros((bp,) + state_ssm.shape[2:], state_ssm.dtype)
        xp, k1, v1, c1, s1 = _decoder_layer(xp, lw, zero_conv, zero_ssm, attend_prompt, attn_tile=t_attn)
        xs, k2, v2, c2, s2 = _decoder_layer(xs, lw, state_conv[l], state_ssm[l], attend_sample)
        for o, a in zip(outs, (k1, v1, c1, s1, k2, v2, c2, s2)):
            o.append(a)
    return (xp, xs) + tuple(jnp.stack(o) for o in outs)
```

```python
import functools
import math

import jax
import jax.numpy as jnp
from jax import lax
from jax.experimental import pallas as pl
from jax.experimental.pallas import tpu as pltpu

F32 = jnp.float32
BF16 = jnp.bfloat16

H_A = 4
DK_A = 128
DV_A = 128
CONV_W = 4
CHUNK = 64
H_B = 4
DQK_B = 64
DV_B = 2 * DQK_B
PAGE_SIZE = 128
NUM_BUCKETS = 32
MAX_DISTANCE = 128
EPS = 1e-6
NEG_INF = -1e30

LANES = 128
SUBLANES = 8
VMEM_LIMIT = 56 * 1024 * 1024

HIGHEST = lax.Precision.HIGHEST

LOG2E = math.log2(math.e)
QK_LOGIT_SCALE = DQK_B ** -0.5 * LOG2E

GDN_CHUNKS_PER_STEP = 4
GDN_SEQS_PER_STEP = 8
ATTN_ROW_BLOCK = 256
ATTN_COL_BLOCK = 512
ATTN_MXU_TILE = 256
ATTN_SCHEDULER_FLAGS = None
ATTN_LOOKAHEAD = 2
ATTN_BLOCKS_PER_TRIP = 4
SAMPLE_PAGES_PER_STEP = 32
SAMPLE_PAGE_GROUP = 4


def _dot(a, b):
    return jnp.dot(a, b, preferred_element_type=F32)


def _dot_nt(a, b, precision=None):
    return lax.dot_general(a, b, (((1,), (1,)), ((), ())), precision=precision,
                           preferred_element_type=F32)


def _rms(x, g):
    return x * lax.rsqrt(jnp.mean(x * x, axis=-1, keepdims=True) + EPS) * g


def _silu(x):
    return x * jax.nn.sigmoid(x)


def _const_spec(shape):
    nd = len(shape)
    return pl.BlockSpec(shape, lambda *_: (0,) * nd, pipeline_mode=pl.Buffered(1))


def _params(sem, flags=None):
    return pltpu.CompilerParams(dimension_semantics=sem, vmem_limit_bytes=VMEM_LIMIT, flags=flags)


def _ffn_half_step(x, pre_ref, wg_ref, wu_ref, wd_ref, post_ref, f_chunk):
    h = _rms(x, pre_ref[...]).astype(BF16)
    d_ff = wg_ref.shape[1]
    acc = jnp.zeros(x.shape, F32)
    for c in range(d_ff // f_chunk):
        sl = slice(c * f_chunk, (c + 1) * f_chunk)
        g = _dot(h, wg_ref[:, sl])
        u = _dot(h, wu_ref[:, sl])
        acc = acc + _dot((_silu(g) * u).astype(BF16), wd_ref[sl, :])
    return x + 0.5 * _rms(acc, post_ref[...])


def _ffn_body(x_ref, pre_ref, wg_ref, wu_ref, wd_ref, post_ref, o_ref, *, f_chunk):
    o_ref[...] = _ffn_half_step(x_ref[...], pre_ref, wg_ref, wu_ref, wd_ref, post_ref, f_chunk)


def _ffn(x, pre, wg, wu, wd, post, *, tm):
    t, d = x.shape
    d_ff = wg.shape[1]
    f_chunk = 256 if d_ff % 256 == 0 else d_ff
    return pl.pallas_call(
        functools.partial(_ffn_body, f_chunk=f_chunk),
        grid=(t // tm,),
        in_specs=[pl.BlockSpec((tm, d), lambda i: (i, 0)),
                  _const_spec((1, d)), _const_spec((d, d_ff)), _const_spec((d, d_ff)),
                  _const_spec((d_ff, d)), _const_spec((1, d))],
        out_specs=pl.BlockSpec((tm, d), lambda i: (i, 0)),
        out_shape=jax.ShapeDtypeStruct((t, d), F32),
        compiler_params=_params(("parallel",)),
        name="ffn",
    )(x, pre, wg, wu, wd, post)


_PROJ_OUT = (("qkv", H_A * (2 * DK_A + DV_A), F32), ("z", H_A * DV_A, F32),
             ("q", H_B * 2 * DQK_B, BF16), ("k", H_B * 2 * DQK_B, F32), ("v", H_B * DV_B, F32),
             ("ga", None, F32), ("gb", None, F32), ("ba", LANES, F32))


def _proj_body(x_ref, g_ref, w_ref, *o_refs, groups, names):
    h = _rms(x_ref[...], g_ref[...]).astype(BF16)
    refs = dict(zip(names, o_refs))
    off = 0
    for name, w, _ in groups:
        y = _dot(h, w_ref[:, off:off + w])
        if name == "q":
            y = y * QK_LOGIT_SCALE
        if name in ("k", "v"):
            hd = w // H_B
            for hh in range(H_B):
                refs[name][:, hh, :] = y[:, hh * hd:(hh + 1) * hd]
            if name == "k" and "k16" in refs:
                refs["k16"][...] = y.astype(BF16)
            if name == "v" and "vt" in refs:
                for hh in range(H_B):
                    refs["vt"][hh, 0] = y[:, hh * hd:(hh + 1) * hd].T.astype(BF16)
        else:
            refs[name][...] = y.astype(refs[name].dtype)
        off += w


def _proj(x, g, w_r, groups, *, tm, attn_operands):
    t, d = x.shape
    row = lambda w: pl.BlockSpec((tm, w), lambda i: (i, 0))
    names, specs, shapes = [], [], []
    for n, w, dt in groups:
        names.append(n)
        if n in ("k", "v"):
            specs.append(pl.BlockSpec((tm, H_B, w // H_B), lambda i: (i, 0, 0)))
            shapes.append(jax.ShapeDtypeStruct((t, H_B, w // H_B), dt))
        else:
            specs.append(row(w))
            shapes.append(jax.ShapeDtypeStruct((t, w), dt))
    if attn_operands:
        wk, wv = groups[3][1], groups[4][1]
        names += ["k16", "vt"]
        specs += [row(wk), pl.BlockSpec((H_B, 1, wv // H_B, tm), lambda i: (0, i, 0, 0))]
        shapes += [jax.ShapeDtypeStruct((t, wk), BF16), jax.ShapeDtypeStruct((H_B, t // tm, wv // H_B, tm), BF16)]
    res = pl.pallas_call(
        functools.partial(_proj_body, groups=groups, names=names),
        grid=(t // tm,),
        in_specs=[row(d), _const_spec((1, d)), _const_spec(w_r.shape)],
        out_specs=specs,
        out_shape=shapes,
        compiler_params=_params(("parallel",)),
        name="proj",
    )(x, g, w_r)
    return dict(zip(names, res))


def _gdn_body_v1(qkv_ref, cst_ref, z_ref, ba_ref, bat_ref, s0_ref, cw_ref, prow_ref, pcol_ref, gn_ref,
              ya_ref, s_ref, ext_ref, *, chunk, l_valid):
    c = chunk
    j = pl.program_id(1)

    @pl.when(j == 0)
    def _():
        ext_ref[0:SUBLANES, :] = cst_ref[0]
        s_ref[0] = s0_ref[0]

    u_raw = qkv_ref[0]
    ext_ref[SUBLANES:SUBLANES + c, :] = u_raw
    cw = cw_ref[...]
    conv = u_raw * cw[CONV_W - 1:CONV_W, :]
    for k in range(1, CONV_W):
        conv = conv + ext_ref[SUBLANES - k:SUBLANES - k + c, :] * cw[CONV_W - 1 - k:CONV_W - k, :]
    ext_ref[0:SUBLANES, :] = ext_ref[c:c + SUBLANES, :]
    act = _silu(conv)

    ba = ba_ref[0]
    beta_c = jax.nn.sigmoid(ba)
    g_c = -jnp.exp(prow_ref[0:1, :]) * jax.nn.softplus(ba + prow_ref[1:2, :])
    bat = bat_ref[0, 0]
    g_r = -jnp.exp(pcol_ref[0][:, :c]) * jax.nn.softplus(bat + pcol_ref[1][:, :c])
    if l_valid < c:
        g_c = jnp.where(lax.broadcasted_iota(jnp.int32, g_c.shape, 0) < l_valid, g_c, 0.0)
        beta_c = jnp.where(lax.broadcasted_iota(jnp.int32, beta_c.shape, 0) < l_valid, beta_c, 0.0)
        g_r = jnp.where(lax.broadcasted_iota(jnp.int32, g_r.shape, 1) < l_valid, g_r, 0.0)

    ii = lax.broadcasted_iota(jnp.int32, (c, c), 0)
    jj = lax.broadcasted_iota(jnp.int32, (c, c), 1)
    causal = ii >= jj
    strict = ii > jj
    ltri = causal.astype(F32)
    eye = (ii == jj).astype(F32)
    gcum_c = jnp.dot(ltri, g_c, precision=HIGHEST, preferred_element_type=F32)
    gcum_r = _dot_nt(g_r, ltri, precision=HIGHEST)

    hk = H_A * DK_A
    for h in range(H_A):
        q_raw = act[:, h * DK_A:(h + 1) * DK_A]
        k_raw = act[:, hk + h * DK_A:hk + (h + 1) * DK_A]
        v = act[:, 2 * hk + h * DV_A:2 * hk + (h + 1) * DV_A]
        q = q_raw * lax.rsqrt(jnp.sum(q_raw * q_raw, axis=-1, keepdims=True) + EPS) * (DK_A ** -0.5)
        k = k_raw * lax.rsqrt(jnp.sum(k_raw * k_raw, axis=-1, keepdims=True) + EPS)
        beta = beta_c[:, h:h + 1]
        gc = gcum_c[:, H_A + h:H_A + h + 1]
        gr = gcum_r[H_A + h:H_A + h + 1, :]
        decay = jnp.where(causal, jnp.exp(jnp.where(causal, gc - gr, 0.0)), 0.0)
        kb = k * beta
        k16 = k.astype(BF16)
        a = jnp.where(strict, _dot_nt(kb.astype(BF16), k16) * decay, 0.0)
        p = -a
        t_inv = eye + p
        n = 2
        while n < c:
            p = jnp.dot(p, p, precision=HIGHEST, preferred_element_type=F32)
            t_inv = t_inv + jnp.dot(t_inv, p, precision=HIGHEST, preferred_element_type=F32)
            n *= 2
        t16 = t_inv.astype(BF16)
        eg = jnp.exp(gc)
        u = _dot(t16, (v * beta).astype(BF16))
        w = _dot(t16, (kb * eg).astype(BF16))
        qk = _dot_nt(q.astype(BF16), k16) * decay
        s_old = s_ref[0, h]
        s16 = s_old.astype(BF16)
        v_new = u - _dot(w.astype(BF16), s16)
        v_new16 = v_new.astype(BF16)
        o = _dot((q * eg).astype(BF16), s16) + _dot(qk.astype(BF16), v_new16)
        g_last = gc[c - 1:c, :]
        k_dec = k * jnp.exp(g_last - gc)
        s_ref[0, h] = s_old * jnp.exp(g_last) + lax.dot_general(
            k_dec.astype(BF16), v_new16, (((0,), (0,)), ((), ())), preferred_element_type=F32)
        zh = z_ref[0, :, h * DV_A:(h + 1) * DV_A]
        ya_ref[0, :, h * DV_A:(h + 1) * DV_A] = (_rms(o, gn_ref[...]) * _silu(zh)).astype(ya_ref.dtype)


def _gdn_v1(qkv, conv_state, z, ba, s0, conv_w, a_log, dt_bias, g_norm, *, chunk, l_valid):
    b, l, cc = qkv.shape
    n = l // chunk
    assert l_valid == chunk or n == 1
    cst = jnp.pad(conv_state, ((0, 0), (SUBLANES - (CONV_W - 1), 0), (0, 0)))
    bat = ba[..., :2 * H_A].reshape(b, n, chunk, 2 * H_A).transpose(0, 1, 3, 2)
    zeros_h = jnp.zeros((H_A,), F32)
    lane_pad = jnp.zeros((LANES - 2 * H_A,), F32)
    prow = jnp.stack([jnp.concatenate([zeros_h, a_log, lane_pad]),
                      jnp.concatenate([zeros_h, dt_bias, lane_pad])])
    pcol = jnp.broadcast_to(prow[:, :2 * H_A, None], (2, 2 * H_A, LANES))
    return pl.pallas_call(
        functools.partial(_gdn_body, chunk=chunk, l_valid=l_valid),
        grid=(b, n),
        in_specs=[pl.BlockSpec((1, chunk, cc), lambda i, j: (i, j, 0)),
                  pl.BlockSpec((1, SUBLANES, cc), lambda i, j: (i, 0, 0)),
                  pl.BlockSpec((1, chunk, H_A * DV_A), lambda i, j: (i, j, 0)),
                  pl.BlockSpec((1, chunk, LANES), lambda i, j: (i, j, 0)),
                  pl.BlockSpec((1, 1, 2 * H_A, chunk), lambda i, j: (i, j, 0, 0)),
                  pl.BlockSpec((1, H_A, DK_A, DV_A), lambda i, j: (i, 0, 0, 0)),
                  _const_spec((CONV_W, cc)), _const_spec((2, LANES)),
                  _const_spec((2, 2 * H_A, LANES)), _const_spec((1, DV_A))],
        out_specs=[pl.BlockSpec((1, chunk, H_A * DV_A), lambda i, j: (i, j, 0)),
                   pl.BlockSpec((1, H_A, DK_A, DV_A), lambda i, j: (i, 0, 0, 0))],
        out_shape=[jax.ShapeDtypeStruct((b, l, H_A * DV_A), BF16),
                   jax.ShapeDtypeStruct((b, H_A, DK_A, DV_A), F32)],
        scratch_shapes=[pltpu.VMEM((SUBLANES + chunk, cc), F32)],
        compiler_params=_params(("parallel", "arbitrary")),
        name="gdn",
    )(qkv, cst, z, ba, bat, s0, conv_w, prow, pcol, g_norm.reshape(1, DV_A))


def _split2(x):
    hi = x.astype(BF16)
    return hi, (x - hi.astype(F32)).astype(BF16)


def _split3(x):
    hi = x.astype(BF16)
    r = x - hi.astype(F32)
    mid = r.astype(BF16)
    return hi, mid, (r - mid.astype(F32)).astype(BF16)


def _mm3(x, y):
    x_hi, x_lo = _split2(x)
    y_hi, y_lo = _split2(y)
    return _dot(x_hi, y_hi) + _dot(x_lo, y_hi) + _dot(x_hi, y_lo)


def _neumann_level(p, t_inv, c):
    if c % 16 == 0:
        x_hi, x_lo = _split2(jnp.concatenate([p, t_inv], axis=0))
        p_hi, p_lo = x_hi[:c], x_lo[:c]
        y = _dot(jnp.concatenate([x_hi, x_lo], axis=0), p_hi)
        y = y[:2 * c] + y[2 * c:] + _dot(x_hi, p_lo)
        return y[:c], t_inv + y[c:]
    p_hi, p_lo = _split2(p)
    t_hi, t_lo = _split2(t_inv)
    pp = _dot(p_hi, p_hi) + _dot(p_lo, p_hi) + _dot(p_hi, p_lo)
    tp = _dot(t_hi, p_hi) + _dot(t_lo, p_hi) + _dot(t_hi, p_lo)
    return pp, t_inv + tp


def _gdn_body(qkv_ref, cst_ref, z_ref, ba_ref, bat_ref, s0_ref, cw_ref, prow_ref, pcol_ref, gn_ref,
              ya_ref, s_ref, ext_ref, *, chunk, n_chunk, n_seq, l_valid):
    c = chunk
    tb = n_chunk * c
    j = pl.program_id(1)

    @pl.when(j == 0)
    def _():
        ext_ref[:, 0:SUBLANES, :] = cst_ref[...]
        s_ref[...] = s0_ref[...]

    ii = lax.broadcasted_iota(jnp.int32, (c, c), 0)
    jj = lax.broadcasted_iota(jnp.int32, (c, c), 1)
    causal = ii >= jj
    strict = ii > jj
    ltri16 = causal.astype(BF16)
    eye = (ii == jj).astype(F32)
    cw = cw_ref[...]
    hk = H_A * DK_A

    rows = lambda n: slice(n * c, (n + 1) * c)
    act, beta_c, g_c = [], [], []
    for s in range(n_seq):
        u_raw = qkv_ref[s]
        ext_ref[s, SUBLANES:SUBLANES + tb, :] = u_raw
        conv = u_raw * cw[CONV_W - 1:CONV_W, :]
        for k in range(1, CONV_W):
            conv = conv + ext_ref[s, SUBLANES - k:SUBLANES - k + tb, :] * cw[CONV_W - 1 - k:CONV_W - k, :]
        ext_ref[s, 0:SUBLANES, :] = ext_ref[s, tb:tb + SUBLANES, :]
        act.append(_silu(conv))
        ba = ba_ref[s]
        beta_s = jax.nn.sigmoid(ba)
        g_s = -jnp.exp(prow_ref[0:1, :]) * jax.nn.softplus(ba + prow_ref[1:2, :])
        if l_valid < c:
            valid = lax.broadcasted_iota(jnp.int32, g_s.shape, 0) < l_valid
            g_s = jnp.where(valid, g_s, 0.0)
            beta_s = jnp.where(valid, beta_s, 0.0)
        beta_c.append(beta_s)
        g_c.append(g_s)

    blocks = [(s, n) for s in range(n_seq) for n in range(n_chunk)]
    chains = [(s, n, h) for s, n in blocks for h in range(H_A)]

    gcum_c, gcum_r = {}, {}
    for s, n in blocks:
        g_r = -jnp.exp(pcol_ref[0][:, :c]) * jax.nn.softplus(bat_ref[s, n] + pcol_ref[1][:, :c])
        if l_valid < c:
            g_r = jnp.where(lax.broadcasted_iota(jnp.int32, g_r.shape, 1) < l_valid, g_r, 0.0)
        gcum_c[s, n] = functools.reduce(jnp.add, [_dot(ltri16, x) for x in _split3(g_c[s][rows(n)])])
        gcum_r[s, n] = functools.reduce(jnp.add, [_dot_nt(x, ltri16) for x in _split3(g_r)])

    st = {}
    for ch in chains:
        s, n, h = ch
        rs = rows(n)
        q_raw = act[s][rs, h * DK_A:(h + 1) * DK_A]
        k_raw = act[s][rs, hk + h * DK_A:hk + (h + 1) * DK_A]
        v = act[s][rs, 2 * hk + h * DV_A:2 * hk + (h + 1) * DV_A]
        q = q_raw * lax.rsqrt(jnp.sum(q_raw * q_raw, axis=-1, keepdims=True) + EPS) * (DK_A ** -0.5)
        k = k_raw * lax.rsqrt(jnp.sum(k_raw * k_raw, axis=-1, keepdims=True) + EPS)
        beta = beta_c[s][rs, h:h + 1]
        gc = gcum_c[s, n][:, H_A + h:H_A + h + 1]
        gr = gcum_r[s, n][H_A + h:H_A + h + 1, :]
        decay = jnp.where(causal, jnp.exp(jnp.where(causal, gc - gr, 0.0)), 0.0)
        kb = k * beta
        kq = _dot_nt(jnp.concatenate([kb, q], axis=0).astype(BF16), k.astype(BF16))
        eg = jnp.exp(gc)
        g_last = gc[c - 1:c, :]
        p = -jnp.where(strict, kq[:c] * decay, 0.0)
        st[ch] = dict(p=p, t=eye + p, qk=(kq[c:] * decay).astype(BF16),
                      rhs=jnp.concatenate([v * beta, kb * eg], axis=1).astype(BF16), qe=q * eg,
                      k_dec=(k * jnp.exp(g_last - gc)).astype(BF16), dec=jnp.exp(g_last))

    n_factor = c.bit_length() - 1
    if n_factor >= 2:
        for e in st.values():
            e["p"] = _mm3(e["p"], e["p"])
        for _ in range(n_factor - 2):
            for e in st.values():
                e["p"], e["t"] = _neumann_level(e["p"], e["t"], c)
        for e in st.values():
            e["t"] = e["t"] + _mm3(e["t"], e["p"])
    for e in st.values():
        uw = _dot(e["t"].astype(BF16), e["rhs"])
        e["u"] = uw[:, :DV_A]
        e["wq"] = jnp.concatenate([uw[:, DV_A:], e["qe"]], axis=0).astype(BF16)

    state = {(s, h): s_ref[s, h] for s in range(n_seq) for h in range(H_A)}
    for n in range(n_chunk):
        grp = [(s, h) for s in range(n_seq) for h in range(H_A)]
        ws = {g: _dot(st[g[0], n, g[1]]["wq"], state[g].astype(BF16)) for g in grp}
        v_new = {g: (st[g[0], n, g[1]]["u"] - ws[g][:c]).astype(BF16) for g in grp}
        for g in grp:
            s, h = g
            e = st[s, n, h]
            o = ws[g][c:] + _dot(e["qk"], v_new[g])
            state[g] = state[g] * e["dec"] + lax.dot_general(
                e["k_dec"], v_new[g], (((0,), (0,)), ((), ())), preferred_element_type=F32)
            zh = z_ref[s, rows(n), h * DV_A:(h + 1) * DV_A]
            ya_ref[s, rows(n), h * DV_A:(h + 1) * DV_A] = (_rms(o, gn_ref[...]) * _silu(zh)).astype(ya_ref.dtype)
    for (s, h), val in state.items():
        s_ref[s, h] = val


def _gdn(qkv, conv_state, z, ba, s0, conv_w, a_log, dt_bias, g_norm, *, chunk, n_chunk, n_seq, l_valid):
    b, l, cc = qkv.shape
    n = l // chunk
    tb = n_chunk * chunk
    assert l_valid == chunk or n == 1
    assert l % tb == 0 and b % n_seq == 0
    assert chunk & (chunk - 1) == 0
    cst = jnp.pad(conv_state, ((0, 0), (SUBLANES - (CONV_W - 1), 0), (0, 0)))
    bat = ba[..., :2 * H_A].reshape(b, n, chunk, 2 * H_A).transpose(0, 1, 3, 2)
    zeros_h = jnp.zeros((H_A,), F32)
    lane_pad = jnp.zeros((LANES - 2 * H_A,), F32)
    prow = jnp.stack([jnp.concatenate([zeros_h, a_log, lane_pad]),
                      jnp.concatenate([zeros_h, dt_bias, lane_pad])])
    pcol = jnp.broadcast_to(prow[:, :2 * H_A, None], (2, 2 * H_A, LANES))
    return pl.pallas_call(
        functools.partial(_gdn_body, chunk=chunk, n_chunk=n_chunk, n_seq=n_seq, l_valid=l_valid),
        grid=(b // n_seq, l // tb),
        in_specs=[pl.BlockSpec((n_seq, tb, cc), lambda i, j: (i, j, 0)),
                  pl.BlockSpec((n_seq, SUBLANES, cc), lambda i, j: (i, 0, 0)),
                  pl.BlockSpec((n_seq, tb, H_A * DV_A), lambda i, j: (i, j, 0)),
                  pl.BlockSpec((n_seq, tb, LANES), lambda i, j: (i, j, 0)),
                  pl.BlockSpec((n_seq, n_chunk, 2 * H_A, chunk), lambda i, j: (i, j, 0, 0)),
                  pl.BlockSpec((n_seq, H_A, DK_A, DV_A), lambda i, j: (i, 0, 0, 0)),
                  _const_spec((CONV_W, cc)), _const_spec((2, LANES)),
                  _const_spec((2, 2 * H_A, LANES)), _const_spec((1, DV_A))],
        out_specs=[pl.BlockSpec((n_seq, tb, H_A * DV_A), lambda i, j: (i, j, 0)),
                   pl.BlockSpec((n_seq, H_A, DK_A, DV_A), lambda i, j: (i, 0, 0, 0))],
        out_shape=[jax.ShapeDtypeStruct((b, l, H_A * DV_A), BF16),
                   jax.ShapeDtypeStruct((b, H_A, DK_A, DV_A), F32)],
        scratch_shapes=[pltpu.VMEM((n_seq, SUBLANES + tb, cc), F32)],
        compiler_params=_params(("parallel", "arbitrary")),
        name="gdn",
    )(qkv, cst, z, ba, bat, s0, conv_w, prow, pcol, g_norm.reshape(1, DV_A))


def _rel_bucket(n):
    n = jnp.maximum(n, 0)
    max_exact = NUM_BUCKETS // 2
    large = max_exact + (jnp.log(jnp.maximum(n, 1).astype(F32) / max_exact)
                         / math.log(MAX_DISTANCE / max_exact) * (NUM_BUCKETS - max_exact)).astype(jnp.int32)
    large = jnp.minimum(large, NUM_BUCKETS - 1)
    return jnp.where(n < max_exact, n, large)


def _bias_rows(rel, tab_rows):
    onehot = _rel_bucket(rel)[..., None] == jnp.arange(NUM_BUCKETS)
    b = jnp.sum(jnp.where(onehot, tab_rows[..., None, :], 0.0), axis=-1)
    return jnp.where(rel >= 0, b, NEG_INF)


def _lam_of(lp_ref, lam_init):
    lp = lp_ref[...]
    s1 = jnp.sum(lp[0:1] * lp[1:2], axis=-1, keepdims=True)
    s2 = jnp.sum(lp[2:3] * lp[3:4], axis=-1, keepdims=True)
    return jnp.exp(s1) - jnp.exp(s2) + lam_init


def _attn_prompt_body(q_ref, k_ref, v_ref, tiles_ref, lp_ref, sub_ref, o_ref,
                      qq_ref, m_ref, l_ref, acc_ref, *, t, row_block, lam_init):
    i = pl.program_id(2)
    q = q_ref[0]
    lane = lax.broadcasted_iota(jnp.int32, q.shape, 1)
    scale = jnp.asarray(DQK_B ** -0.5, BF16)
    zero = jnp.zeros_like(q)
    qq_ref[0:t, :] = jnp.where(lane < DQK_B, q * scale, zero)
    qq_ref[t:2 * t, :] = jnp.where(lane >= DQK_B, q * scale, zero)
    m_ref[...] = jnp.full(m_ref.shape, NEG_INF, F32)
    l_ref[...] = jnp.zeros(l_ref.shape, F32)
    acc_ref[...] = jnp.zeros(acc_ref.shape, F32)
    n_col = t // LANES

    def step(j, tile):
        start = pl.multiple_of(j * t, t)
        k = k_ref[0, pl.ds(start, t), :]
        v = v_ref[0, pl.ds(start, t), :]
        starts = list(range(0, 2 * t, row_block))
        qk = lambda r: _dot_nt(qq_ref[r:r + row_block, :], k)
        scores = {r: qk(r) for r in starts[:ATTN_LOOKAHEAD]}
        for idx, r0 in enumerate(starts):
            rows = slice(r0, r0 + row_block)
            if idx + ATTN_LOOKAHEAD < len(starts):
                scores[starts[idx + ATTN_LOOKAHEAD]] = qk(starts[idx + ATTN_LOOKAHEAD])
            s = scores.pop(r0)
            if tile is not None:
                s = s + tiles_ref[0, tile, r0 % t:r0 % t + row_block, :]
            cols = [s[:, c * LANES:(c + 1) * LANES] for c in range(n_col)]
            m_prev = m_ref[rows, :]
            m_new = jnp.maximum(m_prev, jnp.max(functools.reduce(jnp.maximum, cols), axis=-1, keepdims=True))
            alpha = jnp.exp(m_prev - m_new)
            ps = [jnp.exp(c - m_new) for c in cols]
            l_ref[rows, :] = alpha * l_ref[rows, :] + functools.reduce(jnp.add, ps)
            p16 = jnp.concatenate([p.astype(BF16) for p in ps], axis=-1)
            acc_ref[rows, :] = alpha * acc_ref[rows, :] + _dot(p16, v)
            m_ref[rows, :] = m_new

    def far(j, carry):
        step(j, None)
        return carry

    lax.fori_loop(0, jnp.maximum(i - 1, 0), far, 0)

    @pl.when(i >= 1)
    def _():
        step(i - 1, 1)

    step(i, 0)

    lam = _lam_of(lp_ref, lam_init)
    acc = acc_ref[...] / jnp.sum(l_ref[...], axis=-1, keepdims=True)
    o = acc[0:t] - lam * acc[t:2 * t]
    o_ref[0] = (_rms(o, sub_ref[...]) * (1.0 - lam_init)).astype(o_ref.dtype)


def _attn_prompt(q, k, v, rel_table, lam_params, subln, *, t, lam_init):
    b, l, _ = q.shape
    nq = l // t
    assert t >= MAX_DISTANCE
    table = rel_table.astype(F32)
    d = jnp.arange(t)[:, None] - jnp.arange(t)[None, :]
    rel = jnp.broadcast_to(jnp.stack([d, d + t]), (H_B, 2, t, t))
    tab_rows = jnp.broadcast_to(table.T[:, None, None, :], (H_B, 2, t, NUM_BUCKETS))
    tiles = _bias_rows(rel, tab_rows) - table[NUM_BUCKETS - 1][:, None, None, None]
    hw = 2 * DQK_B
    return pl.pallas_call(
        functools.partial(_attn_prompt_body, t=t, row_block=min(ATTN_ROW_BLOCK, t), lam_init=lam_init),
        grid=(b, H_B, nq),
        in_specs=[pl.BlockSpec((1, t, hw), lambda bi, h, i: (bi, i, h)),
                  pl.BlockSpec((1, l, hw), lambda bi, h, i: (bi, 0, h)),
                  pl.BlockSpec((1, l, DV_B), lambda bi, h, i: (bi, 0, h)),
                  pl.BlockSpec((1, 2, t, t), lambda bi, h, i: (h, 0, 0, 0)),
                  _const_spec(lam_params.shape), _const_spec((1, DV_B))],
        out_specs=pl.BlockSpec((1, t, DV_B), lambda bi, h, i: (bi, i, h)),
        out_shape=jax.ShapeDtypeStruct((b, l, H_B * DV_B), BF16),
        scratch_shapes=[pltpu.VMEM((2 * t, hw), BF16), pltpu.VMEM((2 * t, LANES), F32),
                        pltpu.VMEM((2 * t, LANES), F32), pltpu.VMEM((2 * t, DV_B), F32)],
        compiler_params=_params(("parallel", "parallel", "arbitrary")),
        name="attn_prompt",
    )(q, k, v, tiles, lam_params, subln.reshape(1, DV_B))


def _attn_prompt_t_body(q_ref, k_ref, vt_ref, tiles_ref, lp_ref, sub_ref, o_ref,
                        qt_ref, m_ref, l_ref, acc_ref, s_ref, *, t, col_block, lam_init):
    i = pl.program_id(2)
    n2 = 2 * t
    qt = q_ref[0].astype(F32).T
    dim = lax.broadcasted_iota(jnp.int32, qt.shape, 0)
    qt_ref[:, 0:t] = jnp.where(dim < DQK_B, qt, 0.0).astype(BF16)
    qt_ref[:, t:n2] = jnp.where(dim >= DQK_B, qt, 0.0).astype(BF16)
    m_ref[...] = jnp.full(m_ref.shape, NEG_INF, F32)
    l_ref[...] = jnp.zeros(l_ref.shape, F32)
    acc_ref[...] = jnp.zeros(acc_ref.shape, F32)
    groups = t // SUBLANES

    def run(blocks):
        kv = []
        for j, _ in blocks:
            start = pl.multiple_of(j * t, t)
            kv.append((k_ref[0, pl.ds(start, t), :], vt_ref[0, j]))
        items = [(b, c0) for b in range(len(blocks)) for c0 in range(0, n2, col_block)]

        def qk(n):
            b, c0 = items[n]
            s = _dot(kv[b][0], qt_ref[:, c0:c0 + col_block])
            if blocks[b][1] is not None:
                s = s + tiles_ref[0, blocks[b][1], :, c0 % t:c0 % t + col_block]
            s_ref[n % (ATTN_LOOKAHEAD + 1)] = s
            return jnp.max(s.reshape(groups, SUBLANES, col_block), axis=0)

        part_max = {n: qk(n) for n in range(min(ATTN_LOOKAHEAD, len(items)))}
        mt = min(ATTN_MXU_TILE, t, col_block)
        for idx, it in enumerate(items):
            b, c0 = it
            cols = slice(c0, c0 + col_block)
            if idx + ATTN_LOOKAHEAD < len(items):
                part_max[idx + ATTN_LOOKAHEAD] = qk(idx + ATTN_LOOKAHEAD)
            s = s_ref.at[idx % (ATTN_LOOKAHEAD + 1)]
            m_prev = m_ref[:, cols]
            m_new = jnp.maximum(m_prev, jnp.max(part_max.pop(idx), axis=0, keepdims=True))
            alpha = jnp.exp2(m_prev - m_new)
            l_new, pv = [], []
            for n0 in range(0, col_block, mt):
                l_n = alpha[:, n0:n0 + mt] * l_ref[:, c0 + n0:c0 + n0 + mt]
                pv_n = None
                for k0 in range(0, t, mt):
                    p3 = jnp.exp2(s[k0:k0 + mt, n0:n0 + mt].reshape(mt // SUBLANES, SUBLANES, mt)
                                  - m_new[None, :, n0:n0 + mt])
                    l_n = l_n + jnp.sum(p3, axis=0)
                    d = _dot(kv[b][1][:, k0:k0 + mt], p3.reshape(mt, mt).astype(BF16))
                    pv_n = d if pv_n is None else pv_n + d
                l_new.append(l_n)
                pv.append(pv_n)
            l_ref[:, cols] = jnp.concatenate(l_new, axis=1)
            acc = acc_ref[:, cols].reshape(DV_B // SUBLANES, SUBLANES, col_block) * alpha[None]
            acc_ref[:, cols] = acc.reshape(DV_B, col_block) + jnp.concatenate(pv, axis=1)
            m_ref[:, cols] = m_new

    n_far = jnp.maximum(i - 1, 0)
    per_trip = ATTN_BLOCKS_PER_TRIP

    def far_group(jj, carry):
        run([(per_trip * jj + u, None) for u in range(per_trip)])
        return carry

    def far_single(j, carry):
        run([(j, None)])
        return carry

    lax.fori_loop(0, n_far // per_trip, far_group, 0)
    lax.fori_loop(n_far - n_far % per_trip, n_far, far_single, 0)

    @pl.when(i >= 1)
    def _():
        run([(i - 1, 1), (i, 0)])

    @pl.when(i == 0)
    def _():
        run([(i, 0)])

    lam = _lam_of(lp_ref, lam_init)
    acc = acc_ref[...] / jnp.sum(l_ref[...], axis=0, keepdims=True)
    o = (acc[:, 0:t] - lam * acc[:, t:n2]).T
    o_ref[0] = (_rms(o, sub_ref[...]) * (1.0 - lam_init)).astype(o_ref.dtype)


def _attn_prompt_t(q, k, vt, rel_table, lam_params, subln, *, t, lam_init):
    b, l, _ = q.shape
    nq = l // t
    assert t >= MAX_DISTANCE
    assert vt.shape == (H_B, b * nq, DV_B, t)
    table = rel_table.astype(F32)
    dist = jnp.arange(-t + 1, 2 * t)[None, None, :]
    f = _bias_rows(jnp.broadcast_to(dist, (H_B, 1, 3 * t - 1)), table.T[:, None, :])[:, 0]
    f = (f - table[NUM_BUCKETS - 1][:, None]) * LOG2E

    def toeplitz(w):
        wp = jnp.pad(w, ((0, 0), (0, 1)))
        g = jnp.tile(wp, (1, t))[:, :t * (2 * t - 1)].reshape(H_B, t, 2 * t - 1)
        return g[:, :, t - 1:]

    tiles = jnp.stack([toeplitz(f[:, :2 * t - 1]), toeplitz(f[:, t:])], axis=1)
    hw = 2 * DQK_B
    return pl.pallas_call(
        functools.partial(_attn_prompt_t_body, t=t, col_block=min(ATTN_COL_BLOCK, t), lam_init=lam_init),
        grid=(b, H_B, nq),
        in_specs=[pl.BlockSpec((1, t, hw), lambda bi, h, i: (bi, i, h)),
                  pl.BlockSpec((1, l, hw), lambda bi, h, i: (bi, 0, h)),
                  pl.BlockSpec((1, nq, DV_B, t), lambda bi, h, i: (h, bi, 0, 0)),
                  pl.BlockSpec((1, 2, t, t), lambda bi, h, i: (h, 0, 0, 0)),
                  _const_spec(lam_params.shape), _const_spec((1, DV_B))],
        out_specs=pl.BlockSpec((1, t, DV_B), lambda bi, h, i: (bi, i, h)),
        out_shape=jax.ShapeDtypeStruct((b, l, H_B * DV_B), BF16),
        scratch_shapes=[pltpu.VMEM((hw, 2 * t), BF16), pltpu.VMEM((SUBLANES, 2 * t), F32),
                        pltpu.VMEM((SUBLANES, 2 * t), F32), pltpu.VMEM((DV_B, 2 * t), F32),
                        pltpu.VMEM((ATTN_LOOKAHEAD + 1, t, min(ATTN_COL_BLOCK, t)), F32)],
        compiler_params=_params(("parallel", "parallel", "arbitrary"), flags=ATTN_SCHEDULER_FLAGS),
        name="attn_prompt",
    )(q, k, vt, tiles, lam_params, subln.reshape(1, DV_B))


def _attn_sample_body(pt_ref, q_ref, kn_ref, vn_ref, bpast_ref, bself_ref, lp_ref, sub_ref, *rest,
                      pages, lam_init):
    k_refs = rest[:pages]
    v_refs = rest[pages:2 * pages]
    o_ref, qf_ref, m_ref, l_ref, acc_ref = rest[2 * pages:]
    j = pl.program_id(1)
    rows = qf_ref.shape[0]
    half = rows // 2
    page_cols = PAGE_SIZE * H_B

    @pl.when(j == 0)
    def _():
        q = q_ref[0].astype(F32)
        r = lax.broadcasted_iota(jnp.int32, q.shape, 0)
        lane = lax.broadcasted_iota(jnp.int32, q.shape, 1)
        qf = jnp.where(lane // DQK_B == r // half, q, 0.0)
        qf_ref[...] = qf
        kn = kn_ref[0]
        vn = vn_ref[0]
        n_self = kn.shape[0]
        s_self = [jnp.sum(qf * kn[c:c + 1], axis=-1, keepdims=True) + bself_ref[:, c:c + 1]
                  for c in range(n_self)]
        m0 = functools.reduce(jnp.maximum, s_self)
        p_self = [jnp.exp2(s - m0) for s in s_self]
        m_ref[...] = m0
        l_ref[...] = functools.reduce(jnp.add, p_self)
        acc_ref[...] = functools.reduce(jnp.add, [p * vn[c:c + 1] for c, p in enumerate(p_self)])

    q16 = qf_ref[...].astype(BF16)
    group = math.gcd(pages, SAMPLE_PAGE_GROUP)
    groups = [range(g, g + group) for g in range(0, pages, group)]
    qk = lambda grp: [_dot_nt(q16, k_refs[c][0].astype(BF16)) for c in grp]
    scores = qk(groups[0])
    m_run, l_run, acc = m_ref[...], l_ref[...], acc_ref[...]
    for gi, grp in enumerate(groups):
        cur = scores
        if gi + 1 < len(groups):
            scores = qk(groups[gi + 1])
        s = (jnp.concatenate(cur, axis=-1)
             + bpast_ref[j, :, grp[0] * page_cols:(grp[-1] + 1) * page_cols])
        m_new = jnp.maximum(m_run, jnp.max(s, axis=-1, keepdims=True))
        alpha = jnp.exp2(m_run - m_new)
        p = jnp.exp2(s - m_new)
        l_run = alpha * l_run + jnp.sum(p, axis=-1, keepdims=True)
        p16 = p.astype(BF16)
        pv = functools.reduce(jnp.add, [_dot(p16[:, i * page_cols:(i + 1) * page_cols], v_refs[c][0].astype(BF16))
                                        for i, c in enumerate(grp)])
        acc = alpha * acc + pv
        m_run = m_new
    m_ref[...], l_ref[...], acc_ref[...] = m_run, l_run, acc

    @pl.when(j == pl.num_programs(1) - 1)
    def _():
        lam = _lam_of(lp_ref, lam_init)
        an = acc_ref[...] / l_ref[...]
        o = an[0:half] - lam * an[half:rows]
        o_ref[0] = _rms(o, sub_ref[...]) * (1.0 - lam_init)


def _attn_sample(q, k_new, v_new, cache_k, cache_v, page_base, page_table, rel_table, lam_params, subln,
                 *, pages, lam_init):
    bd, l_new, width = q.shape
    n_pages = page_table.shape[1]
    past = n_pages * PAGE_SIZE
    half = l_new * H_B
    rows = 2 * half
    page_cols = PAGE_SIZE * H_B
    table = rel_table.astype(F32)
    t_of = (jnp.arange(rows) % half) // H_B
    h_of = jnp.arange(rows) % H_B
    tab_rows = table.T[h_of]
    far = table[NUM_BUCKETS - 1][h_of][:, None]
    own = h_of[:, None, None] == jnp.arange(H_B)[None, None, :]
    rel_past = past + t_of[:, None] - jnp.arange(past)[None, :]
    bpast = jnp.where(own, ((_bias_rows(rel_past, tab_rows) - far) * LOG2E)[:, :, None], NEG_INF)
    n_steps = n_pages // pages
    bpast = bpast.reshape(rows, n_steps, pages * page_cols).transpose(1, 0, 2)
    n_self = l_new * H_B
    t_key = jnp.arange(LANES) // H_B
    rel_self = jnp.where((jnp.arange(LANES)[None, :] < n_self) & (h_of[:, None] == jnp.arange(LANES)[None, :] % H_B),
                         t_of[:, None] - t_key[None, :], -1)
    bself = (_bias_rows(rel_self, tab_rows) - far) * LOG2E

    hd = width // H_B
    q_rows = q.reshape(bd, half, hd)
    q_rows = jnp.concatenate([q_rows, q_rows], axis=1)
    kn = k_new.reshape(bd, n_self, hd)
    vn = v_new.reshape(bd, n_self, hd)

    def page_spec(c):
        return pl.BlockSpec((1, page_cols, hd),
                            lambda b, j, pt: (page_base + pt[b * n_pages + j * pages + c], 0, 0))

    seq_spec = lambda r: pl.BlockSpec((1, r, hd), lambda b, j, pt: (b, 0, 0))
    whole = lambda shape: pl.BlockSpec(shape, lambda b, j, pt: (0,) * len(shape), pipeline_mode=pl.Buffered(1))
    grid_spec = pltpu.PrefetchScalarGridSpec(
        num_scalar_prefetch=1,
        grid=(bd, n_pages // pages),
        in_specs=[seq_spec(rows), seq_spec(n_self), seq_spec(n_self),
                  whole(bpast.shape), whole(bself.shape), whole(lam_params.shape), whole((1, DV_B))]
                 + [page_spec(c) for c in range(pages)] + [page_spec(c) for c in range(pages)],
        out_specs=seq_spec(half),
        scratch_shapes=[pltpu.VMEM((rows, hd), F32), pltpu.VMEM((rows, 1), F32),
                        pltpu.VMEM((rows, 1), F32), pltpu.VMEM((rows, DV_B), F32)])
    out = pl.pallas_call(
        functools.partial(_attn_sample_body, pages=pages, lam_init=lam_init),
        grid_spec=grid_spec,
        out_shape=jax.ShapeDtypeStruct((bd, half, DV_B), F32),
        compiler_params=_params(("parallel", "arbitrary")),
        name="attn_sample",
    )(page_table.reshape(-1), q_rows, kn, vn, bpast, bself, lam_params, subln.reshape(1, DV_B),
      *([cache_k] * pages), *([cache_v] * pages))
    return out.reshape(bd, l_new, width)


def _merge_ffn_body(x_ref, ya_ref, yb_ref, ga_ref, gb_ref, wa_ref, wb_ref, wo_ref, post_ref,
                    pre2_ref, wg_ref, wu_ref, wd_ref, post2_ref, o_ref, *, f_chunk):
    merged = (jax.nn.sigmoid(ga_ref[...]) * _dot(ya_ref[...], wa_ref[...])
              + jax.nn.sigmoid(gb_ref[...]) * _dot(yb_ref[...], wb_ref[...]))
    x = x_ref[...] + _rms(_dot(merged.astype(BF16), wo_ref[...]), post_ref[...])
    o_ref[...] = _ffn_half_step(x, pre2_ref, wg_ref, wu_ref, wd_ref, post2_ref, f_chunk)


def _merge_ffn(x, ya, yb, ga, gb, wa, wb, wo, post, pre2, wg, wu, wd, post2, *, tm):
    t, d = x.shape
    d_ff = wg.shape[1]
    f_chunk = 256 if d_ff % 256 == 0 else d_ff
    row = lambda w: pl.BlockSpec((tm, w), lambda i: (i, 0))
    weights = (wa, wb, wo, post, pre2, wg, wu, wd, post2)
    return pl.pallas_call(
        functools.partial(_merge_ffn_body, f_chunk=f_chunk),
        grid=(t // tm,),
        in_specs=[row(d), row(ya.shape[1]), row(yb.shape[1]), row(d), row(d)]
                 + [_const_spec(w.shape) for w in weights],
        out_specs=row(d),
        out_shape=jax.ShapeDtypeStruct((t, d), F32),
        compiler_params=_params(("parallel",)),
        name="merge_ffn",
    )(x, ya, yb, ga, gb, *weights)


def _lambda_init(layer):
    return 0.8 - 0.6 * math.exp(-0.3 * layer)


def _token_tile(t):
    tm = 512
    while t % tm:
        tm //= 2
    return tm


def _attn_tile(seq):
    return min(512, seq)


def _layer_weights(l, W):
    d = W["w_in"].shape[1]
    bf = lambda a: a.astype(BF16)
    row = lambda a: a.reshape(1, -1).astype(F32)
    widths = {"qkv": H_A * (2 * DK_A + DV_A), "z": H_A * DV_A, "beta": H_A, "alpha": H_A,
              "q": H_B * 2 * DQK_B, "k": H_B * 2 * DQK_B, "v": H_B * DV_B, "ga": d, "gb": d}
    order = ("qkv", "z", "beta", "alpha", "q", "k", "v", "ga", "gb")
    offs, off = {}, 0
    for n in order:
        offs[n] = off
        off += widths[n]
    w_in = W["w_in"][l]
    col = lambda n: w_in[:, offs[n]:offs[n] + widths[n]]
    pad = jnp.zeros((d, LANES - 2 * H_A), w_in.dtype)
    w_r = jnp.concatenate([col("qkv"), col("z"), col("q"), col("k"), col("v"), col("ga"), col("gb"),
                           col("beta"), col("alpha"), pad], axis=1)
    groups = tuple((n, (d if w is None else w), dt) for n, w, dt in _PROJ_OUT)
    lam_params = jnp.stack([W["lam_q1"][l], W["lam_k1"][l], W["lam_q2"][l], W["lam_k2"][l]]).astype(F32)
    return dict(
        ffn1=(row(W["ffn1_pre"][l]), bf(W["ffn1_wg"][l]), bf(W["ffn1_wu"][l]), bf(W["ffn1_wd"][l]),
              row(W["ffn1_post"][l])),
        ffn2=(row(W["ffn2_pre"][l]), bf(W["ffn2_wg"][l]), bf(W["ffn2_wu"][l]), bf(W["ffn2_wd"][l]),
              row(W["ffn2_post"][l])),
        mix_pre=row(W["mix_pre"][l]), w_r=bf(w_r), groups=groups,
        conv_w=W["conv_w"][l].astype(F32), a_log=W["a_log"][l].astype(F32), dt_bias=W["dt_bias"][l].astype(F32),
        gdn_norm=W["gdn_norm"][l].astype(F32), lam_params=lam_params, subln=W["subln"][l].astype(F32),
        w_a=bf(W["w_a"][l]), w_b=bf(W["w_b"][l]), w_out=bf(W["w_out"][l]), mix_post=row(W["mix_post"][l]),
        lam_init=_lambda_init(l))


def _decoder_layer(x, lw, conv_state, ssm_state, attend, attn_tile=None):
    b, l, d = x.shape
    t = b * l
    tm = attn_tile or _token_tile(t)
    x2 = _ffn(x.reshape(t, d), *lw["ffn1"], tm=tm)
    c = _proj(x2, lw["mix_pre"], lw["w_r"], lw["groups"], tm=tm, attn_operands=attn_tile is not None)
    c3 = {n: c[n].reshape(b, l, c[n].shape[-1]) for n in ("qkv", "z", "ba")}

    chunk = min(CHUNK, l)
    if chunk % SUBLANES:
        chunk = -(-chunk // SUBLANES) * SUBLANES
    lp = -(-l // chunk) * chunk
    padl = lambda a: jnp.pad(a, ((0, 0), (0, lp - l), (0, 0))) if lp != l else a
    ya, ssm_new = _gdn(padl(c3["qkv"]), conv_state, padl(c3["z"]), padl(c3["ba"]), ssm_state,
                       lw["conv_w"], lw["a_log"], lw["dt_bias"], lw["gdn_norm"],
                       chunk=chunk, n_chunk=math.gcd(lp // chunk, GDN_CHUNKS_PER_STEP),
                       n_seq=math.gcd(b, GDN_SEQS_PER_STEP) if lp == chunk else 1,
                       l_valid=min(l - (lp - chunk), chunk))
    ya = ya[:, :l]
    if l >= CONV_W - 1:
        conv_new = c3["qkv"][:, l - (CONV_W - 1):]
    else:
        conv_new = jnp.concatenate([conv_state.astype(F32), c3["qkv"]], axis=1)[:, -(CONV_W - 1):]

    yb = attend(c, b, l)
    y = _merge_ffn(x2, ya.reshape(t, -1), yb.reshape(t, -1).astype(BF16), c["ga"], c["gb"],
                   lw["w_a"], lw["w_b"], lw["w_out"], lw["mix_post"], *lw["ffn2"], tm=tm)
    return (y.reshape(b, l, d), c["k"].reshape(b, l, H_B, 2 * DQK_B), c["v"].reshape(b, l, H_B, DV_B),
            conv_new, ssm_new)


def kernel(x_prompt, x_sample, cache_k, cache_v, state_conv, state_ssm, page_table, rel_table,
           ffn1_pre, ffn1_wg, ffn1_wu, ffn1_wd, ffn1_post, mix_pre, w_in, conv_w, a_log, dt_bias,
           gdn_norm, lam_q1, lam_k1, lam_q2, lam_k2, subln, w_a, w_b, w_out, mix_post,
           ffn2_pre, ffn2_wg, ffn2_wu, ffn2_wd, ffn2_post):
    W = dict(ffn1_pre=ffn1_pre, ffn1_wg=ffn1_wg, ffn1_wu=ffn1_wu, ffn1_wd=ffn1_wd, ffn1_post=ffn1_post,
             mix_pre=mix_pre, w_in=w_in, conv_w=conv_w, a_log=a_log, dt_bias=dt_bias, gdn_norm=gdn_norm,
             lam_q1=lam_q1, lam_k1=lam_k1, lam_q2=lam_q2, lam_k2=lam_k2, subln=subln, w_a=w_a, w_b=w_b,
             w_out=w_out, mix_post=mix_post, ffn2_pre=ffn2_pre, ffn2_wg=ffn2_wg, ffn2_wu=ffn2_wu,
             ffn2_wd=ffn2_wd, ffn2_post=ffn2_post)
    depth = w_in.shape[0]
    bp, seq, _ = x_prompt.shape
    n_pool = cache_k.shape[1]
    n_pages = page_table.shape[1]
    t_attn = _attn_tile(seq)
    pages = math.gcd(n_pages, SAMPLE_PAGES_PER_STEP)
    ck = cache_k.reshape(depth * n_pool, PAGE_SIZE * H_B, -1)
    cv = cache_v.reshape(depth * n_pool, PAGE_SIZE * H_B, -1)
    xp, xs = x_prompt, x_sample
    outs = [[] for _ in range(8)]
    for l in range(depth):
        lw = _layer_weights(l, W)

        def attend_prompt(c, b, n):
            return _attn_prompt_t(c["q"].reshape(b, n, -1), c["k16"].reshape(b, n, -1), c["vt"], rel_table,
                                  lw["lam_params"], lw["subln"], t=t_attn, lam_init=lw["lam_init"])

        def attend_sample(c, b, n):
            return _attn_sample(c["q"].reshape(b, n, -1), c["k"], c["v"], ck, cv, l * n_pool, page_table,
                                rel_table, lw["lam_params"], lw["subln"], pages=pages, lam_init=lw["lam_init"])

        zero_conv = jnp.zeros((bp, CONV_W - 1, state_conv.shape[-1]), x_prompt.dtype)
        zero_ssm = jnp.zeros((bp,) + state_ssm.shape[2:], state_ssm.dtype)
        xp, k1, v1, c1, s1 = _decoder_layer(xp, lw, zero_conv, zero_ssm, attend_prompt, attn_tile=t_attn)
        xs, k2, v2, c2, s2 = _decoder_layer(xs, lw, state_conv[l], state_ssm[l], attend_sample)
        for o, a in zip(outs, (k1, v1, c1, s1, k2, v2, c2, s2)):
            o.append(a)
    return (xp, xs) + tuple(jnp.stack(o) for o in outs)
```

```python
import functools
import math

import jax
import jax.numpy as jnp
from jax import lax
from jax.experimental import pallas as pl
from jax.experimental.pallas import tpu as pltpu

F32 = jnp.float32
BF16 = jnp.bfloat16

H_A = 4
DK_A = 128
DV_A = 128
CONV_W = 4
CHUNK = 64
H_B = 4
DQK_B = 64
DV_B = 2 * DQK_B
PAGE_SIZE = 128
NUM_BUCKETS = 32
MAX_DISTANCE = 128
EPS = 1e-6
NEG_INF = -1e30

LANES = 128
SUBLANES = 8
VMEM_LIMIT = 56 * 1024 * 1024

LOG2E = math.log2(math.e)
QK_LOGIT_SCALE = DQK_B ** -0.5 * LOG2E

TOKEN_TILE = 512
FFN_CHUNK = 256
GDN_CHUNKS_PER_STEP = 4
GDN_SEQS_PER_STEP = 8
ATTN_BLOCK = 512
ATTN_COL_BLOCK = 512
ATTN_MXU_TILE = 256
ATTN_LOOKAHEAD = 2
ATTN_BLOCKS_PER_TRIP = 4
SAMPLE_PAGES_PER_STEP = 32
SAMPLE_PAGE_GROUP = 4


def _dot(a, b):
    return jnp.dot(a, b, preferred_element_type=F32)


def _dot_nt(a, b):
    return lax.dot_general(a, b, (((1,), (1,)), ((), ())), preferred_element_type=F32)


def _rms(x, g):
    return x * lax.rsqrt(jnp.mean(x * x, axis=-1, keepdims=True) + EPS) * g


def _silu(x):
    return x * jax.nn.sigmoid(x)


def _const_spec(shape):
    nd = len(shape)
    return pl.BlockSpec(shape, lambda *_: (0,) * nd, pipeline_mode=pl.Buffered(1))


def _params(sem):
    return pltpu.CompilerParams(dimension_semantics=sem, vmem_limit_bytes=VMEM_LIMIT)


def _ffn_half_step(x, pre_ref, wg_ref, wu_ref, wd_ref, post_ref, f_chunk):
    h = _rms(x, pre_ref[...]).astype(BF16)
    d_ff = wg_ref.shape[1]
    acc = jnp.zeros(x.shape, F32)
    for c in range(d_ff // f_chunk):
        sl = slice(c * f_chunk, (c + 1) * f_chunk)
        g = _dot(h, wg_ref[:, sl])
        u = _dot(h, wu_ref[:, sl])
        acc = acc + _dot((_silu(g) * u).astype(BF16), wd_ref[sl, :])
    return x + 0.5 * _rms(acc, post_ref[...])


def _ffn_body(x_ref, pre_ref, wg_ref, wu_ref, wd_ref, post_ref, o_ref, *, f_chunk):
    o_ref[...] = _ffn_half_step(x_ref[...], pre_ref, wg_ref, wu_ref, wd_ref, post_ref, f_chunk)


def _ffn(x, pre, wg, wu, wd, post, *, tm):
    t, d = x.shape
    d_ff = wg.shape[1]
    f_chunk = FFN_CHUNK if d_ff % FFN_CHUNK == 0 else d_ff
    return pl.pallas_call(
        functools.partial(_ffn_body, f_chunk=f_chunk),
        grid=(t // tm,),
        in_specs=[pl.BlockSpec((tm, d), lambda i: (i, 0)),
                  _const_spec((1, d)), _const_spec((d, d_ff)), _const_spec((d, d_ff)),
                  _const_spec((d_ff, d)), _const_spec((1, d))],
        out_specs=pl.BlockSpec((tm, d), lambda i: (i, 0)),
        out_shape=jax.ShapeDtypeStruct((t, d), F32),
        compiler_params=_params(("parallel",)),
        name="ffn",
    )(x, pre, wg, wu, wd, post)


_PROJ_OUT = (("qkv", H_A * (2 * DK_A + DV_A), F32), ("z", H_A * DV_A, F32),
             ("q", H_B * 2 * DQK_B, BF16), ("k", H_B * 2 * DQK_B, F32), ("v", H_B * DV_B, F32),
             ("ga", None, F32), ("gb", None, F32), ("ba", LANES, F32))


def _proj_body(x_ref, g_ref, w_ref, *o_refs, groups, names):
    h = _rms(x_ref[...], g_ref[...]).astype(BF16)
    refs = dict(zip(names, o_refs))
    off = 0
    for name, w, _ in groups:
        y = _dot(h, w_ref[:, off:off + w])
        if name == "q":
            y = y * QK_LOGIT_SCALE
        if name in ("k", "v"):
            hd = w // H_B
            for hh in range(H_B):
                refs[name][:, hh, :] = y[:, hh * hd:(hh + 1) * hd]
            if name == "k" and "k16" in refs:
                refs["k16"][...] = y.astype(BF16)
            if name == "v" and "vt" in refs:
                for hh in range(H_B):
                    refs["vt"][hh, 0] = y[:, hh * hd:(hh + 1) * hd].T.astype(BF16)
        else:
            refs[name][...] = y.astype(refs[name].dtype)
        off += w


def _proj(x, g, w_r, groups, *, tm, attn_operands):
    t, d = x.shape
    row = lambda w: pl.BlockSpec((tm, w), lambda i: (i, 0))
    names, specs, shapes = [], [], []
    for n, w, dt in groups:
        names.append(n)
        if n in ("k", "v"):
            specs.append(pl.BlockSpec((tm, H_B, w // H_B), lambda i: (i, 0, 0)))
            shapes.append(jax.ShapeDtypeStruct((t, H_B, w // H_B), dt))
        else:
            specs.append(row(w))
            shapes.append(jax.ShapeDtypeStruct((t, w), dt))
    if attn_operands:
        wk, wv = groups[3][1], groups[4][1]
        names += ["k16", "vt"]
        specs += [row(wk), pl.BlockSpec((H_B, 1, wv // H_B, tm), lambda i: (0, i, 0, 0))]
        shapes += [jax.ShapeDtypeStruct((t, wk), BF16), jax.ShapeDtypeStruct((H_B, t // tm, wv // H_B, tm), BF16)]
    res = pl.pallas_call(
        functools.partial(_proj_body, groups=groups, names=names),
        grid=(t // tm,),
        in_specs=[row(d), _const_spec((1, d)), _const_spec(w_r.shape)],
        out_specs=specs,
        out_shape=shapes,
        compiler_params=_params(("parallel",)),
        name="proj",
    )(x, g, w_r)
    return dict(zip(names, res))


def _split2(x):
    hi = x.astype(BF16)
    return hi, (x - hi.astype(F32)).astype(BF16)


def _split3(x):
    hi = x.astype(BF16)
    r = x - hi.astype(F32)
    mid = r.astype(BF16)
    return hi, mid, (r - mid.astype(F32)).astype(BF16)


def _mm3(x, y):
    x_hi, x_lo = _split2(x)
    y_hi, y_lo = _split2(y)
    return _dot(x_hi, y_hi) + _dot(x_lo, y_hi) + _dot(x_hi, y_lo)


def _neumann_level(p, t_inv, c):
    if c % 16 == 0:
        x_hi, x_lo = _split2(jnp.concatenate([p, t_inv], axis=0))
        p_hi, p_lo = x_hi[:c], x_lo[:c]
        y = _dot(jnp.concatenate([x_hi, x_lo], axis=0), p_hi)
        y = y[:2 * c] + y[2 * c:] + _dot(x_hi, p_lo)
        return y[:c], t_inv + y[c:]
    p_hi, p_lo = _split2(p)
    t_hi, t_lo = _split2(t_inv)
    pp = _dot(p_hi, p_hi) + _dot(p_lo, p_hi) + _dot(p_hi, p_lo)
    tp = _dot(t_hi, p_hi) + _dot(t_lo, p_hi) + _dot(t_hi, p_lo)
    return pp, t_inv + tp


def _gdn_body(qkv_ref, cst_ref, z_ref, ba_ref, bat_ref, s0_ref, cw_ref, prow_ref, pcol_ref, gn_ref,
              ya_ref, s_ref, ext_ref, *, chunk, n_chunk, n_seq, l_valid):
    c = chunk
    tb = n_chunk * c
    j = pl.program_id(1)

    @pl.when(j == 0)
    def _():
        ext_ref[:, 0:SUBLANES, :] = cst_ref[...]
        s_ref[...] = s0_ref[...]

    ii = lax.broadcasted_iota(jnp.int32, (c, c), 0)
    jj = lax.broadcasted_iota(jnp.int32, (c, c), 1)
    causal = ii >= jj
    strict = ii > jj
    ltri16 = causal.astype(BF16)
    eye = (ii == jj).astype(F32)
    cw = cw_ref[...]
    hk = H_A * DK_A

    rows = lambda n: slice(n * c, (n + 1) * c)
    act, beta_c, g_c = [], [], []
    for s in range(n_seq):
        u_raw = qkv_ref[s]
        ext_ref[s, SUBLANES:SUBLANES + tb, :] = u_raw
        conv = u_raw * cw[CONV_W - 1:CONV_W, :]
        for k in range(1, CONV_W):
            conv = conv + ext_ref[s, SUBLANES - k:SUBLANES - k + tb, :] * cw[CONV_W - 1 - k:CONV_W - k, :]
        ext_ref[s, 0:SUBLANES, :] = ext_ref[s, tb:tb + SUBLANES, :]
        act.append(_silu(conv))
        ba = ba_ref[s]
        beta_s = jax.nn.sigmoid(ba)
        g_s = -jnp.exp(prow_ref[0:1, :]) * jax.nn.softplus(ba + prow_ref[1:2, :])
        if l_valid < c:
            valid = lax.broadcasted_iota(jnp.int32, g_s.shape, 0) < l_valid
            g_s = jnp.where(valid, g_s, 0.0)
            beta_s = jnp.where(valid, beta_s, 0.0)
        beta_c.append(beta_s)
        g_c.append(g_s)

    blocks = [(s, n) for s in range(n_seq) for n in range(n_chunk)]
    chains = [(s, n, h) for s, n in blocks for h in range(H_A)]

    gcum_c, gcum_r = {}, {}
    for s, n in blocks:
        g_r = -jnp.exp(pcol_ref[0][:, :c]) * jax.nn.softplus(bat_ref[s, n] + pcol_ref[1][:, :c])
        if l_valid < c:
            g_r = jnp.where(lax.broadcasted_iota(jnp.int32, g_r.shape, 1) < l_valid, g_r, 0.0)
        gcum_c[s, n] = functools.reduce(jnp.add, [_dot(ltri16, x) for x in _split3(g_c[s][rows(n)])])
        gcum_r[s, n] = functools.reduce(jnp.add, [_dot_nt(x, ltri16) for x in _split3(g_r)])

    st = {}
    for ch in chains:
        s, n, h = ch
        rs = rows(n)
        q_raw = act[s][rs, h * DK_A:(h + 1) * DK_A]
        k_raw = act[s][rs, hk + h * DK_A:hk + (h + 1) * DK_A]
        v = act[s][rs, 2 * hk + h * DV_A:2 * hk + (h + 1) * DV_A]
        q = q_raw * lax.rsqrt(jnp.sum(q_raw * q_raw, axis=-1, keepdims=True) + EPS) * (DK_A ** -0.5)
        k = k_raw * lax.rsqrt(jnp.sum(k_raw * k_raw, axis=-1, keepdims=True) + EPS)
        beta = beta_c[s][rs, h:h + 1]
        gc = gcum_c[s, n][:, H_A + h:H_A + h + 1]
        gr = gcum_r[s, n][H_A + h:H_A + h + 1, :]
        decay = jnp.where(causal, jnp.exp(jnp.where(causal, gc - gr, 0.0)), 0.0)
        kb = k * beta
        kq = _dot_nt(jnp.concatenate([kb, q], axis=0).astype(BF16), k.astype(BF16))
        eg = jnp.exp(gc)
        g_last = gc[c - 1:c, :]
        p = -jnp.where(strict, kq[:c] * decay, 0.0)
        st[ch] = dict(p=p, t=eye + p, qk=(kq[c:] * decay).astype(BF16),
                      rhs=jnp.concatenate([v * beta, kb * eg], axis=1).astype(BF16), qe=q * eg,
                      k_dec=(k * jnp.exp(g_last - gc)).astype(BF16), dec=jnp.exp(g_last))

    n_factor = c.bit_length() - 1
    if n_factor >= 2:
        for e in st.values():
            e["p"] = _mm3(e["p"], e["p"])
        for _ in range(n_factor - 2):
            for e in st.values():
                e["p"], e["t"] = _neumann_level(e["p"], e["t"], c)
        for e in st.values():
            e["t"] = e["t"] + _mm3(e["t"], e["p"])
    for e in st.values():
        uw = _dot(e["t"].astype(BF16), e["rhs"])
        e["u"] = uw[:, :DV_A]
        e["wq"] = jnp.concatenate([uw[:, DV_A:], e["qe"]], axis=0).astype(BF16)

    state = {(s, h): s_ref[s, h] for s in range(n_seq) for h in range(H_A)}
    for n in range(n_chunk):
        grp = [(s, h) for s in range(n_seq) for h in range(H_A)]
        ws = {g: _dot(st[g[0], n, g[1]]["wq"], state[g].astype(BF16)) for g in grp}
        v_new = {g: (st[g[0], n, g[1]]["u"] - ws[g][:c]).astype(BF16) for g in grp}
        for g in grp:
            s, h = g
            e = st[s, n, h]
            o = ws[g][c:] + _dot(e["qk"], v_new[g])
            state[g] = state[g] * e["dec"] + lax.dot_general(
                e["k_dec"], v_new[g], (((0,), (0,)), ((), ())), preferred_element_type=F32)
            zh = z_ref[s, rows(n), h * DV_A:(h + 1) * DV_A]
            ya_ref[s, rows(n), h * DV_A:(h + 1) * DV_A] = (_rms(o, gn_ref[...]) * _silu(zh)).astype(ya_ref.dtype)
    for (s, h), val in state.items():
        s_ref[s, h] = val


def _gdn(qkv, conv_state, z, ba, s0, conv_w, a_log, dt_bias, g_norm, *, chunk, n_chunk, n_seq, l_valid):
    b, l, cc = qkv.shape
    n = l // chunk
    tb = n_chunk * chunk
    assert l_valid == chunk or n == 1
    assert l % tb == 0 and b % n_seq == 0
    assert chunk & (chunk - 1) == 0
    cst = jnp.pad(conv_state, ((0, 0), (SUBLANES - (CONV_W - 1), 0), (0, 0)))
    bat = ba[..., :2 * H_A].reshape(b, n, chunk, 2 * H_A).transpose(0, 1, 3, 2)
    zeros_h = jnp.zeros((H_A,), F32)
    lane_pad = jnp.zeros((LANES - 2 * H_A,), F32)
    prow = jnp.stack([jnp.concatenate([zeros_h, a_log, lane_pad]),
                      jnp.concatenate([zeros_h, dt_bias, lane_pad])])
    pcol = jnp.broadcast_to(prow[:, :2 * H_A, None], (2, 2 * H_A, LANES))
    return pl.pallas_call(
        functools.partial(_gdn_body, chunk=chunk, n_chunk=n_chunk, n_seq=n_seq, l_valid=l_valid),
        grid=(b // n_seq, l // tb),
        in_specs=[pl.BlockSpec((n_seq, tb, cc), lambda i, j: (i, j, 0)),
                  pl.BlockSpec((n_seq, SUBLANES, cc), lambda i, j: (i, 0, 0)),
                  pl.BlockSpec((n_seq, tb, H_A * DV_A), lambda i, j: (i, j, 0)),
                  pl.BlockSpec((n_seq, tb, LANES), lambda i, j: (i, j, 0)),
                  pl.BlockSpec((n_seq, n_chunk, 2 * H_A, chunk), lambda i, j: (i, j, 0, 0)),
                  pl.BlockSpec((n_seq, H_A, DK_A, DV_A), lambda i, j: (i, 0, 0, 0)),
                  _const_spec((CONV_W, cc)), _const_spec((2, LANES)),
                  _const_spec((2, 2 * H_A, LANES)), _const_spec((1, DV_A))],
        out_specs=[pl.BlockSpec((n_seq, tb, H_A * DV_A), lambda i, j: (i, j, 0)),
                   pl.BlockSpec((n_seq, H_A, DK_A, DV_A), lambda i, j: (i, 0, 0, 0))],
        out_shape=[jax.ShapeDtypeStruct((b, l, H_A * DV_A), BF16),
                   jax.ShapeDtypeStruct((b, H_A, DK_A, DV_A), F32)],
        scratch_shapes=[pltpu.VMEM((n_seq, SUBLANES + tb, cc), F32)],
        compiler_params=_params(("parallel", "arbitrary")),
        name="gdn",
    )(qkv, cst, z, ba, bat, s0, conv_w, prow, pcol, g_norm.reshape(1, DV_A))


def _rel_bucket(n):
    n = jnp.maximum(n, 0)
    max_exact = NUM_BUCKETS // 2
    large = max_exact + (jnp.log(jnp.maximum(n, 1).astype(F32) / max_exact)
                         / math.log(MAX_DISTANCE / max_exact) * (NUM_BUCKETS - max_exact)).astype(jnp.int32)
    large = jnp.minimum(large, NUM_BUCKETS - 1)
    return jnp.where(n < max_exact, n, large)


def _bias_rows(rel, tab_rows):
    onehot = _rel_bucket(rel)[..., None] == jnp.arange(NUM_BUCKETS)
    b = jnp.sum(jnp.where(onehot, tab_rows[..., None, :], 0.0), axis=-1)
    return jnp.where(rel >= 0, b, NEG_INF)


def _lam_of(lp_ref, lam_init):
    lp = lp_ref[...]
    s1 = jnp.sum(lp[0:1] * lp[1:2], axis=-1, keepdims=True)
    s2 = jnp.sum(lp[2:3] * lp[3:4], axis=-1, keepdims=True)
    return jnp.exp(s1) - jnp.exp(s2) + lam_init


def _attn_prompt_body(q_ref, k_ref, vt_ref, tiles_ref, lp_ref, sub_ref, o_ref,
                        qt_ref, m_ref, l_ref, acc_ref, s_ref, *, t, col_block, lam_init):
    i = pl.program_id(2)
    n2 = 2 * t
    qt = q_ref[0].astype(F32).T
    dim = lax.broadcasted_iota(jnp.int32, qt.shape, 0)
    qt_ref[:, 0:t] = jnp.where(dim < DQK_B, qt, 0.0).astype(BF16)
    qt_ref[:, t:n2] = jnp.where(dim >= DQK_B, qt, 0.0).astype(BF16)
    m_ref[...] = jnp.full(m_ref.shape, NEG_INF, F32)
    l_ref[...] = jnp.zeros(l_ref.shape, F32)
    acc_ref[...] = jnp.zeros(acc_ref.shape, F32)
    groups = t // SUBLANES

    def run(blocks):
        kv = []
        for j, _ in blocks:
            start = pl.multiple_of(j * t, t)
            kv.append((k_ref[0, pl.ds(start, t), :], vt_ref[0, j]))
        items = [(b, c0) for b in range(len(blocks)) for c0 in range(0, n2, col_block)]

        def qk(n):
            b, c0 = items[n]
            s = _dot(kv[b][0], qt_ref[:, c0:c0 + col_block])
            if blocks[b][1] is not None:
                s = s + tiles_ref[0, blocks[b][1], :, c0 % t:c0 % t + col_block]
            s_ref[n % (ATTN_LOOKAHEAD + 1)] = s
            return jnp.max(s.reshape(groups, SUBLANES, col_block), axis=0)

        part_max = {n: qk(n) for n in range(min(ATTN_LOOKAHEAD, len(items)))}
        mt = min(ATTN_MXU_TILE, t, col_block)
        for idx, it in enumerate(items):
            b, c0 = it
            cols = slice(c0, c0 + col_block)
            if idx + ATTN_LOOKAHEAD < len(items):
                part_max[idx + ATTN_LOOKAHEAD] = qk(idx + ATTN_LOOKAHEAD)
            s = s_ref.at[idx % (ATTN_LOOKAHEAD + 1)]
            m_prev = m_ref[:, cols]
            m_new = jnp.maximum(m_prev, jnp.max(part_max.pop(idx), axis=0, keepdims=True))
            alpha = jnp.exp2(m_prev - m_new)
            l_new, pv = [], []
            for n0 in range(0, col_block, mt):
                l_n = alpha[:, n0:n0 + mt] * l_ref[:, c0 + n0:c0 + n0 + mt]
                pv_n = None
                for k0 in range(0, t, mt):
                    p3 = jnp.exp2(s[k0:k0 + mt, n0:n0 + mt].reshape(mt // SUBLANES, SUBLANES, mt)
                                  - m_new[None, :, n0:n0 + mt])
                    l_n = l_n + jnp.sum(p3, axis=0)
                    d = _dot(kv[b][1][:, k0:k0 + mt], p3.reshape(mt, mt).astype(BF16))
                    pv_n = d if pv_n is None else pv_n + d
                l_new.append(l_n)
                pv.append(pv_n)
            l_ref[:, cols] = jnp.concatenate(l_new, axis=1)
            acc = acc_ref[:, cols].reshape(DV_B // SUBLANES, SUBLANES, col_block) * alpha[None]
            acc_ref[:, cols] = acc.reshape(DV_B, col_block) + jnp.concatenate(pv, axis=1)
            m_ref[:, cols] = m_new

    n_far = jnp.maximum(i - 1, 0)
    per_trip = ATTN_BLOCKS_PER_TRIP

    def far_group(jj, carry):
        run([(per_trip * jj + u, None) for u in range(per_trip)])
        return carry

    def far_single(j, carry):
        run([(j, None)])
        return carry

    lax.fori_loop(0, n_far // per_trip, far_group, 0)
    lax.fori_loop(n_far - n_far % per_trip, n_far, far_single, 0)

    @pl.when(i >= 1)
    def _():
        run([(i - 1, 1), (i, 0)])

    @pl.when(i == 0)
    def _():
        run([(i, 0)])

    lam = _lam_of(lp_ref, lam_init)
    acc = acc_ref[...] / jnp.sum(l_ref[...], axis=0, keepdims=True)
    o = (acc[:, 0:t] - lam * acc[:, t:n2]).T
    o_ref[0] = (_rms(o, sub_ref[...]) * (1.0 - lam_init)).astype(o_ref.dtype)


def _attn_prompt(q, k, vt, rel_table, lam_params, subln, *, t, lam_init):
    b, l, _ = q.shape
    nq = l // t
    assert t >= MAX_DISTANCE
    assert vt.shape == (H_B, b * nq, DV_B, t)
    table = rel_table.astype(F32)
    dist = jnp.arange(-t + 1, 2 * t)[None, None, :]
    f = _bias_rows(jnp.broadcast_to(dist, (H_B, 1, 3 * t - 1)), table.T[:, None, :])[:, 0]
    f = (f - table[NUM_BUCKETS - 1][:, None]) * LOG2E

    def toeplitz(w):
        wp = jnp.pad(w, ((0, 0), (0, 1)))
        g = jnp.tile(wp, (1, t))[:, :t * (2 * t - 1)].reshape(H_B, t, 2 * t - 1)
        return g[:, :, t - 1:]

    tiles = jnp.stack([toeplitz(f[:, :2 * t - 1]), toeplitz(f[:, t:])], axis=1)
    hw = 2 * DQK_B
    return pl.pallas_call(
        functools.partial(_attn_prompt_body, t=t, col_block=min(ATTN_COL_BLOCK, t), lam_init=lam_init),
        grid=(b, H_B, nq),
        in_specs=[pl.BlockSpec((1, t, hw), lambda bi, h, i: (bi, i, h)),
                  pl.BlockSpec((1, l, hw), lambda bi, h, i: (bi, 0, h)),
                  pl.BlockSpec((1, nq, DV_B, t), lambda bi, h, i: (h, bi, 0, 0)),
                  pl.BlockSpec((1, 2, t, t), lambda bi, h, i: (h, 0, 0, 0)),
                  _const_spec(lam_params.shape), _const_spec((1, DV_B))],
        out_specs=pl.BlockSpec((1, t, DV_B), lambda bi, h, i: (bi, i, h)),
        out_shape=jax.ShapeDtypeStruct((b, l, H_B * DV_B), BF16),
        scratch_shapes=[pltpu.VMEM((hw, 2 * t), BF16), pltpu.VMEM((SUBLANES, 2 * t), F32),
                        pltpu.VMEM((SUBLANES, 2 * t), F32), pltpu.VMEM((DV_B, 2 * t), F32),
                        pltpu.VMEM((ATTN_LOOKAHEAD + 1, t, min(ATTN_COL_BLOCK, t)), F32)],
        compiler_params=_params(("parallel", "parallel", "arbitrary")),
        name="attn_prompt",
    )(q, k, vt, tiles, lam_params, subln.reshape(1, DV_B))


def _attn_sample_body(pt_ref, q_ref, kn_ref, vn_ref, bpast_ref, bself_ref, lp_ref, sub_ref, *rest,
                      pages, lam_init):
    k_refs = rest[:pages]
    v_refs = rest[pages:2 * pages]
    o_ref, qf_ref, m_ref, l_ref, acc_ref = rest[2 * pages:]
    j = pl.program_id(1)
    rows = qf_ref.shape[0]
    half = rows // 2
    page_cols = PAGE_SIZE * H_B

    @pl.when(j == 0)
    def _():
        q = q_ref[0].astype(F32)
        r = lax.broadcasted_iota(jnp.int32, q.shape, 0)
        lane = lax.broadcasted_iota(jnp.int32, q.shape, 1)
        qf = jnp.where(lane // DQK_B == r // half, q, 0.0)
        qf_ref[...] = qf
        kn = kn_ref[0]
        vn = vn_ref[0]
        n_self = kn.shape[0]
        s_self = [jnp.sum(qf * kn[c:c + 1], axis=-1, keepdims=True) + bself_ref[:, c:c + 1]
                  for c in range(n_self)]
        m0 = functools.reduce(jnp.maximum, s_self)
        p_self = [jnp.exp2(s - m0) for s in s_self]
        m_ref[...] = m0
        l_ref[...] = functools.reduce(jnp.add, p_self)
        acc_ref[...] = functools.reduce(jnp.add, [p * vn[c:c + 1] for c, p in enumerate(p_self)])

    q16 = qf_ref[...].astype(BF16)
    group = math.gcd(pages, SAMPLE_PAGE_GROUP)
    groups = [range(g, g + group) for g in range(0, pages, group)]
    qk = lambda grp: [_dot_nt(q16, k_refs[c][0].astype(BF16)) for c in grp]
    scores = qk(groups[0])
    m_run, l_run, acc = m_ref[...], l_ref[...], acc_ref[...]
    for gi, grp in enumerate(groups):
        cur = scores
        if gi + 1 < len(groups):
            scores = qk(groups[gi + 1])
        s = (jnp.concatenate(cur, axis=-1)
             + bpast_ref[j, :, grp[0] * page_cols:(grp[-1] + 1) * page_cols])
        m_new = jnp.maximum(m_run, jnp.max(s, axis=-1, keepdims=True))
        alpha = jnp.exp2(m_run - m_new)
        p = jnp.exp2(s - m_new)
        l_run = alpha * l_run + jnp.sum(p, axis=-1, keepdims=True)
        p16 = p.astype(BF16)
        pv = functools.reduce(jnp.add, [_dot(p16[:, i * page_cols:(i + 1) * page_cols], v_refs[c][0].astype(BF16))
                                        for i, c in enumerate(grp)])
        acc = alpha * acc + pv
        m_run = m_new
    m_ref[...], l_ref[...], acc_ref[...] = m_run, l_run, acc

    @pl.when(j == pl.num_programs(1) - 1)
    def _():
        lam = _lam_of(lp_ref, lam_init)
        an = acc_ref[...] / l_ref[...]
        o = an[0:half] - lam * an[half:rows]
        o_ref[0] = _rms(o, sub_ref[...]) * (1.0 - lam_init)


def _attn_sample(q, k_new, v_new, cache_k, cache_v, page_base, page_table, rel_table, lam_params, subln,
                 *, pages, lam_init):
    bd, l_new, width = q.shape
    n_pages = page_table.shape[1]
    past = n_pages * PAGE_SIZE
    half = l_new * H_B
    rows = 2 * half
    page_cols = PAGE_SIZE * H_B
    table = rel_table.astype(F32)
    t_of = (jnp.arange(rows) % half) // H_B
    h_of = jnp.arange(rows) % H_B
    tab_rows = table.T[h_of]
    far = table[NUM_BUCKETS - 1][h_of][:, None]
    own = h_of[:, None, None] == jnp.arange(H_B)[None, None, :]
    rel_past = past + t_of[:, None] - jnp.arange(past)[None, :]
    bpast = jnp.where(own, ((_bias_rows(rel_past, tab_rows) - far) * LOG2E)[:, :, None], NEG_INF)
    n_steps = n_pages // pages
    bpast = bpast.reshape(rows, n_steps, pages * page_cols).transpose(1, 0, 2)
    n_self = l_new * H_B
    t_key = jnp.arange(LANES) // H_B
    rel_self = jnp.where((jnp.arange(LANES)[None, :] < n_self) & (h_of[:, None] == jnp.arange(LANES)[None, :] % H_B),
                         t_of[:, None] - t_key[None, :], -1)
    bself = (_bias_rows(rel_self, tab_rows) - far) * LOG2E

    hd = width // H_B
    q_rows = q.reshape(bd, half, hd)
    q_rows = jnp.concatenate([q_rows, q_rows], axis=1)
    kn = k_new.reshape(bd, n_self, hd)
    vn = v_new.reshape(bd, n_self, hd)

    def page_spec(c):
        return pl.BlockSpec((1, page_cols, hd),
                            lambda b, j, pt: (page_base + pt[b * n_pages + j * pages + c], 0, 0))

    seq_spec = lambda r: pl.BlockSpec((1, r, hd), lambda b, j, pt: (b, 0, 0))
    whole = lambda shape: pl.BlockSpec(shape, lambda b, j, pt: (0,) * len(shape), pipeline_mode=pl.Buffered(1))
    grid_spec = pltpu.PrefetchScalarGridSpec(
        num_scalar_prefetch=1,
        grid=(bd, n_pages // pages),
        in_specs=[seq_spec(rows), seq_spec(n_self), seq_spec(n_self),
                  whole(bpast.shape), whole(bself.shape), whole(lam_params.shape), whole((1, DV_B))]
                 + [page_spec(c) for c in range(pages)] + [page_spec(c) for c in range(pages)],
        out_specs=seq_spec(half),
        scratch_shapes=[pltpu.VMEM((rows, hd), F32), pltpu.VMEM((rows, 1), F32),
                        pltpu.VMEM((rows, 1), F32), pltpu.VMEM((rows, DV_B), F32)])
    out = pl.pallas_call(
        functools.partial(_attn_sample_body, pages=pages, lam_init=lam_init),
        grid_spec=grid_spec,
        out_shape=jax.ShapeDtypeStruct((bd, half, DV_B), F32),
        compiler_params=_params(("parallel", "arbitrary")),
        name="attn_sample",
    )(page_table.reshape(-1), q_rows, kn, vn, bpast, bself, lam_params, subln.reshape(1, DV_B),
      *([cache_k] * pages), *([cache_v] * pages))
    return out.reshape(bd, l_new, width)


def _merge_ffn_body(x_ref, ya_ref, yb_ref, ga_ref, gb_ref, wa_ref, wb_ref, wo_ref, post_ref,
                    pre2_ref, wg_ref, wu_ref, wd_ref, post2_ref, o_ref, *, f_chunk):
    merged = (jax.nn.sigmoid(ga_ref[...]) * _dot(ya_ref[...], wa_ref[...])
              + jax.nn.sigmoid(gb_ref[...]) * _dot(yb_ref[...], wb_ref[...]))
    x = x_ref[...] + _rms(_dot(merged.astype(BF16), wo_ref[...]), post_ref[...])
    o_ref[...] = _ffn_half_step(x, pre2_ref, wg_ref, wu_ref, wd_ref, post2_ref, f_chunk)


def _merge_ffn(x, ya, yb, ga, gb, wa, wb, wo, post, pre2, wg, wu, wd, post2, *, tm):
    t, d = x.shape
    d_ff = wg.shape[1]
    f_chunk = FFN_CHUNK if d_ff % FFN_CHUNK == 0 else d_ff
    row = lambda w: pl.BlockSpec((tm, w), lambda i: (i, 0))
    weights = (wa, wb, wo, post, pre2, wg, wu, wd, post2)
    return pl.pallas_call(
        functools.partial(_merge_ffn_body, f_chunk=f_chunk),
        grid=(t // tm,),
        in_specs=[row(d), row(ya.shape[1]), row(yb.shape[1]), row(d), row(d)]
                 + [_const_spec(w.shape) for w in weights],
        out_specs=row(d),
        out_shape=jax.ShapeDtypeStruct((t, d), F32),
        compiler_params=_params(("parallel",)),
        name="merge_ffn",
    )(x, ya, yb, ga, gb, *weights)


def _lambda_init(layer):
    return 0.8 - 0.6 * math.exp(-0.3 * layer)


def _token_tile(t):
    tm = TOKEN_TILE
    while t % tm:
        tm //= 2
    return tm


def _attn_tile(seq):
    return min(ATTN_BLOCK, seq)


def _layer_weights(l, W):
    d = W["w_in"].shape[1]
    bf = lambda a: a.astype(BF16)
    row = lambda a: a.reshape(1, -1).astype(F32)
    widths = {"qkv": H_A * (2 * DK_A + DV_A), "z": H_A * DV_A, "beta": H_A, "alpha": H_A,
              "q": H_B * 2 * DQK_B, "k": H_B * 2 * DQK_B, "v": H_B * DV_B, "ga": d, "gb": d}
    order = ("qkv", "z", "beta", "alpha", "q", "k", "v", "ga", "gb")
    offs, off = {}, 0
    for n in order:
        offs[n] = off
        off += widths[n]
    w_in = W["w_in"][l]
    col = lambda n: w_in[:, offs[n]:offs[n] + widths[n]]
    pad = jnp.zeros((d, LANES - 2 * H_A), w_in.dtype)
    w_r = jnp.concatenate([col("qkv"), col("z"), col("q"), col("k"), col("v"), col("ga"), col("gb"),
                           col("beta"), col("alpha"), pad], axis=1)
    groups = tuple((n, (d if w is None else w), dt) for n, w, dt in _PROJ_OUT)
    lam_params = jnp.stack([W["lam_q1"][l], W["lam_k1"][l], W["lam_q2"][l], W["lam_k2"][l]]).astype(F32)
    return dict(
        ffn1=(row(W["ffn1_pre"][l]), bf(W["ffn1_wg"][l]), bf(W["ffn1_wu"][l]), bf(W["ffn1_wd"][l]),
              row(W["ffn1_post"][l])),
        ffn2=(row(W["ffn2_pre"][l]), bf(W["ffn2_wg"][l]), bf(W["ffn2_wu"][l]), bf(W["ffn2_wd"][l]),
              row(W["ffn2_post"][l])),
        mix_pre=row(W["mix_pre"][l]), w_r=bf(w_r), groups=groups,
        conv_w=W["conv_w"][l].astype(F32), a_log=W["a_log"][l].astype(F32), dt_bias=W["dt_bias"][l].astype(F32),
        gdn_norm=W["gdn_norm"][l].astype(F32), lam_params=lam_params, subln=W["subln"][l].astype(F32),
        w_a=bf(W["w_a"][l]), w_b=bf(W["w_b"][l]), w_out=bf(W["w_out"][l]), mix_post=row(W["mix_post"][l]),
        lam_init=_lambda_init(l))


def _decoder_layer(x, lw, conv_state, ssm_state, attend, attn_tile=None):
    b, l, d = x.shape
    t = b * l
    tm = attn_tile or _token_tile(t)
    x2 = _ffn(x.reshape(t, d), *lw["ffn1"], tm=tm)
    c = _proj(x2, lw["mix_pre"], lw["w_r"], lw["groups"], tm=tm, attn_operands=attn_tile is not None)
    c3 = {n: c[n].reshape(b, l, c[n].shape[-1]) for n in ("qkv", "z", "ba")}

    chunk = min(CHUNK, l)
    if chunk % SUBLANES:
        chunk = -(-chunk // SUBLANES) * SUBLANES
    lp = -(-l // chunk) * chunk
    padl = lambda a: jnp.pad(a, ((0, 0), (0, lp - l), (0, 0))) if lp != l else a
    ya, ssm_new = _gdn(padl(c3["qkv"]), conv_state, padl(c3["z"]), padl(c3["ba"]), ssm_state,
                       lw["conv_w"], lw["a_log"], lw["dt_bias"], lw["gdn_norm"],
                       chunk=chunk, n_chunk=math.gcd(lp // chunk, GDN_CHUNKS_PER_STEP),
                       n_seq=math.gcd(b, GDN_SEQS_PER_STEP) if lp == chunk else 1,
                       l_valid=min(l - (lp - chunk), chunk))
    ya = ya[:, :l]
    if l >= CONV_W - 1:
        conv_new = c3["qkv"][:, l - (CONV_W - 1):]
    else:
        conv_new = jnp.concatenate([conv_state.astype(F32), c3["qkv"]], axis=1)[:, -(CONV_W - 1):]

    yb = attend(c, b, l)
    y = _merge_ffn(x2, ya.reshape(t, -1), yb.reshape(t, -1).astype(BF16), c["ga"], c["gb"],
                   lw["w_a"], lw["w_b"], lw["w_out"], lw["mix_post"], *lw["ffn2"], tm=tm)
    return (y.reshape(b, l, d), c["k"].reshape(b, l, H_B, 2 * DQK_B), c["v"].reshape(b, l, H_B, DV_B),
            conv_new, ssm_new)


def kernel(x_prompt, x_sample, cache_k, cache_v, state_conv, state_ssm, page_table, rel_table,
           ffn1_pre, ffn1_wg, ffn1_wu, ffn1_wd, ffn1_post, mix_pre, w_in, conv_w, a_log, dt_bias,
           gdn_norm, lam_q1, lam_k1, lam_q2, lam_k2, subln, w_a, w_b, w_out, mix_post,
           ffn2_pre, ffn2_wg, ffn2_wu, ffn2_wd, ffn2_post):
    W = dict(ffn1_pre=ffn1_pre, ffn1_wg=ffn1_wg, ffn1_wu=ffn1_wu, ffn1_wd=ffn1_wd, ffn1_post=ffn1_post,
             mix_pre=mix_pre, w_in=w_in, conv_w=conv_w, a_log=a_log, dt_bias=dt_bias, gdn_norm=gdn_norm,
             lam_q1=lam_q1, lam_k1=lam_k1, lam_q2=lam_q2, lam_k2=lam_k2, subln=subln, w_a=w_a, w_b=w_b,
             w_out=w_out, mix_post=mix_post, ffn2_pre=ffn2_pre, ffn2_wg=ffn2_wg, ffn2_wu=ffn2_wu,
             ffn2_wd=ffn2_wd, ffn2_post=ffn2_post)
    depth = w_in.shape[0]
    bp, seq, _ = x_prompt.shape
    n_pool = cache_k.shape[1]
    n_pages = page_table.shape[1]
    t_attn = _attn_tile(seq)
    pages = math.gcd(n_pages, SAMPLE_PAGES_PER_STEP)
    ck = cache_k.reshape(depth * n_pool, PAGE_SIZE * H_B, -1)
    cv = cache_v.reshape(depth * n_pool, PAGE_SIZE * H_B, -1)
    xp, xs = x_prompt, x_sample
    outs = [[] for _ in range(8)]
    for l in range(depth):
        lw = _layer_weights(l, W)

        def attend_prompt(c, b, n):
            return _attn_prompt(c["q"].reshape(b, n, -1), c["k16"].reshape(b, n, -1), c["vt"], rel_table,
                                lw["lam_params"], lw["subln"], t=t_attn, lam_init=lw["lam_init"])

        def attend_sample(c, b, n):
            return _attn_sample(c["q"].reshape(b, n, -1), c["k"], c["v"], ck, cv, l * n_pool, page_table,
                                rel_table, lw["lam_params"], lw["subln"], pages=pages, lam_init=lw["lam_init"])

        zero_conv = jnp.zeros((bp, CONV_W - 1, state_conv.shape[-1]), x_prompt.dtype)
        zero_ssm = jnp.zeros((bp,) + state_ssm.shape[2:], state_ssm.dtype)
        xp, k1, v1, c1, s1 = _decoder_layer(xp, lw, zero_conv, zero_ssm, attend_prompt, attn_tile=t_attn)
        xs, k2, v2, c2, s2 = _decoder_layer(xs, lw, state_conv[l], state_ssm[l], attend_sample)
        for o, a in zip(outs, (k1, v1, c1, s1, k2, v2, c2, s2)):
            o.append(a)
    return (xp, xs) + tuple(jnp.stack(o) for o in outs)
```

```python
import functools
import math

import jax
import jax.numpy as jnp
from jax import lax
from jax.experimental import pallas as pl
from jax.experimental.pallas import tpu as pltpu

F32 = jnp.float32
BF16 = jnp.bfloat16

H_A = 4
DK_A = 128
DV_A = 128
CONV_W = 4
CHUNK = 64
H_B = 4
DQK_B = 64
DV_B = 2 * DQK_B
PAGE_SIZE = 128
NUM_BUCKETS = 32
MAX_DISTANCE = 128
EPS = 1e-6
NEG_INF = -1e30

LANES = 128
SUBLANES = 8
VMEM_LIMIT = 56 * 1024 * 1024

LOG2E = math.log2(math.e)
QK_LOGIT_SCALE = DQK_B ** -0.5 * LOG2E

TOKEN_TILE = 512
FFN_CHUNK = 256
GDN_CHUNKS_PER_STEP = 4
GDN_SEQS_PER_STEP = 8
ATTN_BLOCK = 512
ATTN_COL_BLOCK = 512
ATTN_MXU_TILE = 256
ATTN_LOOKAHEAD = 2
ATTN_BLOCKS_PER_TRIP = 4
SAMPLE_PAGES_PER_STEP = 32
SAMPLE_PAGE_GROUP = 4


def _dot(a, b):
    return jnp.dot(a, b, preferred_element_type=F32)


def _dot_nt(a, b):
    return lax.dot_general(a, b, (((1,), (1,)), ((), ())), preferred_element_type=F32)


def _rms(x, g):
    return x * lax.rsqrt(jnp.mean(x * x, axis=-1, keepdims=True) + EPS) * g


def _silu(x):
    return x * jax.nn.sigmoid(x)


def _const_spec(shape):
    nd = len(shape)
    return pl.BlockSpec(shape, lambda *_: (0,) * nd, pipeline_mode=pl.Buffered(1))


def _params(sem):
    return pltpu.CompilerParams(dimension_semantics=sem, vmem_limit_bytes=VMEM_LIMIT)


def _ffn_half_step(x, pre_ref, wg_ref, wu_ref, wd_ref, post_ref, f_chunk):
    h = _rms(x, pre_ref[...]).astype(BF16)
    d_ff = wg_ref.shape[1]
    acc = jnp.zeros(x.shape, F32)
    for c in range(d_ff // f_chunk):
        sl = slice(c * f_chunk, (c + 1) * f_chunk)
        g = _dot(h, wg_ref[:, sl])
        u = _dot(h, wu_ref[:, sl])
        acc = acc + _dot((_silu(g) * u).astype(BF16), wd_ref[sl, :])
    return x + 0.5 * _rms(acc, post_ref[...])


def _ffn_body(x_ref, pre_ref, wg_ref, wu_ref, wd_ref, post_ref, o_ref, *, f_chunk):
    o_ref[...] = _ffn_half_step(x_ref[...], pre_ref, wg_ref, wu_ref, wd_ref, post_ref, f_chunk)


def _ffn(x, pre, wg, wu, wd, post, *, tm):
    t, d = x.shape
    d_ff = wg.shape[1]
    f_chunk = FFN_CHUNK if d_ff % FFN_CHUNK == 0 else d_ff
    return pl.pallas_call(
        functools.partial(_ffn_body, f_chunk=f_chunk),
        grid=(t // tm,),
        in_specs=[pl.BlockSpec((tm, d), lambda i: (i, 0)),
                  _const_spec((1, d)), _const_spec((d, d_ff)), _const_spec((d, d_ff)),
                  _const_spec((d_ff, d)), _const_spec((1, d))],
        out_specs=pl.BlockSpec((tm, d), lambda i: (i, 0)),
        out_shape=jax.ShapeDtypeStruct((t, d), F32),
        compiler_params=_params(("parallel",)),
        name="ffn",
    )(x, pre, wg, wu, wd, post)


_PROJ_OUT = (("qkv", H_A * (2 * DK_A + DV_A), F32), ("z", H_A * DV_A, F32),
             ("q", H_B * 2 * DQK_B, BF16), ("k", H_B * 2 * DQK_B, F32), ("v", H_B * DV_B, F32),
             ("ga", None, F32), ("gb", None, F32), ("ba", LANES, F32))


def _proj_body(x_ref, g_ref, w_ref, *o_refs, groups, names):
    h = _rms(x_ref[...], g_ref[...]).astype(BF16)
    refs = dict(zip(names, o_refs))
    off = 0
    for name, w, _ in groups:
        y = _dot(h, w_ref[:, off:off + w])
        if name == "q":
            y = y * QK_LOGIT_SCALE
        if name in ("k", "v"):
            hd = w // H_B
            for hh in range(H_B):
                refs[name][:, hh, :] = y[:, hh * hd:(hh + 1) * hd]
            if name == "k" and "k16" in refs:
                refs["k16"][...] = y.astype(BF16)
            if name == "v" and "vt" in refs:
                for hh in range(H_B):
                    refs["vt"][hh, 0] = y[:, hh * hd:(hh + 1) * hd].T.astype(BF16)
        else:
            refs[name][...] = y.astype(refs[name].dtype)
        off += w


def _proj(x, g, w_r, groups, *, tm, attn_operands):
    t, d = x.shape
    row = lambda w: pl.BlockSpec((tm, w), lambda i: (i, 0))
    names, specs, shapes = [], [], []
    for n, w, dt in groups:
        names.append(n)
        if n in ("k", "v"):
            specs.append(pl.BlockSpec((tm, H_B, w // H_B), lambda i: (i, 0, 0)))
            shapes.append(jax.ShapeDtypeStruct((t, H_B, w // H_B), dt))
        else:
            specs.append(row(w))
            shapes.append(jax.ShapeDtypeStruct((t, w), dt))
    if attn_operands:
        wk, wv = groups[3][1], groups[4][1]
        names += ["k16", "vt"]
        specs += [row(wk), pl.BlockSpec((H_B, 1, wv // H_B, tm), lambda i: (0, i, 0, 0))]
        shapes += [jax.ShapeDtypeStruct((t, wk), BF16), jax.ShapeDtypeStruct((H_B, t // tm, wv // H_B, tm), BF16)]
    res = pl.pallas_call(
        functools.partial(_proj_body, groups=groups, names=names),
        grid=(t // tm,),
        in_specs=[row(d), _const_spec((1, d)), _const_spec(w_r.shape)],
        out_specs=specs,
        out_shape=shapes,
        compiler_params=_params(("parallel",)),
        name="proj",
    )(x, g, w_r)
    return dict(zip(names, res))


def _split2(x):
    hi = x.astype(BF16)
    return hi, (x - hi.astype(F32)).astype(BF16)


def _split3(x):
    hi = x.astype(BF16)
    r = x - hi.astype(F32)
    mid = r.astype(BF16)
    return hi, mid, (r - mid.astype(F32)).astype(BF16)


def _mm3(x, y):
    x_hi, x_lo = _split2(x)
    y_hi, y_lo = _split2(y)
    return _dot(x_hi, y_hi) + _dot(x_lo, y_hi) + _dot(x_hi, y_lo)


def _neumann_level(p, t_inv, c):
    if c % 16 == 0:
        x_hi, x_lo = _split2(jnp.concatenate([p, t_inv], axis=0))
        p_hi, p_lo = x_hi[:c], x_lo[:c]
        y = _dot(jnp.concatenate([x_hi, x_lo], axis=0), p_hi)
        y = y[:2 * c] + y[2 * c:] + _dot(x_hi, p_lo)
        return y[:c], t_inv + y[c:]
    p_hi, p_lo = _split2(p)
    t_hi, t_lo = _split2(t_inv)
    pp = _dot(p_hi, p_hi) + _dot(p_lo, p_hi) + _dot(p_hi, p_lo)
    tp = _dot(t_hi, p_hi) + _dot(t_lo, p_hi) + _dot(t_hi, p_lo)
    return pp, t_inv + tp


def _gdn_body(qkv_ref, cst_ref, z_ref, ba_ref, bat_ref, s0_ref, cw_ref, prow_ref, pcol_ref, gn_ref,
              ya_ref, s_ref, ext_ref, *, chunk, n_chunk, n_seq, l_valid):
    c = chunk
    tb = n_chunk * c
    j = pl.program_id(1)

    @pl.when(j == 0)
    def _():
        ext_ref[:, 0:SUBLANES, :] = cst_ref[...]
        s_ref[...] = s0_ref[...]

    ii = lax.broadcasted_iota(jnp.int32, (c, c), 0)
    jj = lax.broadcasted_iota(jnp.int32, (c, c), 1)
    causal = ii >= jj
    strict = ii > jj
    ltri16 = causal.astype(BF16)
    eye = (ii == jj).astype(F32)
    cw = cw_ref[...]
    hk = H_A * DK_A

    rows = lambda n: slice(n * c, (n + 1) * c)
    act, beta_c, g_c = [], [], []
    for s in range(n_seq):
        u_raw = qkv_ref[s]
        ext_ref[s, SUBLANES:SUBLANES + tb, :] = u_raw
        conv = u_raw * cw[CONV_W - 1:CONV_W, :]
        for k in range(1, CONV_W):
            conv = conv + ext_ref[s, SUBLANES - k:SUBLANES - k + tb, :] * cw[CONV_W - 1 - k:CONV_W - k, :]
        ext_ref[s, 0:SUBLANES, :] = ext_ref[s, tb:tb + SUBLANES, :]
        act.append(_silu(conv))
        ba = ba_ref[s]
        beta_s = jax.nn.sigmoid(ba)
        g_s = -jnp.exp(prow_ref[0:1, :]) * jax.nn.softplus(ba + prow_ref[1:2, :])
        if l_valid < c:
            valid = lax.broadcasted_iota(jnp.int32, g_s.shape, 0) < l_valid
            g_s = jnp.where(valid, g_s, 0.0)
            beta_s = jnp.where(valid, beta_s, 0.0)
        beta_c.append(beta_s)
        g_c.append(g_s)

    blocks = [(s, n) for s in range(n_seq) for n in range(n_chunk)]
    chains = [(s, n, h) for s, n in blocks for h in range(H_A)]

    gcum_c, gcum_r = {}, {}
    for s, n in blocks:
        g_r = -jnp.exp(pcol_ref[0][:, :c]) * jax.nn.softplus(bat_ref[s, n] + pcol_ref[1][:, :c])
        if l_valid < c:
            g_r = jnp.where(lax.broadcasted_iota(jnp.int32, g_r.shape, 1) < l_valid, g_r, 0.0)
        gcum_c[s, n] = functools.reduce(jnp.add, [_dot(ltri16, x) for x in _split3(g_c[s][rows(n)])])
        gcum_r[s, n] = functools.reduce(jnp.add, [_dot_nt(x, ltri16) for x in _split3(g_r)])

    st = {}
    for ch in chains:
        s, n, h = ch
        rs = rows(n)
        q_raw = act[s][rs, h * DK_A:(h + 1) * DK_A]
        k_raw = act[s][rs, hk + h * DK_A:hk + (h + 1) * DK_A]
        v = act[s][rs, 2 * hk + h * DV_A:2 * hk + (h + 1) * DV_A]
        q = q_raw * lax.rsqrt(jnp.sum(q_raw * q_raw, axis=-1, keepdims=True) + EPS) * (DK_A ** -0.5)
        k = k_raw * lax.rsqrt(jnp.sum(k_raw * k_raw, axis=-1, keepdims=True) + EPS)
        beta = beta_c[s][rs, h:h + 1]
        gc = gcum_c[s, n][:, H_A + h:H_A + h + 1]
        gr = gcum_r[s, n][H_A + h:H_A + h + 1, :]
        decay = jnp.where(causal, jnp.exp(jnp.where(causal, gc - gr, 0.0)), 0.0)
        kb = k * beta
        kq = _dot_nt(jnp.concatenate([kb, q], axis=0).astype(BF16), k.astype(BF16))
        eg = jnp.exp(gc)
        g_last = gc[c - 1:c, :]
        p = -jnp.where(strict, kq[:c] * decay, 0.0)
        st[ch] = dict(p=p, t=eye + p, qk=(kq[c:] * decay).astype(BF16),
                      rhs=jnp.concatenate([v * beta, kb * eg], axis=1).astype(BF16), qe=q * eg,
                      k_dec=(k * jnp.exp(g_last - gc)).astype(BF16), dec=jnp.exp(g_last))

    n_factor = c.bit_length() - 1
    if n_factor >= 2:
        for e in st.values():
            e["p"] = _mm3(e["p"], e["p"])
        for _ in range(n_factor - 2):
            for e in st.values():
                e["p"], e["t"] = _neumann_level(e["p"], e["t"], c)
        for e in st.values():
            e["t"] = e["t"] + _mm3(e["t"], e["p"])
    for e in st.values():
        uw = _dot(e["t"].astype(BF16), e["rhs"])
        e["u"] = uw[:, :DV_A]
        e["wq"] = jnp.concatenate([uw[:, DV_A:], e["qe"]], axis=0).astype(BF16)

    state = {(s, h): s_ref[s, h] for s in range(n_seq) for h in range(H_A)}
    for n in range(n_chunk):
        grp = [(s, h) for s in range(n_seq) for h in range(H_A)]
        ws = {g: _dot(st[g[0], n, g[1]]["wq"], state[g].astype(BF16)) for g in grp}
        v_new = {g: (st[g[0], n, g[1]]["u"] - ws[g][:c]).astype(BF16) for g in grp}
        for g in grp:
            s, h = g
            e = st[s, n, h]
            o = ws[g][c:] + _dot(e["qk"], v_new[g])
            state[g] = state[g] * e["dec"] + lax.dot_general(
                e["k_dec"], v_new[g], (((0,), (0,)), ((), ())), preferred_element_type=F32)
            zh = z_ref[s, rows(n), h * DV_A:(h + 1) * DV_A]
            ya_ref[s, rows(n), h * DV_A:(h + 1) * DV_A] = (_rms(o, gn_ref[...]) * _silu(zh)).astype(ya_ref.dtype)
    for (s, h), val in state.items():
        s_ref[s, h] = val


def _gdn(qkv, conv_state, z, ba, s0, conv_w, a_log, dt_bias, g_norm, *, chunk, n_chunk, n_seq, l_valid):
    b, l, cc = qkv.shape
    n = l // chunk
    tb = n_chunk * chunk
    assert l_valid == chunk or n == 1
    assert l % tb == 0 and b % n_seq == 0
    assert chunk & (chunk - 1) == 0
    cst = jnp.pad(conv_state, ((0, 0), (SUBLANES - (CONV_W - 1), 0), (0, 0)))
    bat = ba[..., :2 * H_A].reshape(b, n, chunk, 2 * H_A).transpose(0, 1, 3, 2)
    zeros_h = jnp.zeros((H_A,), F32)
    lane_pad = jnp.zeros((LANES - 2 * H_A,), F32)
    prow = jnp.stack([jnp.concatenate([zeros_h, a_log, lane_pad]),
                      jnp.concatenate([zeros_h, dt_bias, lane_pad])])
    pcol = jnp.broadcast_to(prow[:, :2 * H_A, None], (2, 2 * H_A, LANES))
    return pl.pallas_call(
        functools.partial(_gdn_body, chunk=chunk, n_chunk=n_chunk, n_seq=n_seq, l_valid=l_valid),
        grid=(b // n_seq, l // tb),
        in_specs=[pl.BlockSpec((n_seq, tb, cc), lambda i, j: (i, j, 0)),
                  pl.BlockSpec((n_seq, SUBLANES, cc), lambda i, j: (i, 0, 0)),
                  pl.BlockSpec((n_seq, tb, H_A * DV_A), lambda i, j: (i, j, 0)),
                  pl.BlockSpec((n_seq, tb, LANES), lambda i, j: (i, j, 0)),
                  pl.BlockSpec((n_seq, n_chunk, 2 * H_A, chunk), lambda i, j: (i, j, 0, 0)),
                  pl.BlockSpec((n_seq, H_A, DK_A, DV_A), lambda i, j: (i, 0, 0, 0)),
                  _const_spec((CONV_W, cc)), _const_spec((2, LANES)),
                  _const_spec((2, 2 * H_A, LANES)), _const_spec((1, DV_A))],
        out_specs=[pl.BlockSpec((n_seq, tb, H_A * DV_A), lambda i, j: (i, j, 0)),
                   pl.BlockSpec((n_seq, H_A, DK_A, DV_A), lambda i, j: (i, 0, 0, 0))],
        out_shape=[jax.ShapeDtypeStruct((b, l, H_A * DV_A), BF16),
                   jax.ShapeDtypeStruct((b, H_A, DK_A, DV_A), F32)],
        scratch_shapes=[pltpu.VMEM((n_seq, SUBLANES + tb, cc), F32)],
        compiler_params=_params(("parallel", "arbitrary")),
        name="gdn",
    )(qkv, cst, z, ba, bat, s0, conv_w, prow, pcol, g_norm.reshape(1, DV_A))


def _rel_bucket(n):
    n = jnp.maximum(n, 0)
    max_exact = NUM_BUCKETS // 2
    large = max_exact + (jnp.log(jnp.maximum(n, 1).astype(F32) / max_exact)
                         / math.log(MAX_DISTANCE / max_exact) * (NUM_BUCKETS - max_exact)).astype(jnp.int32)
    large = jnp.minimum(large, NUM_BUCKETS - 1)
    return jnp.where(n < max_exact, n, large)


def _bias_rows(rel, tab_rows):
    onehot = _rel_bucket(rel)[..., None] == jnp.arange(NUM_BUCKETS)
    b = jnp.sum(jnp.where(onehot, tab_rows[..., None, :], 0.0), axis=-1)
    return jnp.where(rel >= 0, b, NEG_INF)


def _lam_of(lp_ref, lam_init):
    lp = lp_ref[...]
    s1 = jnp.sum(lp[0:1] * lp[1:2], axis=-1, keepdims=True)
    s2 = jnp.sum(lp[2:3] * lp[3:4], axis=-1, keepdims=True)
    return jnp.exp(s1) - jnp.exp(s2) + lam_init


def _attn_prompt_body(q_ref, k_ref, vt_ref, tiles_ref, lp_ref, sub_ref, o_ref,
                        qt_ref, m_ref, l_ref, acc_ref, s_ref, *, t, col_block, lam_init):
    i = pl.program_id(2)
    n2 = 2 * t
    qt = q_ref[0].astype(F32).T
    dim = lax.broadcasted_iota(jnp.int32, qt.shape, 0)
    qt_ref[:, 0:t] = jnp.where(dim < DQK_B, qt, 0.0).astype(BF16)
    qt_ref[:, t:n2] = jnp.where(dim >= DQK_B, qt, 0.0).astype(BF16)
    m_ref[...] = jnp.full(m_ref.shape, NEG_INF, F32)
    l_ref[...] = jnp.zeros(l_ref.shape, F32)
    acc_ref[...] = jnp.zeros(acc_ref.shape, F32)
    groups = t // SUBLANES

    def run(blocks):
        kv = []
        for j, _ in blocks:
            start = pl.multiple_of(j * t, t)
            kv.append((k_ref[0, pl.ds(start, t), :], vt_ref[0, j]))
        items = [(b, c0) for b in range(len(blocks)) for c0 in range(0, n2, col_block)]

        def qk(n):
            b, c0 = items[n]
            s = _dot(kv[b][0], qt_ref[:, c0:c0 + col_block])
            if blocks[b][1] is not None:
                s = s + tiles_ref[0, blocks[b][1], :, c0 % t:c0 % t + col_block]
            s_ref[n % (ATTN_LOOKAHEAD + 1)] = s
            return jnp.max(s.reshape(groups, SUBLANES, col_block), axis=0)

        part_max = {n: qk(n) for n in range(min(ATTN_LOOKAHEAD, len(items)))}
        mt = min(ATTN_MXU_TILE, t, col_block)
        for idx, it in enumerate(items):
            b, c0 = it
            cols = slice(c0, c0 + col_block)
            if idx + ATTN_LOOKAHEAD < len(items):
                part_max[idx + ATTN_LOOKAHEAD] = qk(idx + ATTN_LOOKAHEAD)
            s = s_ref.at[idx % (ATTN_LOOKAHEAD + 1)]
            m_prev = m_ref[:, cols]
            m_new = jnp.maximum(m_prev, jnp.max(part_max.pop(idx), axis=0, keepdims=True))
            alpha = jnp.exp2(m_prev - m_new)
            l_new, pv = [], []
            for n0 in range(0, col_block, mt):
                l_n = alpha[:, n0:n0 + mt] * l_ref[:, c0 + n0:c0 + n0 + mt]
                pv_n = None
                for k0 in range(0, t, mt):
                    p3 = jnp.exp2(s[k0:k0 + mt, n0:n0 + mt].reshape(mt // SUBLANES, SUBLANES, mt)
                                  - m_new[None, :, n0:n0 + mt])
                    l_n = l_n + jnp.sum(p3, axis=0)
                    d = _dot(kv[b][1][:, k0:k0 + mt], p3.reshape(mt, mt).astype(BF16))
                    pv_n = d if pv_n is None else pv_n + d
                l_new.append(l_n)
                pv.append(pv_n)
            l_ref[:, cols] = jnp.concatenate(l_new, axis=1)
            acc = acc_ref[:, cols].reshape(DV_B // SUBLANES, SUBLANES, col_block) * alpha[None]
            acc_ref[:, cols] = acc.reshape(DV_B, col_block) + jnp.concatenate(pv, axis=1)
            m_ref[:, cols] = m_new

    n_far = jnp.maximum(i - 1, 0)
    per_trip = ATTN_BLOCKS_PER_TRIP

    def far_group(jj, carry):
        run([(per_trip * jj + u, None) for u in range(per_trip)])
        return carry

    lax.fori_loop(0, n_far // per_trip, far_group, 0)
    rem = n_far % per_trip
    for r in range(per_trip):
        @pl.when((i >= 1) & (rem == r))
        def _():
            run([(n_far - r + u, None) for u in range(r)] + [(i - 1, 1), (i, 0)])

    @pl.when(i == 0)
    def _():
        run([(i, 0)])

    lam = _lam_of(lp_ref, lam_init)
    acc = acc_ref[...] / jnp.sum(l_ref[...], axis=0, keepdims=True)
    o = (acc[:, 0:t] - lam * acc[:, t:n2]).T
    o_ref[0] = (_rms(o, sub_ref[...]) * (1.0 - lam_init)).astype(o_ref.dtype)


def _attn_prompt(q, k, vt, rel_table, lam_params, subln, *, t, lam_init):
    b, l, _ = q.shape
    nq = l // t
    assert t >= MAX_DISTANCE
    assert vt.shape == (H_B, b * nq, DV_B, t)
    table = rel_table.astype(F32)
    dist = jnp.arange(-t + 1, 2 * t)[None, None, :]
    f = _bias_rows(jnp.broadcast_to(dist, (H_B, 1, 3 * t - 1)), table.T[:, None, :])[:, 0]
    f = (f - table[NUM_BUCKETS - 1][:, None]) * LOG2E

    def toeplitz(w):
        wp = jnp.pad(w, ((0, 0), (0, 1)))
        g = jnp.tile(wp, (1, t))[:, :t * (2 * t - 1)].reshape(H_B, t, 2 * t - 1)
        return g[:, :, t - 1:]

    tiles = jnp.stack([toeplitz(f[:, :2 * t - 1]), toeplitz(f[:, t:])], axis=1)
    hw = 2 * DQK_B
    return pl.pallas_call(
        functools.partial(_attn_prompt_body, t=t, col_block=min(ATTN_COL_BLOCK, t), lam_init=lam_init),
        grid=(b, H_B, nq),
        in_specs=[pl.BlockSpec((1, t, hw), lambda bi, h, i: (bi, i, h)),
                  pl.BlockSpec((1, l, hw), lambda bi, h, i: (bi, 0, h)),
                  pl.BlockSpec((1, nq, DV_B, t), lambda bi, h, i: (h, bi, 0, 0)),
                  pl.BlockSpec((1, 2, t, t), lambda bi, h, i: (h, 0, 0, 0)),
                  _const_spec(lam_params.shape), _const_spec((1, DV_B))],
        out_specs=pl.BlockSpec((1, t, DV_B), lambda bi, h, i: (bi, i, h)),
        out_shape=jax.ShapeDtypeStruct((b, l, H_B * DV_B), BF16),
        scratch_shapes=[pltpu.VMEM((hw, 2 * t), BF16), pltpu.VMEM((SUBLANES, 2 * t), F32),
                        pltpu.VMEM((SUBLANES, 2 * t), F32), pltpu.VMEM((DV_B, 2 * t), F32),
                        pltpu.VMEM((ATTN_LOOKAHEAD + 1, t, min(ATTN_COL_BLOCK, t)), F32)],
        compiler_params=_params(("parallel", "parallel", "arbitrary")),
        name="attn_prompt",
    )(q, k, vt, tiles, lam_params, subln.reshape(1, DV_B))


def _attn_sample_body(pt_ref, q_ref, kn_ref, vn_ref, bpast_ref, bself_ref, lp_ref, sub_ref, *rest,
                      pages, lam_init):
    k_refs = rest[:pages]
    v_refs = rest[pages:2 * pages]
    o_ref, qf_ref, m_ref, l_ref, acc_ref = rest[2 * pages:]
    j = pl.program_id(1)
    rows = qf_ref.shape[0]
    half = rows // 2
    page_cols = PAGE_SIZE * H_B

    @pl.when(j == 0)
    def _():
        q = q_ref[0].astype(F32)
        r = lax.broadcasted_iota(jnp.int32, q.shape, 0)
        lane = lax.broadcasted_iota(jnp.int32, q.shape, 1)
        qf = jnp.where(lane // DQK_B == r // half, q, 0.0)
        qf_ref[...] = qf
        kn = kn_ref[0]
        vn = vn_ref[0]
        n_self = kn.shape[0]
        s_self = [jnp.sum(qf * kn[c:c + 1], axis=-1, keepdims=True) + bself_ref[:, c:c + 1]
                  for c in range(n_self)]
        m0 = functools.reduce(jnp.maximum, s_self)
        p_self = [jnp.exp2(s - m0) for s in s_self]
        m_ref[...] = m0
        l_ref[...] = functools.reduce(jnp.add, p_self)
        acc_ref[...] = functools.reduce(jnp.add, [p * vn[c:c + 1] for c, p in enumerate(p_self)])

    q16 = qf_ref[...].astype(BF16)
    group = math.gcd(pages, SAMPLE_PAGE_GROUP)
    groups = [range(g, g + group) for g in range(0, pages, group)]
    qk = lambda grp: [_dot_nt(q16, k_refs[c][0].astype(BF16)) for c in grp]
    scores = qk(groups[0])
    m_run, l_run, acc = m_ref[...], l_ref[...], acc_ref[...]
    for gi, grp in enumerate(groups):
        cur = scores
        if gi + 1 < len(groups):
            scores = qk(groups[gi + 1])
        s = (jnp.concatenate(cur, axis=-1)
             + bpast_ref[j, :, grp[0] * page_cols:(grp[-1] + 1) * page_cols])
        m_new = jnp.maximum(m_run, jnp.max(s, axis=-1, keepdims=True))
        alpha = jnp.exp2(m_run - m_new)
        p = jnp.exp2(s - m_new)
        l_run = alpha * l_run + jnp.sum(p, axis=-1, keepdims=True)
        p16 = p.astype(BF16)
        pv = functools.reduce(jnp.add, [_dot(p16[:, i * page_cols:(i + 1) * page_cols], v_refs[c][0].astype(BF16))
                                        for i, c in enumerate(grp)])
        acc = alpha * acc + pv
        m_run = m_new
    m_ref[...], l_ref[...], acc_ref[...] = m_run, l_run, acc

    @pl.when(j == pl.num_programs(1) - 1)
    def _():
        lam = _lam_of(lp_ref, lam_init)
        an = acc_ref[...] / l_ref[...]
        o = an[0:half] - lam * an[half:rows]
        o_ref[0] = _rms(o, sub_ref[...]) * (1.0 - lam_init)


def _attn_sample(q, k_new, v_new, cache_k, cache_v, page_base, page_table, rel_table, lam_params, subln,
                 *, pages, lam_init):
    bd, l_new, width = q.shape
    n_pages = page_table.shape[1]
    past = n_pages * PAGE_SIZE
    half = l_new * H_B
    rows = 2 * half
    page_cols = PAGE_SIZE * H_B
    table = rel_table.astype(F32)
    t_of = (jnp.arange(rows) % half) // H_B
    h_of = jnp.arange(rows) % H_B
    tab_rows = table.T[h_of]
    far = table[NUM_BUCKETS - 1][h_of][:, None]
    own = h_of[:, None, None] == jnp.arange(H_B)[None, None, :]
    rel_past = past + t_of[:, None] - jnp.arange(past)[None, :]
    bpast = jnp.where(own, ((_bias_rows(rel_past, tab_rows) - far) * LOG2E)[:, :, None], NEG_INF)
    n_steps = n_pages // pages
    bpast = bpast.reshape(rows, n_steps, pages * page_cols).transpose(1, 0, 2)
    n_self = l_new * H_B
    t_key = jnp.arange(LANES) // H_B
    rel_self = jnp.where((jnp.arange(LANES)[None, :] < n_self) & (h_of[:, None] == jnp.arange(LANES)[None, :] % H_B),
                         t_of[:, None] - t_key[None, :], -1)
    bself = (_bias_rows(rel_self, tab_rows) - far) * LOG2E

    hd = width // H_B
    q_rows = q.reshape(bd, half, hd)
    q_rows = jnp.concatenate([q_rows, q_rows], axis=1)
    kn = k_new.reshape(bd, n_self, hd)
    vn = v_new.reshape(bd, n_self, hd)

    def page_spec(c):
        return pl.BlockSpec((1, page_cols, hd),
                            lambda b, j, pt: (page_base + pt[b * n_pages + j * pages + c], 0, 0))

    seq_spec = lambda r: pl.BlockSpec((1, r, hd), lambda b, j, pt: (b, 0, 0))
    whole = lambda shape: pl.BlockSpec(shape, lambda b, j, pt: (0,) * len(shape), pipeline_mode=pl.Buffered(1))
    grid_spec = pltpu.PrefetchScalarGridSpec(
        num_scalar_prefetch=1,
        grid=(bd, n_pages // pages),
        in_specs=[seq_spec(rows), seq_spec(n_self), seq_spec(n_self),
                  whole(bpast.shape), whole(bself.shape), whole(lam_params.shape), whole((1, DV_B))]
                 + [page_spec(c) for c in range(pages)] + [page_spec(c) for c in range(pages)],
        out_specs=seq_spec(half),
        scratch_shapes=[pltpu.VMEM((rows, hd), F32), pltpu.VMEM((rows, 1), F32),
                        pltpu.VMEM((rows, 1), F32), pltpu.VMEM((rows, DV_B), F32)])
    out = pl.pallas_call(
        functools.partial(_attn_sample_body, pages=pages, lam_init=lam_init),
        grid_spec=grid_spec,
        out_shape=jax.ShapeDtypeStruct((bd, half, DV_B), F32),
        compiler_params=_params(("parallel", "arbitrary")),
        name="attn_sample",
    )(page_table.reshape(-1), q_rows, kn, vn, bpast, bself, lam_params, subln.reshape(1, DV_B),
      *([cache_k] * pages), *([cache_v] * pages))
    return out.reshape(bd, l_new, width)


def _merge_ffn_body(x_ref, ya_ref, yb_ref, ga_ref, gb_ref, wa_ref, wb_ref, wo_ref, post_ref,
                    pre2_ref, wg_ref, wu_ref, wd_ref, post2_ref, o_ref, *, f_chunk):
    merged = (jax.nn.sigmoid(ga_ref[...]) * _dot(ya_ref[...], wa_ref[...])
              + jax.nn.sigmoid(gb_ref[...]) * _dot(yb_ref[...], wb_ref[...]))
    x = x_ref[...] + _rms(_dot(merged.astype(BF16), wo_ref[...]), post_ref[...])
    o_ref[...] = _ffn_half_step(x, pre2_ref, wg_ref, wu_ref, wd_ref, post2_ref, f_chunk)


def _merge_ffn(x, ya, yb, ga, gb, wa, wb, wo, post, pre2, wg, wu, wd, post2, *, tm):
    t, d = x.shape
    d_ff = wg.shape[1]
    f_chunk = FFN_CHUNK if d_ff % FFN_CHUNK == 0 else d_ff
    row = lambda w: pl.BlockSpec((tm, w), lambda i: (i, 0))
    weights = (wa, wb, wo, post, pre2, wg, wu, wd, post2)
    return pl.pallas_call(
        functools.partial(_merge_ffn_body, f_chunk=f_chunk),
        grid=(t // tm,),
        in_specs=[row(d), row(ya.shape[1]), row(yb.shape[1]), row(d), row(d)]
                 + [_const_spec(w.shape) for w in weights],
        out_specs=row(d),
        out_shape=jax.ShapeDtypeStruct((t, d), F32),
        compiler_params=_params(("parallel",)),
        name="merge_ffn",
    )(x, ya, yb, ga, gb, *weights)


def _lambda_init(layer):
    return 0.8 - 0.6 * math.exp(-0.3 * layer)


def _token_tile(t):
    tm = TOKEN_TILE
    while t % tm:
        tm //= 2
    return tm


def _attn_tile(seq):
    return min(ATTN_BLOCK, seq)


def _layer_weights(l, W):
    d = W["w_in"].shape[1]
    bf = lambda a: a.astype(BF16)
    row = lambda a: a.reshape(1, -1).astype(F32)
    widths = {"qkv": H_A * (2 * DK_A + DV_A), "z": H_A * DV_A, "beta": H_A, "alpha": H_A,
              "q": H_B * 2 * DQK_B, "k": H_B * 2 * DQK_B, "v": H_B * DV_B, "ga": d, "gb": d}
    order = ("qkv", "z", "beta", "alpha", "q", "k", "v", "ga", "gb")
    offs, off = {}, 0
    for n in order:
        offs[n] = off
        off += widths[n]
    w_in = W["w_in"][l]
    col = lambda n: w_in[:, offs[n]:offs[n] + widths[n]]
    pad = jnp.zeros((d, LANES - 2 * H_A), w_in.dtype)
    w_r = jnp.concatenate([col("qkv"), col("z"), col("q"), col("k"), col("v"), col("ga"), col("gb"),
                           col("beta"), col("alpha"), pad], axis=1)
    groups = tuple((n, (d if w is None else w), dt) for n, w, dt in _PROJ_OUT)
    lam_params = jnp.stack([W["lam_q1"][l], W["lam_k1"][l], W["lam_q2"][l], W["lam_k2"][l]]).astype(F32)
    return dict(
        ffn1=(row(W["ffn1_pre"][l]), bf(W["ffn1_wg"][l]), bf(W["ffn1_wu"][l]), bf(W["ffn1_wd"][l]),
              row(W["ffn1_post"][l])),
        ffn2=(row(W["ffn2_pre"][l]), bf(W["ffn2_wg"][l]), bf(W["ffn2_wu"][l]), bf(W["ffn2_wd"][l]),
              row(W["ffn2_post"][l])),
        mix_pre=row(W["mix_pre"][l]), w_r=bf(w_r), groups=groups,
        conv_w=W["conv_w"][l].astype(F32), a_log=W["a_log"][l].astype(F32), dt_bias=W["dt_bias"][l].astype(F32),
        gdn_norm=W["gdn_norm"][l].astype(F32), lam_params=lam_params, subln=W["subln"][l].astype(F32),
        w_a=bf(W["w_a"][l]), w_b=bf(W["w_b"][l]), w_out=bf(W["w_out"][l]), mix_post=row(W["mix_post"][l]),
        lam_init=_lambda_init(l))


def _decoder_layer(x, lw, conv_state, ssm_state, attend, attn_tile=None):
    b, l, d = x.shape
    t = b * l
    tm = attn_tile or _token_tile(t)
    x2 = _ffn(x.reshape(t, d), *lw["ffn1"], tm=tm)
    c = _proj(x2, lw["mix_pre"], lw["w_r"], lw["groups"], tm=tm, attn_operands=attn_tile is not None)
    c3 = {n: c[n].reshape(b, l, c[n].shape[-1]) for n in ("qkv", "z", "ba")}

    chunk = min(CHUNK, l)
    if chunk % SUBLANES:
        chunk = -(-chunk // SUBLANES) * SUBLANES
    lp = -(-l // chunk) * chunk
    padl = lambda a: jnp.pad(a, ((0, 0), (0, lp - l), (0, 0))) if lp != l else a
    ya, ssm_new = _gdn(padl(c3["qkv"]), conv_state, padl(c3["z"]), padl(c3["ba"]), ssm_state,
                       lw["conv_w"], lw["a_log"], lw["dt_bias"], lw["gdn_norm"],
                       chunk=chunk, n_chunk=math.gcd(lp // chunk, GDN_CHUNKS_PER_STEP),
                       n_seq=math.gcd(b, GDN_SEQS_PER_STEP) if lp == chunk else 1,
                       l_valid=min(l - (lp - chunk), chunk))
    ya = ya[:, :l]
    if l >= CONV_W - 1:
        conv_new = c3["qkv"][:, l - (CONV_W - 1):]
    else:
        conv_new = jnp.concatenate([conv_state.astype(F32), c3["qkv"]], axis=1)[:, -(CONV_W - 1):]

    yb = attend(c, b, l)
    y = _merge_ffn(x2, ya.reshape(t, -1), yb.reshape(t, -1).astype(BF16), c["ga"], c["gb"],
                   lw["w_a"], lw["w_b"], lw["w_out"], lw["mix_post"], *lw["ffn2"], tm=tm)
    return (y.reshape(b, l, d), c["k"].reshape(b, l, H_B, 2 * DQK_B), c["v"].reshape(b, l, H_B, DV_B),
            conv_new, ssm_new)


def kernel(x_prompt, x_sample, cache_k, cache_v, state_conv, state_ssm, page_table, rel_table,
           ffn1_pre, ffn1_wg, ffn1_wu, ffn1_wd, ffn1_post, mix_pre, w_in, conv_w, a_log, dt_bias,
           gdn_norm, lam_q1, lam_k1, lam_q2, lam_k2, subln, w_a, w_b, w_out, mix_post,
           ffn2_pre, ffn2_wg, ffn2_wu, ffn2_wd, ffn2_post):
    W = dict(ffn1_pre=ffn1_pre, ffn1_wg=ffn1_wg, ffn1_wu=ffn1_wu, ffn1_wd=ffn1_wd, ffn1_post=ffn1_post,
             mix_pre=mix_pre, w_in=w_in, conv_w=conv_w, a_log=a_log, dt_bias=dt_bias, gdn_norm=gdn_norm,
             lam_q1=lam_q1, lam_k1=lam_k1, lam_q2=lam_q2, lam_k2=lam_k2, subln=subln, w_a=w_a, w_b=w_b,
             w_out=w_out, mix_post=mix_post, ffn2_pre=ffn2_pre, ffn2_wg=ffn2_wg, ffn2_wu=ffn2_wu,
             ffn2_wd=ffn2_wd, ffn2_post=ffn2_post)
    depth = w_in.shape[0]
    bp, seq, _ = x_prompt.shape
    n_pool = cache_k.shape[1]
    n_pages = page_table.shape[1]
    t_attn = _attn_tile(seq)
    pages = math.gcd(n_pages, SAMPLE_PAGES_PER_STEP)
    ck = cache_k.reshape(depth * n_pool, PAGE_SIZE * H_B, -1)
    cv = cache_v.reshape(depth * n_pool, PAGE_SIZE * H_B, -1)
    xp, xs = x_prompt, x_sample
    outs = [[] for _ in range(8)]
    for l in range(depth):
        lw = _layer_weights(l, W)

        def attend_prompt(c, b, n):
            return _attn_prompt(c["q"].reshape(b, n, -1), c["k16"].reshape(b, n, -1), c["vt"], rel_table,
                                lw["lam_params"], lw["subln"], t=t_attn, lam_init=lw["lam_init"])

        def attend_sample(c, b, n):
            return _attn_sample(c["q"].reshape(b, n, -1), c["k"], c["v"], ck, cv, l * n_pool, page_table,
                                rel_table, lw["lam_params"], lw["subln"], pages=pages, lam_init=lw["lam_init"])

        zero_conv = jnp.zeros((bp, CONV_W - 1, state_conv.shape[-1]), x_prompt.dtype)
        zero_ssm = jnp.zeros((bp,) + state_ssm.shape[2:], state_ssm.dtype)
        xp, k1, v1, c1, s1 = _decoder_layer(xp, lw, zero_conv, zero_ssm, attend_prompt, attn_tile=t_attn)
        xs, k2, v2, c2, s2 = _decoder_layer(xs, lw, state_conv[l], state_ssm[l], attend_sample)
        for o, a in zip(outs, (k1, v1, c1, s1, k2, v2, c2, s2)):
            o.append(a)
    return (xp, xs) + tuple(jnp.stack(o) for o in outs)
```

```python
import functools
import math

import jax
import jax.numpy as jnp
from jax import lax
from jax.experimental import pallas as pl
from jax.experimental.pallas import tpu as pltpu

F32 = jnp.float32
BF16 = jnp.bfloat16

H_A = 4
DK_A = 128
DV_A = 128
CONV_W = 4
CHUNK = 64
H_B = 4
DQK_B = 64
DV_B = 2 * DQK_B
PAGE_SIZE = 128
NUM_BUCKETS = 32
MAX_DISTANCE = 128
EPS = 1e-6
NEG_INF = -1e30

LANES = 128
SUBLANES = 8
VMEM_LIMIT = 56 * 1024 * 1024

LOG2E = math.log2(math.e)
QK_LOGIT_SCALE = DQK_B ** -0.5 * LOG2E

TOKEN_TILE = 512
FFN_CHUNK = 256
GDN_CHUNKS_PER_STEP = 4
GDN_SEQS_PER_STEP = 8
ATTN_BLOCK = 512
ATTN_COL_BLOCK = 512
ATTN_MXU_TILE = 256
ATTN_LOOKAHEAD = 2
ATTN_BLOCKS_PER_TRIP = 8
SAMPLE_PAGES_PER_STEP = 32
SAMPLE_PAGE_GROUP = 4


def _dot(a, b):
    return jnp.dot(a, b, preferred_element_type=F32)


def _dot_nt(a, b):
    return lax.dot_general(a, b, (((1,), (1,)), ((), ())), preferred_element_type=F32)


def _rms(x, g):
    return x * lax.rsqrt(jnp.mean(x * x, axis=-1, keepdims=True) + EPS) * g


def _silu(x):
    return x * jax.nn.sigmoid(x)


def _const_spec(shape):
    nd = len(shape)
    return pl.BlockSpec(shape, lambda *_: (0,) * nd, pipeline_mode=pl.Buffered(1))


def _params(sem):
    return pltpu.CompilerParams(dimension_semantics=sem, vmem_limit_bytes=VMEM_LIMIT)


def _ffn_half_step(x, pre_ref, wg_ref, wu_ref, wd_ref, post_ref, f_chunk):
    h = _rms(x, pre_ref[...]).astype(BF16)
    d_ff = wg_ref.shape[1]
    acc = jnp.zeros(x.shape, F32)
    for c in range(d_ff // f_chunk):
        sl = slice(c * f_chunk, (c + 1) * f_chunk)
        g = _dot(h, wg_ref[:, sl])
        u = _dot(h, wu_ref[:, sl])
        acc = acc + _dot((_silu(g) * u).astype(BF16), wd_ref[sl, :])
    return x + 0.5 * _rms(acc, post_ref[...])


def _ffn_body(x_ref, pre_ref, wg_ref, wu_ref, wd_ref, post_ref, o_ref, *, f_chunk):
    o_ref[...] = _ffn_half_step(x_ref[...], pre_ref, wg_ref, wu_ref, wd_ref, post_ref, f_chunk)


def _ffn(x, pre, wg, wu, wd, post, *, tm):
    t, d = x.shape
    d_ff = wg.shape[1]
    f_chunk = FFN_CHUNK if d_ff % FFN_CHUNK == 0 else d_ff
    return pl.pallas_call(
        functools.partial(_ffn_body, f_chunk=f_chunk),
        grid=(t // tm,),
        in_specs=[pl.BlockSpec((tm, d), lambda i: (i, 0)),
                  _const_spec((1, d)), _const_spec((d, d_ff)), _const_spec((d, d_ff)),
                  _const_spec((d_ff, d)), _const_spec((1, d))],
        out_specs=pl.BlockSpec((tm, d), lambda i: (i, 0)),
        out_shape=jax.ShapeDtypeStruct((t, d), F32),
        compiler_params=_params(("parallel",)),
        name="ffn",
    )(x, pre, wg, wu, wd, post)


_PROJ_OUT = (("qkv", H_A * (2 * DK_A + DV_A), F32), ("z", H_A * DV_A, F32),
             ("q", H_B * 2 * DQK_B, BF16), ("k", H_B * 2 * DQK_B, F32), ("v", H_B * DV_B, F32),
             ("ga", None, F32), ("gb", None, F32), ("ba", LANES, F32))


def _proj_body(x_ref, g_ref, w_ref, *o_refs, groups, names):
    h = _rms(x_ref[...], g_ref[...]).astype(BF16)
    refs = dict(zip(names, o_refs))
    off = 0
    for name, w, _ in groups:
        y = _dot(h, w_ref[:, off:off + w])
        if name == "q":
            y = y * QK_LOGIT_SCALE
        if name in ("k", "v"):
            hd = w // H_B
            for hh in range(H_B):
                refs[name][:, hh, :] = y[:, hh * hd:(hh + 1) * hd]
            if name == "k" and "k16" in refs:
                refs["k16"][...] = y.astype(BF16)
            if name == "v" and "vt" in refs:
                for hh in range(H_B):
                    refs["vt"][hh, 0] = y[:, hh * hd:(hh + 1) * hd].T.astype(BF16)
        else:
            refs[name][...] = y.astype(refs[name].dtype)
        off += w


def _proj(x, g, w_r, groups, *, tm, attn_operands):
    t, d = x.shape
    row = lambda w: pl.BlockSpec((tm, w), lambda i: (i, 0))
    names, specs, shapes = [], [], []
    for n, w, dt in groups:
        names.append(n)
        if n in ("k", "v"):
            specs.append(pl.BlockSpec((tm, H_B, w // H_B), lambda i: (i, 0, 0)))
            shapes.append(jax.ShapeDtypeStruct((t, H_B, w // H_B), dt))
        else:
            specs.append(row(w))
            shapes.append(jax.ShapeDtypeStruct((t, w), dt))
    if attn_operands:
        wk, wv = groups[3][1], groups[4][1]
        names += ["k16", "vt"]
        specs += [row(wk), pl.BlockSpec((H_B, 1, wv // H_B, tm), lambda i: (0, i, 0, 0))]
        shapes += [jax.ShapeDtypeStruct((t, wk), BF16), jax.ShapeDtypeStruct((H_B, t // tm, wv // H_B, tm), BF16)]
    res = pl.pallas_call(
        functools.partial(_proj_body, groups=groups, names=names),
        grid=(t // tm,),
        in_specs=[row(d), _const_spec((1, d)), _const_spec(w_r.shape)],
        out_specs=specs,
        out_shape=shapes,
        compiler_params=_params(("parallel",)),
        name="proj",
    )(x, g, w_r)
    return dict(zip(names, res))


def _split2(x):
    hi = x.astype(BF16)
    return hi, (x - hi.astype(F32)).astype(BF16)


def _split3(x):
    hi = x.astype(BF16)
    r = x - hi.astype(F32)
    mid = r.astype(BF16)
    return hi, mid, (r - mid.astype(F32)).astype(BF16)


def _mm3(x, y):
    x_hi, x_lo = _split2(x)
    y_hi, y_lo = _split2(y)
    return _dot(x_hi, y_hi) + _dot(x_lo, y_hi) + _dot(x_hi, y_lo)


def _neumann_level(p, t_inv, c):
    if c % 16 == 0:
        x_hi, x_lo = _split2(jnp.concatenate([p, t_inv], axis=0))
        p_hi, p_lo = x_hi[:c], x_lo[:c]
        y = _dot(jnp.concatenate([x_hi, x_lo], axis=0), p_hi)
        y = y[:2 * c] + y[2 * c:] + _dot(x_hi, p_lo)
        return y[:c], t_inv + y[c:]
    p_hi, p_lo = _split2(p)
    t_hi, t_lo = _split2(t_inv)
    pp = _dot(p_hi, p_hi) + _dot(p_lo, p_hi) + _dot(p_hi, p_lo)
    tp = _dot(t_hi, p_hi) + _dot(t_lo, p_hi) + _dot(t_hi, p_lo)
    return pp, t_inv + tp


def _gdn_body(qkv_ref, cst_ref, z_ref, ba_ref, bat_ref, s0_ref, cw_ref, prow_ref, pcol_ref, gn_ref,
              ya_ref, s_ref, ext_ref, *, chunk, n_chunk, n_seq, l_valid):
    c = chunk
    tb = n_chunk * c
    j = pl.program_id(1)

    @pl.when(j == 0)
    def _():
        ext_ref[:, 0:SUBLANES, :] = cst_ref[...]
        s_ref[...] = s0_ref[...]

    ii = lax.broadcasted_iota(jnp.int32, (c, c), 0)
    jj = lax.broadcasted_iota(jnp.int32, (c, c), 1)
    causal = ii >= jj
    strict = ii > jj
    ltri16 = causal.astype(BF16)
    eye = (ii == jj).astype(F32)
    cw = cw_ref[...]
    hk = H_A * DK_A

    rows = lambda n: slice(n * c, (n + 1) * c)
    act, beta_c, g_c = [], [], []
    for s in range(n_seq):
        u_raw = qkv_ref[s]
        ext_ref[s, SUBLANES:SUBLANES + tb, :] = u_raw
        conv = u_raw * cw[CONV_W - 1:CONV_W, :]
        for k in range(1, CONV_W):
            conv = conv + ext_ref[s, SUBLANES - k:SUBLANES - k + tb, :] * cw[CONV_W - 1 - k:CONV_W - k, :]
        ext_ref[s, 0:SUBLANES, :] = ext_ref[s, tb:tb + SUBLANES, :]
        act.append(_silu(conv))
        ba = ba_ref[s]
        beta_s = jax.nn.sigmoid(ba)
        g_s = -jnp.exp(prow_ref[0:1, :]) * jax.nn.softplus(ba + prow_ref[1:2, :])
        if l_valid < c:
            valid = lax.broadcasted_iota(jnp.int32, g_s.shape, 0) < l_valid
            g_s = jnp.where(valid, g_s, 0.0)
            beta_s = jnp.where(valid, beta_s, 0.0)
        beta_c.append(beta_s)
        g_c.append(g_s)

    blocks = [(s, n) for s in range(n_seq) for n in range(n_chunk)]
    chains = [(s, n, h) for s, n in blocks for h in range(H_A)]

    gcum_c, gcum_r = {}, {}
    for s, n in blocks:
        g_r = -jnp.exp(pcol_ref[0][:, :c]) * jax.nn.softplus(bat_ref[s, n] + pcol_ref[1][:, :c])
        if l_valid < c:
            g_r = jnp.where(lax.broadcasted_iota(jnp.int32, g_r.shape, 1) < l_valid, g_r, 0.0)
        gcum_c[s, n] = functools.reduce(jnp.add, [_dot(ltri16, x) for x in _split3(g_c[s][rows(n)])])
        gcum_r[s, n] = functools.reduce(jnp.add, [_dot_nt(x, ltri16) for x in _split3(g_r)])

    st = {}
    for ch in chains:
        s, n, h = ch
        rs = rows(n)
        q_raw = act[s][rs, h * DK_A:(h + 1) * DK_A]
        k_raw = act[s][rs, hk + h * DK_A:hk + (h + 1) * DK_A]
        v = act[s][rs, 2 * hk + h * DV_A:2 * hk + (h + 1) * DV_A]
        q = q_raw * lax.rsqrt(jnp.sum(q_raw * q_raw, axis=-1, keepdims=True) + EPS) * (DK_A ** -0.5)
        k = k_raw * lax.rsqrt(jnp.sum(k_raw * k_raw, axis=-1, keepdims=True) + EPS)
        beta = beta_c[s][rs, h:h + 1]
        gc = gcum_c[s, n][:, H_A + h:H_A + h + 1]
        gr = gcum_r[s, n][H_A + h:H_A + h + 1, :]
        decay = jnp.where(causal, jnp.exp(jnp.where(causal, gc - gr, 0.0)), 0.0)
        kb = k * beta
        kq = _dot_nt(jnp.concatenate([kb, q], axis=0).astype(BF16), k.astype(BF16))
        eg = jnp.exp(gc)
        g_last = gc[c - 1:c, :]
        p = -jnp.where(strict, kq[:c] * decay, 0.0)
        st[ch] = dict(p=p, t=eye + p, qk=(kq[c:] * decay).astype(BF16),
                      rhs=jnp.concatenate([v * beta, kb * eg], axis=1).astype(BF16), qe=q * eg,
                      k_dec=(k * jnp.exp(g_last - gc)).astype(BF16), dec=jnp.exp(g_last))

    n_factor = c.bit_length() - 1
    if n_factor >= 2:
        for e in st.values():
            e["p"] = _mm3(e["p"], e["p"])
        for _ in range(n_factor - 2):
            for e in st.values():
                e["p"], e["t"] = _neumann_level(e["p"], e["t"], c)
        for e in st.values():
            e["t"] = e["t"] + _mm3(e["t"], e["p"])
    for e in st.values():
        uw = _dot(e["t"].astype(BF16), e["rhs"])
        e["u"] = uw[:, :DV_A]
        e["wq"] = jnp.concatenate([uw[:, DV_A:], e["qe"]], axis=0).astype(BF16)

    state = {(s, h): s_ref[s, h] for s in range(n_seq) for h in range(H_A)}
    for n in range(n_chunk):
        grp = [(s, h) for s in range(n_seq) for h in range(H_A)]
        ws = {g: _dot(st[g[0], n, g[1]]["wq"], state[g].astype(BF16)) for g in grp}
        v_new = {g: (st[g[0], n, g[1]]["u"] - ws[g][:c]).astype(BF16) for g in grp}
        for g in grp:
            s, h = g
            e = st[s, n, h]
            o = ws[g][c:] + _dot(e["qk"], v_new[g])
            state[g] = state[g] * e["dec"] + lax.dot_general(
                e["k_dec"], v_new[g], (((0,), (0,)), ((), ())), preferred_element_type=F32)
            zh = z_ref[s, rows(n), h * DV_A:(h + 1) * DV_A]
            ya_ref[s, rows(n), h * DV_A:(h + 1) * DV_A] = (_rms(o, gn_ref[...]) * _silu(zh)).astype(ya_ref.dtype)
    for (s, h), val in state.items():
        s_ref[s, h] = val


def _gdn(qkv, conv_state, z, ba, s0, conv_w, a_log, dt_bias, g_norm, *, chunk, n_chunk, n_seq, l_valid):
    b, l, cc = qkv.shape
    n = l // chunk
    tb = n_chunk * chunk
    assert l_valid == chunk or n == 1
    assert l % tb == 0 and b % n_seq == 0
    assert chunk & (chunk - 1) == 0
    cst = jnp.pad(conv_state, ((0, 0), (SUBLANES - (CONV_W - 1), 0), (0, 0)))
    bat = ba[..., :2 * H_A].reshape(b, n, chunk, 2 * H_A).transpose(0, 1, 3, 2)
    zeros_h = jnp.zeros((H_A,), F32)
    lane_pad = jnp.zeros((LANES - 2 * H_A,), F32)
    prow = jnp.stack([jnp.concatenate([zeros_h, a_log, lane_pad]),
                      jnp.concatenate([zeros_h, dt_bias, lane_pad])])
    pcol = jnp.broadcast_to(prow[:, :2 * H_A, None], (2, 2 * H_A, LANES))
    return pl.pallas_call(
        functools.partial(_gdn_body, chunk=chunk, n_chunk=n_chunk, n_seq=n_seq, l_valid=l_valid),
        grid=(b // n_seq, l // tb),
        in_specs=[pl.BlockSpec((n_seq, tb, cc), lambda i, j: (i, j, 0)),
                  pl.BlockSpec((n_seq, SUBLANES, cc), lambda i, j: (i, 0, 0)),
                  pl.BlockSpec((n_seq, tb, H_A * DV_A), lambda i, j: (i, j, 0)),
                  pl.BlockSpec((n_seq, tb, LANES), lambda i, j: (i, j, 0)),
                  pl.BlockSpec((n_seq, n_chunk, 2 * H_A, chunk), lambda i, j: (i, j, 0, 0)),
                  pl.BlockSpec((n_seq, H_A, DK_A, DV_A), lambda i, j: (i, 0, 0, 0)),
                  _const_spec((CONV_W, cc)), _const_spec((2, LANES)),
                  _const_spec((2, 2 * H_A, LANES)), _const_spec((1, DV_A))],
        out_specs=[pl.BlockSpec((n_seq, tb, H_A * DV_A), lambda i, j: (i, j, 0)),
                   pl.BlockSpec((n_seq, H_A, DK_A, DV_A), lambda i, j: (i, 0, 0, 0))],
        out_shape=[jax.ShapeDtypeStruct((b, l, H_A * DV_A), BF16),
                   jax.ShapeDtypeStruct((b, H_A, DK_A, DV_A), F32)],
        scratch_shapes=[pltpu.VMEM((n_seq, SUBLANES + tb, cc), F32)],
        compiler_params=_params(("parallel", "arbitrary")),
        name="gdn",
    )(qkv, cst, z, ba, bat, s0, conv_w, prow, pcol, g_norm.reshape(1, DV_A))


def _rel_bucket(n):
    n = jnp.maximum(n, 0)
    max_exact = NUM_BUCKETS // 2
    large = max_exact + (jnp.log(jnp.maximum(n, 1).astype(F32) / max_exact)
                         / math.log(MAX_DISTANCE / max_exact) * (NUM_BUCKETS - max_exact)).astype(jnp.int32)
    large = jnp.minimum(large, NUM_BUCKETS - 1)
    return jnp.where(n < max_exact, n, large)


def _bias_rows(rel, tab_rows):
    onehot = _rel_bucket(rel)[..., None] == jnp.arange(NUM_BUCKETS)
    b = jnp.sum(jnp.where(onehot, tab_rows[..., None, :], 0.0), axis=-1)
    return jnp.where(rel >= 0, b, NEG_INF)


def _lam_of(lp_ref, lam_init):
    lp = lp_ref[...]
    s1 = jnp.sum(lp[0:1] * lp[1:2], axis=-1, keepdims=True)
    s2 = jnp.sum(lp[2:3] * lp[3:4], axis=-1, keepdims=True)
    return jnp.exp(s1) - jnp.exp(s2) + lam_init


def _attn_prompt_body(q_ref, k_ref, vt_ref, tiles_ref, lp_ref, sub_ref, o_ref,
                        qt_ref, m_ref, l_ref, acc_ref, s_ref, *, t, col_block, lam_init):
    i = pl.program_id(2)
    n2 = 2 * t
    qt = q_ref[0].astype(F32).T
    dim = lax.broadcasted_iota(jnp.int32, qt.shape, 0)
    qt_ref[:, 0:t] = jnp.where(dim < DQK_B, qt, 0.0).astype(BF16)
    qt_ref[:, t:n2] = jnp.where(dim >= DQK_B, qt, 0.0).astype(BF16)
    m_ref[...] = jnp.full(m_ref.shape, NEG_INF, F32)
    l_ref[...] = jnp.zeros(l_ref.shape, F32)
    acc_ref[...] = jnp.zeros(acc_ref.shape, F32)
    groups = t // SUBLANES

    def run(blocks):
        kv = []
        for j, _ in blocks:
            start = pl.multiple_of(j * t, t)
            kv.append((k_ref[0, pl.ds(start, t), :], vt_ref[0, j]))
        items = [(b, c0) for b in range(len(blocks)) for c0 in range(0, n2, col_block)]

        def qk(n):
            b, c0 = items[n]
            s = _dot(kv[b][0], qt_ref[:, c0:c0 + col_block])
            if blocks[b][1] is not None:
                s = s + tiles_ref[0, blocks[b][1], :, c0 % t:c0 % t + col_block]
            s_ref[n % (ATTN_LOOKAHEAD + 1)] = s
            return jnp.max(s.reshape(groups, SUBLANES, col_block), axis=0)

        part_max = {n: qk(n) for n in range(min(ATTN_LOOKAHEAD, len(items)))}
        mt = min(ATTN_MXU_TILE, t, col_block)
        for idx, it in enumerate(items):
            b, c0 = it
            cols = slice(c0, c0 + col_block)
            if idx + ATTN_LOOKAHEAD < len(items):
                part_max[idx + ATTN_LOOKAHEAD] = qk(idx + ATTN_LOOKAHEAD)
            s = s_ref.at[idx % (ATTN_LOOKAHEAD + 1)]
            m_prev = m_ref[:, cols]
            m_new = jnp.maximum(m_prev, jnp.max(part_max.pop(idx), axis=0, keepdims=True))
            alpha = jnp.exp2(m_prev - m_new)
            l_new, pv = [], []
            for n0 in range(0, col_block, mt):
                l_n = alpha[:, n0:n0 + mt] * l_ref[:, c0 + n0:c0 + n0 + mt]
                pv_n = None
                for k0 in range(0, t, mt):
                    p3 = jnp.exp2(s[k0:k0 + mt, n0:n0 + mt].reshape(mt // SUBLANES, SUBLANES, mt)
                                  - m_new[None, :, n0:n0 + mt])
                    l_n = l_n + jnp.sum(p3, axis=0)
                    d = _dot(kv[b][1][:, k0:k0 + mt], p3.reshape(mt, mt).astype(BF16))
                    pv_n = d if pv_n is None else pv_n + d
                l_new.append(l_n)
                pv.append(pv_n)
            l_ref[:, cols] = jnp.concatenate(l_new, axis=1)
            acc = acc_ref[:, cols].reshape(DV_B // SUBLANES, SUBLANES, col_block) * alpha[None]
            acc_ref[:, cols] = acc.reshape(DV_B, col_block) + jnp.concatenate(pv, axis=1)
            m_ref[:, cols] = m_new

    n_far = jnp.maximum(i - 1, 0)
    per_trip = ATTN_BLOCKS_PER_TRIP

    def far_group(jj, carry):
        run([(per_trip * jj + u, None) for u in range(per_trip)])
        return carry

    lax.fori_loop(0, n_far // per_trip, far_group, 0)
    rem = n_far % per_trip
    for r in range(per_trip):
        @pl.when((i >= 1) & (rem == r))
        def _():
            run([(n_far - r + u, None) for u in range(r)] + [(i - 1, 1), (i, 0)])

    @pl.when(i == 0)
    def _():
        run([(i, 0)])

    lam = _lam_of(lp_ref, lam_init)
    acc = acc_ref[...] / jnp.sum(l_ref[...], axis=0, keepdims=True)
    o = (acc[:, 0:t] - lam * acc[:, t:n2]).T
    o_ref[0] = (_rms(o, sub_ref[...]) * (1.0 - lam_init)).astype(o_ref.dtype)


def _attn_prompt(q, k, vt, rel_table, lam_params, subln, *, t, lam_init):
    b, l, _ = q.shape
    nq = l // t
    assert t >= MAX_DISTANCE
    assert vt.shape == (H_B, b * nq, DV_B, t)
    table = rel_table.astype(F32)
    dist = jnp.arange(-t + 1, 2 * t)[None, None, :]
    f = _bias_rows(jnp.broadcast_to(dist, (H_B, 1, 3 * t - 1)), table.T[:, None, :])[:, 0]
    f = (f - table[NUM_BUCKETS - 1][:, None]) * LOG2E

    def toeplitz(w):
        wp = jnp.pad(w, ((0, 0), (0, 1)))
        g = jnp.tile(wp, (1, t))[:, :t * (2 * t - 1)].reshape(H_B, t, 2 * t - 1)
        return g[:, :, t - 1:]

    tiles = jnp.stack([toeplitz(f[:, :2 * t - 1]), toeplitz(f[:, t:])], axis=1)
    hw = 2 * DQK_B
    return pl.pallas_call(
        functools.partial(_attn_prompt_body, t=t, col_block=min(ATTN_COL_BLOCK, t), lam_init=lam_init),
        grid=(b, H_B, nq),
        in_specs=[pl.BlockSpec((1, t, hw), lambda bi, h, i: (bi, i, h)),
                  pl.BlockSpec((1, l, hw), lambda bi, h, i: (bi, 0, h)),
                  pl.BlockSpec((1, nq, DV_B, t), lambda bi, h, i: (h, bi, 0, 0)),
                  pl.BlockSpec((1, 2, t, t), lambda bi, h, i: (h, 0, 0, 0)),
                  _const_spec(lam_params.shape), _const_spec((1, DV_B))],
        out_specs=pl.BlockSpec((1, t, DV_B), lambda bi, h, i: (bi, i, h)),
        out_shape=jax.ShapeDtypeStruct((b, l, H_B * DV_B), BF16),
        scratch_shapes=[pltpu.VMEM((hw, 2 * t), BF16), pltpu.VMEM((SUBLANES, 2 * t), F32),
                        pltpu.VMEM((SUBLANES, 2 * t), F32), pltpu.VMEM((DV_B, 2 * t), F32),
                        pltpu.VMEM((ATTN_LOOKAHEAD + 1, t, min(ATTN_COL_BLOCK, t)), F32)],
        compiler_params=_params(("parallel", "parallel", "arbitrary")),
        name="attn_prompt",
    )(q, k, vt, tiles, lam_params, subln.reshape(1, DV_B))


def _attn_sample_body(pt_ref, q_ref, kn_ref, vn_ref, bpast_ref, bself_ref, lp_ref, sub_ref, *rest,
                      pages, lam_init):
    k_refs = rest[:pages]
    v_refs = rest[pages:2 * pages]
    o_ref, qf_ref, m_ref, l_ref, acc_ref = rest[2 * pages:]
    j = pl.program_id(1)
    rows = qf_ref.shape[0]
    half = rows // 2
    page_cols = PAGE_SIZE * H_B

    @pl.when(j == 0)
    def _():
        q = q_ref[0].astype(F32)
        r = lax.broadcasted_iota(jnp.int32, q.shape, 0)
        lane = lax.broadcasted_iota(jnp.int32, q.shape, 1)
        qf = jnp.where(lane // DQK_B == r // half, q, 0.0)
        qf_ref[...] = qf
        kn = kn_ref[0]
        vn = vn_ref[0]
        n_self = kn.shape[0]
        s_self = [jnp.sum(qf * kn[c:c + 1], axis=-1, keepdims=True) + bself_ref[:, c:c + 1]
                  for c in range(n_self)]
        m0 = functools.reduce(jnp.maximum, s_self)
        p_self = [jnp.exp2(s - m0) for s in s_self]
        m_ref[...] = m0
        l_ref[...] = functools.reduce(jnp.add, p_self)
        acc_ref[...] = functools.reduce(jnp.add, [p * vn[c:c + 1] for c, p in enumerate(p_self)])

    q16 = qf_ref[...].astype(BF16)
    group = math.gcd(pages, SAMPLE_PAGE_GROUP)
    groups = [range(g, g + group) for g in range(0, pages, group)]
    qk = lambda grp: [_dot_nt(q16, k_refs[c][0].astype(BF16)) for c in grp]
    scores = qk(groups[0])
    m_run, l_run, acc = m_ref[...], l_ref[...], acc_ref[...]
    for gi, grp in enumerate(groups):
        cur = scores
        if gi + 1 < len(groups):
            scores = qk(groups[gi + 1])
        s = (jnp.concatenate(cur, axis=-1)
             + bpast_ref[j, :, grp[0] * page_cols:(grp[-1] + 1) * page_cols])
        m_new = jnp.maximum(m_run, jnp.max(s, axis=-1, keepdims=True))
        alpha = jnp.exp2(m_run - m_new)
        p = jnp.exp2(s - m_new)
        l_run = alpha * l_run + jnp.sum(p, axis=-1, keepdims=True)
        p16 = p.astype(BF16)
        pv = functools.reduce(jnp.add, [_dot(p16[:, i * page_cols:(i + 1) * page_cols], v_refs[c][0].astype(BF16))
                                        for i, c in enumerate(grp)])
        acc = alpha * acc + pv
        m_run = m_new
    m_ref[...], l_ref[...], acc_ref[...] = m_run, l_run, acc

    @pl.when(j == pl.num_programs(1) - 1)
    def _():
        lam = _lam_of(lp_ref, lam_init)
        an = acc_ref[...] / l_ref[...]
        o = an[0:half] - lam * an[half:rows]
        o_ref[0] = _rms(o, sub_ref[...]) * (1.0 - lam_init)


def _attn_sample(q, k_new, v_new, cache_k, cache_v, page_base, page_table, rel_table, lam_params, subln,
                 *, pages, lam_init):
    bd, l_new, width = q.shape
    n_pages = page_table.shape[1]
    past = n_pages * PAGE_SIZE
    half = l_new * H_B
    rows = 2 * half
    page_cols = PAGE_SIZE * H_B
    table = rel_table.astype(F32)
    t_of = (jnp.arange(rows) % half) // H_B
    h_of = jnp.arange(rows) % H_B
    tab_rows = table.T[h_of]
    far = table[NUM_BUCKETS - 1][h_of][:, None]
    own = h_of[:, None, None] == jnp.arange(H_B)[None, None, :]
    rel_past = past + t_of[:, None] - jnp.arange(past)[None, :]
    bpast = jnp.where(own, ((_bias_rows(rel_past, tab_rows) - far) * LOG2E)[:, :, None], NEG_INF)
    n_steps = n_pages // pages
    bpast = bpast.reshape(rows, n_steps, pages * page_cols).transpose(1, 0, 2)
    n_self = l_new * H_B
    t_key = jnp.arange(LANES) // H_B
    rel_self = jnp.where((jnp.arange(LANES)[None, :] < n_self) & (h_of[:, None] == jnp.arange(LANES)[None, :] % H_B),
                         t_of[:, None] - t_key[None, :], -1)
    bself = (_bias_rows(rel_self, tab_rows) - far) * LOG2E

    hd = width // H_B
    q_rows = q.reshape(bd, half, hd)
    q_rows = jnp.concatenate([q_rows, q_rows], axis=1)
    kn = k_new.reshape(bd, n_self, hd)
    vn = v_new.reshape(bd, n_self, hd)

    def page_spec(c):
        return pl.BlockSpec((1, page_cols, hd),
                            lambda b, j, pt: (page_base + pt[b * n_pages + j * pages + c], 0, 0))

    seq_spec = lambda r: pl.BlockSpec((1, r, hd), lambda b, j, pt: (b, 0, 0))
    whole = lambda shape: pl.BlockSpec(shape, lambda b, j, pt: (0,) * len(shape), pipeline_mode=pl.Buffered(1))
    grid_spec = pltpu.PrefetchScalarGridSpec(
        num_scalar_prefetch=1,
        grid=(bd, n_pages // pages),
        in_specs=[seq_spec(rows), seq_spec(n_self), seq_spec(n_self),
                  whole(bpast.shape), whole(bself.shape), whole(lam_params.shape), whole((1, DV_B))]
                 + [page_spec(c) for c in range(pages)] + [page_spec(c) for c in range(pages)],
        out_specs=seq_spec(half),
        scratch_shapes=[pltpu.VMEM((rows, hd), F32), pltpu.VMEM((rows, 1), F32),
                        pltpu.VMEM((rows, 1), F32), pltpu.VMEM((rows, DV_B), F32)])
    out = pl.pallas_call(
        functools.partial(_attn_sample_body, pages=pages, lam_init=lam_init),
        grid_spec=grid_spec,
        out_shape=jax.ShapeDtypeStruct((bd, half, DV_B), F32),
        compiler_params=_params(("parallel", "arbitrary")),
        name="attn_sample",
    )(page_table.reshape(-1), q_rows, kn, vn, bpast, bself, lam_params, subln.reshape(1, DV_B),
      *([cache_k] * pages), *([cache_v] * pages))
    return out.reshape(bd, l_new, width)


def _merge_ffn_body(x_ref, ya_ref, yb_ref, ga_ref, gb_ref, wa_ref, wb_ref, wo_ref, post_ref,
                    pre2_ref, wg_ref, wu_ref, wd_ref, post2_ref, o_ref, *, f_chunk):
    merged = (jax.nn.sigmoid(ga_ref[...]) * _dot(ya_ref[...], wa_ref[...])
              + jax.nn.sigmoid(gb_ref[...]) * _dot(yb_ref[...], wb_ref[...]))
    x = x_ref[...] + _rms(_dot(merged.astype(BF16), wo_ref[...]), post_ref[...])
    o_ref[...] = _ffn_half_step(x, pre2_ref, wg_ref, wu_ref, wd_ref, post2_ref, f_chunk)


def _merge_ffn(x, ya, yb, ga, gb, wa, wb, wo, post, pre2, wg, wu, wd, post2, *, tm):
    t, d = x.shape
    d_ff = wg.shape[1]
    f_chunk = FFN_CHUNK if d_ff % FFN_CHUNK == 0 else d_ff
    row = lambda w: pl.BlockSpec((tm, w), lambda i: (i, 0))
    weights = (wa, wb, wo, post, pre2, wg, wu, wd, post2)
    return pl.pallas_call(
        functools.partial(_merge_ffn_body, f_chunk=f_chunk),
        grid=(t // tm,),
        in_specs=[row(d), row(ya.shape[1]), row(yb.shape[1]), row(d), row(d)]
                 + [_const_spec(w.shape) for w in weights],
        out_specs=row(d),
        out_shape=jax.ShapeDtypeStruct((t, d), F32),
        compiler_params=_params(("parallel",)),
        name="merge_ffn",
    )(x, ya, yb, ga, gb, *weights)


def _lambda_init(layer):
    return 0.8 - 0.6 * math.exp(-0.3 * layer)


def _token_tile(t):
    tm = TOKEN_TILE
    while t % tm:
        tm //= 2
    return tm


def _attn_tile(seq):
    return min(ATTN_BLOCK, seq)


def _layer_weights(l, W):
    d = W["w_in"].shape[1]
    bf = lambda a: a.astype(BF16)
    row = lambda a: a.reshape(1, -1).astype(F32)
    widths = {"qkv": H_A * (2 * DK_A + DV_A), "z": H_A * DV_A, "beta": H_A, "alpha": H_A,
              "q": H_B * 2 * DQK_B, "k": H_B * 2 * DQK_B, "v": H_B * DV_B, "ga": d, "gb": d}
    order = ("qkv", "z", "beta", "alpha", "q", "k", "v", "ga", "gb")
    offs, off = {}, 0
    for n in order:
        offs[n] = off
        off += widths[n]
    w_in = W["w_in"][l]
    col = lambda n: w_in[:, offs[n]:offs[n] + widths[n]]
    pad = jnp.zeros((d, LANES - 2 * H_A), w_in.dtype)
    w_r = jnp.concatenate([col("qkv"), col("z"), col("q"), col("k"), col("v"), col("ga"), col("gb"),
                           col("beta"), col("alpha"), pad], axis=1)
    groups = tuple((n, (d if w is None else w), dt) for n, w, dt in _PROJ_OUT)
    lam_params = jnp.stack([W["lam_q1"][l], W["lam_k1"][l], W["lam_q2"][l], W["lam_k2"][l]]).astype(F32)
    return dict(
        ffn1=(row(W["ffn1_pre"][l]), bf(W["ffn1_wg"][l]), bf(W["ffn1_wu"][l]), bf(W["ffn1_wd"][l]),
              row(W["ffn1_post"][l])),
        ffn2=(row(W["ffn2_pre"][l]), bf(W["ffn2_wg"][l]), bf(W["ffn2_wu"][l]), bf(W["ffn2_wd"][l]),
              row(W["ffn2_post"][l])),
        mix_pre=row(W["mix_pre"][l]), w_r=bf(w_r), groups=groups,
        conv_w=W["conv_w"][l].astype(F32), a_log=W["a_log"][l].astype(F32), dt_bias=W["dt_bias"][l].astype(F32),
        gdn_norm=W["gdn_norm"][l].astype(F32), lam_params=lam_params, subln=W["subln"][l].astype(F32),
        w_a=bf(W["w_a"][l]), w_b=bf(W["w_b"][l]), w_out=bf(W["w_out"][l]), mix_post=row(W["mix_post"][l]),
        lam_init=_lambda_init(l))


def _decoder_layer(x, lw, conv_state, ssm_state, attend, attn_tile=None):
    b, l, d = x.shape
    t = b * l
    tm = attn_tile or _token_tile(t)
    x2 = _ffn(x.reshape(t, d), *lw["ffn1"], tm=tm)
    c = _proj(x2, lw["mix_pre"], lw["w_r"], lw["groups"], tm=tm, attn_operands=attn_tile is not None)
    c3 = {n: c[n].reshape(b, l, c[n].shape[-1]) for n in ("qkv", "z", "ba")}

    chunk = min(CHUNK, l)
    if chunk % SUBLANES:
        chunk = -(-chunk // SUBLANES) * SUBLANES
    lp = -(-l // chunk) * chunk
    padl = lambda a: jnp.pad(a, ((0, 0), (0, lp - l), (0, 0))) if lp != l else a
    ya, ssm_new = _gdn(padl(c3["qkv"]), conv_state, padl(c3["z"]), padl(c3["ba"]), ssm_state,
                       lw["conv_w"], lw["a_log"], lw["dt_bias"], lw["gdn_norm"],
                       chunk=chunk, n_chunk=math.gcd(lp // chunk, GDN_CHUNKS_PER_STEP),
                       n_seq=math.gcd(b, GDN_SEQS_PER_STEP) if lp == chunk else 1,
                       l_valid=min(l - (lp - chunk), chunk))
    ya = ya[:, :l]
    if l >= CONV_W - 1:
        conv_new = c3["qkv"][:, l - (CONV_W - 1):]
    else:
        conv_new = jnp.concatenate([conv_state.astype(F32), c3["qkv"]], axis=1)[:, -(CONV_W - 1):]

    yb = attend(c, b, l)
    y = _merge_ffn(x2, ya.reshape(t, -1), yb.reshape(t, -1).astype(BF16), c["ga"], c["gb"],
                   lw["w_a"], lw["w_b"], lw["w_out"], lw["mix_post"], *lw["ffn2"], tm=tm)
    return (y.reshape(b, l, d), c["k"].reshape(b, l, H_B, 2 * DQK_B), c["v"].reshape(b, l, H_B, DV_B),
            conv_new, ssm_new)


def kernel(x_prompt, x_sample, cache_k, cache_v, state_conv, state_ssm, page_table, rel_table,
           ffn1_pre, ffn1_wg, ffn1_wu, ffn1_wd, ffn1_post, mix_pre, w_in, conv_w, a_log, dt_bias,
           gdn_norm, lam_q1, lam_k1, lam_q2, lam_k2, subln, w_a, w_b, w_out, mix_post,
           ffn2_pre, ffn2_wg, ffn2_wu, ffn2_wd, ffn2_post):
    W = dict(ffn1_pre=ffn1_pre, ffn1_wg=ffn1_wg, ffn1_wu=ffn1_wu, ffn1_wd=ffn1_wd, ffn1_post=ffn1_post,
             mix_pre=mix_pre, w_in=w_in, conv_w=conv_w, a_log=a_log, dt_bias=dt_bias, gdn_norm=gdn_norm,
             lam_q1=lam_q1, lam_k1=lam_k1, lam_q2=lam_q2, lam_k2=lam_k2, subln=subln, w_a=w_a, w_b=w_b,
             w_out=w_out, mix_post=mix_post, ffn2_pre=ffn2_pre, ffn2_wg=ffn2_wg, ffn2_wu=ffn2_wu,
             ffn2_wd=ffn2_wd, ffn2_post=ffn2_post)
    depth = w_in.shape[0]
    bp, seq, _ = x_prompt.shape
    n_pool = cache_k.shape[1]
    n_pages = page_table.shape[1]
    t_attn = _attn_tile(seq)
    pages = math.gcd(n_pages, SAMPLE_PAGES_PER_STEP)
    ck = cache_k.reshape(depth * n_pool, PAGE_SIZE * H_B, -1)
    cv = cache_v.reshape(depth * n_pool, PAGE_SIZE * H_B, -1)
    xp, xs = x_prompt, x_sample
    outs = [[] for _ in range(8)]
    for l in range(depth):
        lw = _layer_weights(l, W)

        def attend_prompt(c, b, n):
            return _attn_prompt(c["q"].reshape(b, n, -1), c["k16"].reshape(b, n, -1), c["vt"], rel_table,
                                lw["lam_params"], lw["subln"], t=t_attn, lam_init=lw["lam_init"])

        def attend_sample(c, b, n):
            return _attn_sample(c["q"].reshape(b, n, -1), c["k"], c["v"], ck, cv, l * n_pool, page_table,
                                rel_table, lw["lam_params"], lw["subln"], pages=pages, lam_init=lw["lam_init"])

        zero_conv = jnp.zeros((bp, CONV_W - 1, state_conv.shape[-1]), x_prompt.dtype)
        zero_ssm = jnp.zeros((bp,) + state_ssm.shape[2:], state_ssm.dtype)
        xp, k1, v1, c1, s1 = _decoder_layer(xp, lw, zero_conv, zero_ssm, attend_prompt, attn_tile=t_attn)
        xs, k2, v2, c2, s2 = _decoder_layer(xs, lw, state_conv[l], state_ssm[l], attend_sample)
        for o, a in zip(outs, (k1, v1, c1, s1, k2, v2, c2, s2)):
            o.append(a)
    return (xp, xs) + tuple(jnp.stack(o) for o in outs)
```

```python
import functools
import math

import jax
import jax.numpy as jnp
from jax import lax
from jax.experimental import pallas as pl
from jax.experimental.pallas import tpu as pltpu

F32 = jnp.float32
BF16 = jnp.bfloat16

H_A = 4
DK_A = 128
DV_A = 128
CONV_W = 4
CHUNK = 64
H_B = 4
DQK_B = 64
DV_B = 2 * DQK_B
PAGE_SIZE = 128
NUM_BUCKETS = 32
MAX_DISTANCE = 128
EPS = 1e-6
NEG_INF = -1e30

LANES = 128
SUBLANES = 8
V7X_VMEM_BYTES = 64 * 1024 * 1024
VMEM_LIMIT = V7X_VMEM_BYTES - 8 * 1024 * 1024

LOG2E = math.log2(math.e)
QK_LOGIT_SCALE = DQK_B ** -0.5 * LOG2E

TOKEN_TILE = 512
FFN_CHUNK = 256
GDN_CHUNKS_PER_STEP = 8
GDN_SEQS_PER_STEP = 8
ATTN_BLOCK = 512
ATTN_COL_BLOCK = 512
ATTN_MXU_TILE = 256
ATTN_LOOKAHEAD = 2
ATTN_BLOCKS_PER_TRIP = 8
SAMPLE_PAGES_PER_STEP = 32
SAMPLE_PAGE_GROUP = 4


def _dot(a, b):
    return jnp.dot(a, b, preferred_element_type=F32)


def _dot_nt(a, b):
    return lax.dot_general(a, b, (((1,), (1,)), ((), ())), preferred_element_type=F32)


def _rms(x, g):
    return x * lax.rsqrt(jnp.mean(x * x, axis=-1, keepdims=True) + EPS) * g


def _silu(x):
    return x * jax.nn.sigmoid(x)


def _const_spec(shape):
    nd = len(shape)
    return pl.BlockSpec(shape, lambda *_: (0,) * nd, pipeline_mode=pl.Buffered(1))


def _params(sem):
    return pltpu.CompilerParams(dimension_semantics=sem, vmem_limit_bytes=VMEM_LIMIT)


def _ffn_half_step(x, pre_ref, wg_ref, wu_ref, wd_ref, post_ref, f_chunk):
    h = _rms(x, pre_ref[...]).astype(BF16)
    d_ff = wg_ref.shape[1]
    acc = jnp.zeros(x.shape, F32)
    for c in range(d_ff // f_chunk):
        sl = slice(c * f_chunk, (c + 1) * f_chunk)
        g = _dot(h, wg_ref[:, sl])
        u = _dot(h, wu_ref[:, sl])
        acc = acc + _dot((_silu(g) * u).astype(BF16), wd_ref[sl, :])
    return x + 0.5 * _rms(acc, post_ref[...])


def _ffn_body(x_ref, pre_ref, wg_ref, wu_ref, wd_ref, post_ref, o_ref, *, f_chunk):
    o_ref[...] = _ffn_half_step(x_ref[...], pre_ref, wg_ref, wu_ref, wd_ref, post_ref, f_chunk)


def _ffn(x, pre, wg, wu, wd, post, *, tm):
    t, d = x.shape
    d_ff = wg.shape[1]
    f_chunk = FFN_CHUNK if d_ff % FFN_CHUNK == 0 else d_ff
    return pl.pallas_call(
        functools.partial(_ffn_body, f_chunk=f_chunk),
        grid=(t // tm,),
        in_specs=[pl.BlockSpec((tm, d), lambda i: (i, 0)),
                  _const_spec((1, d)), _const_spec((d, d_ff)), _const_spec((d, d_ff)),
                  _const_spec((d_ff, d)), _const_spec((1, d))],
        out_specs=pl.BlockSpec((tm, d), lambda i: (i, 0)),
        out_shape=jax.ShapeDtypeStruct((t, d), F32),
        compiler_params=_params(("parallel",)),
        name="ffn",
    )(x, pre, wg, wu, wd, post)


_PROJ_OUT = (("qkv", H_A * (2 * DK_A + DV_A), F32), ("z", H_A * DV_A, F32),
             ("q", H_B * 2 * DQK_B, BF16), ("k", H_B * 2 * DQK_B, F32), ("v", H_B * DV_B, F32),
             ("ga", None, F32), ("gb", None, F32), ("ba", LANES, F32))


def _proj_body(x_ref, g_ref, w_ref, *o_refs, groups, names):
    h = _rms(x_ref[...], g_ref[...]).astype(BF16)
    refs = dict(zip(names, o_refs))
    off = 0
    for name, w, _ in groups:
        y = _dot(h, w_ref[:, off:off + w])
        if name == "q":
            y = y * QK_LOGIT_SCALE
        if name in ("k", "v"):
            hd = w // H_B
            for hh in range(H_B):
                refs[name][:, hh, :] = y[:, hh * hd:(hh + 1) * hd]
            if name == "k" and "k16" in refs:
                refs["k16"][...] = y.astype(BF16)
            if name == "v" and "vt" in refs:
                for hh in range(H_B):
                    refs["vt"][hh, 0] = y[:, hh * hd:(hh + 1) * hd].T.astype(BF16)
        else:
            refs[name][...] = y.astype(refs[name].dtype)
        off += w


def _proj(x, g, w_r, groups, *, tm, attn_operands):
    t, d = x.shape
    row = lambda w: pl.BlockSpec((tm, w), lambda i: (i, 0))
    names, specs, shapes = [], [], []
    for n, w, dt in groups:
        names.append(n)
        if n in ("k", "v"):
            specs.append(pl.BlockSpec((tm, H_B, w // H_B), lambda i: (i, 0, 0)))
            shapes.append(jax.ShapeDtypeStruct((t, H_B, w // H_B), dt))
        else:
            specs.append(row(w))
            shapes.append(jax.ShapeDtypeStruct((t, w), dt))
    if attn_operands:
        wk, wv = groups[3][1], groups[4][1]
        names += ["k16", "vt"]
        specs += [row(wk), pl.BlockSpec((H_B, 1, wv // H_B, tm), lambda i: (0, i, 0, 0))]
        shapes += [jax.ShapeDtypeStruct((t, wk), BF16), jax.ShapeDtypeStruct((H_B, t // tm, wv // H_B, tm), BF16)]
    res = pl.pallas_call(
        functools.partial(_proj_body, groups=groups, names=names),
        grid=(t // tm,),
        in_specs=[row(d), _const_spec((1, d)), _const_spec(w_r.shape)],
        out_specs=specs,
        out_shape=shapes,
        compiler_params=_params(("parallel",)),
        name="proj",
    )(x, g, w_r)
    return dict(zip(names, res))


def _split2(x):
    hi = x.astype(BF16)
    return hi, (x - hi.astype(F32)).astype(BF16)


def _split3(x):
    hi = x.astype(BF16)
    r = x - hi.astype(F32)
    mid = r.astype(BF16)
    return hi, mid, (r - mid.astype(F32)).astype(BF16)


def _mm3(x, y):
    x_hi, x_lo = _split2(x)
    y_hi, y_lo = _split2(y)
    return _dot(x_hi, y_hi) + _dot(x_lo, y_hi) + _dot(x_hi, y_lo)


def _neumann_level(p, t_inv, c):
    if c % 16 == 0:
        x_hi, x_lo = _split2(jnp.concatenate([p, t_inv], axis=0))
        p_hi, p_lo = x_hi[:c], x_lo[:c]
        y = _dot(jnp.concatenate([x_hi, x_lo], axis=0), p_hi)
        y = y[:2 * c] + y[2 * c:] + _dot(x_hi, p_lo)
        return y[:c], t_inv + y[c:]
    p_hi, p_lo = _split2(p)
    t_hi, t_lo = _split2(t_inv)
    pp = _dot(p_hi, p_hi) + _dot(p_lo, p_hi) + _dot(p_hi, p_lo)
    tp = _dot(t_hi, p_hi) + _dot(t_lo, p_hi) + _dot(t_hi, p_lo)
    return pp, t_inv + tp


def _gdn_body(qkv_ref, cst_ref, z_ref, ba_ref, bat_ref, s0_ref, cw_ref, prow_ref, pcol_ref, gn_ref,
              ya_ref, s_ref, ext_ref, *, chunk, n_chunk, n_seq, l_valid):
    c = chunk
    tb = n_chunk * c
    j = pl.program_id(1)

    @pl.when(j == 0)
    def _():
        ext_ref[:, 0:SUBLANES, :] = cst_ref[...]
        s_ref[...] = s0_ref[...]

    ii = lax.broadcasted_iota(jnp.int32, (c, c), 0)
    jj = lax.broadcasted_iota(jnp.int32, (c, c), 1)
    causal = ii >= jj
    strict = ii > jj
    ltri16 = causal.astype(BF16)
    eye = (ii == jj).astype(F32)
    cw = cw_ref[...]
    hk = H_A * DK_A

    rows = lambda n: slice(n * c, (n + 1) * c)
    act, beta_c, g_c = [], [], []
    for s in range(n_seq):
        u_raw = qkv_ref[s]
        ext_ref[s, SUBLANES:SUBLANES + tb, :] = u_raw
        conv = u_raw * cw[CONV_W - 1:CONV_W, :]
        for k in range(1, CONV_W):
            conv = conv + ext_ref[s, SUBLANES - k:SUBLANES - k + tb, :] * cw[CONV_W - 1 - k:CONV_W - k, :]
        ext_ref[s, 0:SUBLANES, :] = ext_ref[s, tb:tb + SUBLANES, :]
        act.append(_silu(conv))
        ba = ba_ref[s]
        beta_s = jax.nn.sigmoid(ba)
        g_s = -jnp.exp(prow_ref[0:1, :]) * jax.nn.softplus(ba + prow_ref[1:2, :])
        if l_valid < c:
            valid = lax.broadcasted_iota(jnp.int32, g_s.shape, 0) < l_valid
            g_s = jnp.where(valid, g_s, 0.0)
            beta_s = jnp.where(valid, beta_s, 0.0)
        beta_c.append(beta_s)
        g_c.append(g_s)

    blocks = [(s, n) for s in range(n_seq) for n in range(n_chunk)]
    chains = [(s, n, h) for s, n in blocks for h in range(H_A)]

    gcum_c, gcum_r = {}, {}
    for s, n in blocks:
        g_r = -jnp.exp(pcol_ref[0][:, :c]) * jax.nn.softplus(bat_ref[s, n] + pcol_ref[1][:, :c])
        if l_valid < c:
            g_r = jnp.where(lax.broadcasted_iota(jnp.int32, g_r.shape, 1) < l_valid, g_r, 0.0)
        gcum_c[s, n] = functools.reduce(jnp.add, [_dot(ltri16, x) for x in _split3(g_c[s][rows(n)])])
        gcum_r[s, n] = functools.reduce(jnp.add, [_dot_nt(x, ltri16) for x in _split3(g_r)])

    st = {}
    for ch in chains:
        s, n, h = ch
        rs = rows(n)
        q_raw = act[s][rs, h * DK_A:(h + 1) * DK_A]
        k_raw = act[s][rs, hk + h * DK_A:hk + (h + 1) * DK_A]
        v = act[s][rs, 2 * hk + h * DV_A:2 * hk + (h + 1) * DV_A]
        q = q_raw * lax.rsqrt(jnp.sum(q_raw * q_raw, axis=-1, keepdims=True) + EPS) * (DK_A ** -0.5)
        k = k_raw * lax.rsqrt(jnp.sum(k_raw * k_raw, axis=-1, keepdims=True) + EPS)
        beta = beta_c[s][rs, h:h + 1]
        gc = gcum_c[s, n][:, H_A + h:H_A + h + 1]
        gr = gcum_r[s, n][H_A + h:H_A + h + 1, :]
        decay = jnp.where(causal, jnp.exp(jnp.where(causal, gc - gr, 0.0)), 0.0)
        kb = k * beta
        kq = _dot_nt(jnp.concatenate([kb, q], axis=0).astype(BF16), k.astype(BF16))
        eg = jnp.exp(gc)
        g_last = gc[c - 1:c, :]
        p = -jnp.where(strict, kq[:c] * decay, 0.0)
        st[ch] = dict(p=p, t=eye + p, qk=(kq[c:] * decay).astype(BF16),
                      rhs=jnp.concatenate([v * beta, kb * eg], axis=1).astype(BF16), qe=q * eg,
                      k_dec=(k * jnp.exp(g_last - gc)).astype(BF16), dec=jnp.exp(g_last))

    n_factor = c.bit_length() - 1
    if n_factor >= 2:
        for e in st.values():
            e["p"] = _mm3(e["p"], e["p"])
        for _ in range(n_factor - 2):
            for e in st.values():
                e["p"], e["t"] = _neumann_level(e["p"], e["t"], c)
        for e in st.values():
            e["t"] = e["t"] + _mm3(e["t"], e["p"])
    for e in st.values():
        uw = _dot(e["t"].astype(BF16), e["rhs"])
        e["u"] = uw[:, :DV_A]
        e["wq"] = jnp.concatenate([uw[:, DV_A:], e["qe"]], axis=0).astype(BF16)

    state = {(s, h): s_ref[s, h] for s in range(n_seq) for h in range(H_A)}
    for n in range(n_chunk):
        grp = [(s, h) for s in range(n_seq) for h in range(H_A)]
        ws = {g: _dot(st[g[0], n, g[1]]["wq"], state[g].astype(BF16)) for g in grp}
        v_new = {g: (st[g[0], n, g[1]]["u"] - ws[g][:c]).astype(BF16) for g in grp}
        for g in grp:
            s, h = g
            e = st[s, n, h]
            o = ws[g][c:] + _dot(e["qk"], v_new[g])
            state[g] = state[g] * e["dec"] + lax.dot_general(
                e["k_dec"], v_new[g], (((0,), (0,)), ((), ())), preferred_element_type=F32)
            zh = z_ref[s, rows(n), h * DV_A:(h + 1) * DV_A]
            ya_ref[s, rows(n), h * DV_A:(h + 1) * DV_A] = (_rms(o, gn_ref[...]) * _silu(zh)).astype(ya_ref.dtype)
    for (s, h), val in state.items():
        s_ref[s, h] = val


def _gdn(qkv, conv_state, z, ba, s0, conv_w, a_log, dt_bias, g_norm, *, chunk, n_chunk, n_seq, l_valid):
    b, l, cc = qkv.shape
    n = l // chunk
    tb = n_chunk * chunk
    assert l_valid == chunk or n == 1
    assert l % tb == 0 and b % n_seq == 0
    assert chunk & (chunk - 1) == 0
    cst = jnp.pad(conv_state, ((0, 0), (SUBLANES - (CONV_W - 1), 0), (0, 0)))
    bat = ba[..., :2 * H_A].reshape(b, n, chunk, 2 * H_A).transpose(0, 1, 3, 2)
    zeros_h = jnp.zeros((H_A,), F32)
    lane_pad = jnp.zeros((LANES - 2 * H_A,), F32)
    prow = jnp.stack([jnp.concatenate([zeros_h, a_log, lane_pad]),
                      jnp.concatenate([zeros_h, dt_bias, lane_pad])])
    pcol = jnp.broadcast_to(prow[:, :2 * H_A, None], (2, 2 * H_A, LANES))
    return pl.pallas_call(
        functools.partial(_gdn_body, chunk=chunk, n_chunk=n_chunk, n_seq=n_seq, l_valid=l_valid),
        grid=(b // n_seq, l // tb),
        in_specs=[pl.BlockSpec((n_seq, tb, cc), lambda i, j: (i, j, 0)),
                  pl.BlockSpec((n_seq, SUBLANES, cc), lambda i, j: (i, 0, 0)),
                  pl.BlockSpec((n_seq, tb, H_A * DV_A), lambda i, j: (i, j, 0)),
                  pl.BlockSpec((n_seq, tb, LANES), lambda i, j: (i, j, 0)),
                  pl.BlockSpec((n_seq, n_chunk, 2 * H_A, chunk), lambda i, j: (i, j, 0, 0)),
                  pl.BlockSpec((n_seq, H_A, DK_A, DV_A), lambda i, j: (i, 0, 0, 0)),
                  _const_spec((CONV_W, cc)), _const_spec((2, LANES)),
                  _const_spec((2, 2 * H_A, LANES)), _const_spec((1, DV_A))],
        out_specs=[pl.BlockSpec((n_seq, tb, H_A * DV_A), lambda i, j: (i, j, 0)),
                   pl.BlockSpec((n_seq, H_A, DK_A, DV_A), lambda i, j: (i, 0, 0, 0))],
        out_shape=[jax.ShapeDtypeStruct((b, l, H_A * DV_A), BF16),
                   jax.ShapeDtypeStruct((b, H_A, DK_A, DV_A), F32)],
        scratch_shapes=[pltpu.VMEM((n_seq, SUBLANES + tb, cc), F32)],
        compiler_params=_params(("parallel", "arbitrary")),
        name="gdn",
    )(qkv, cst, z, ba, bat, s0, conv_w, prow, pcol, g_norm.reshape(1, DV_A))


def _rel_bucket(n):
    n = jnp.maximum(n, 0)
    max_exact = NUM_BUCKETS // 2
    large = max_exact + (jnp.log(jnp.maximum(n, 1).astype(F32) / max_exact)
                         / math.log(MAX_DISTANCE / max_exact) * (NUM_BUCKETS - max_exact)).astype(jnp.int32)
    large = jnp.minimum(large, NUM_BUCKETS - 1)
    return jnp.where(n < max_exact, n, large)


def _bias_rows(rel, tab_rows):
    onehot = _rel_bucket(rel)[..., None] == jnp.arange(NUM_BUCKETS)
    b = jnp.sum(jnp.where(onehot, tab_rows[..., None, :], 0.0), axis=-1)
    return jnp.where(rel >= 0, b, NEG_INF)


def _lam_of(lp_ref, lam_init):
    lp = lp_ref[...]
    s1 = jnp.sum(lp[0:1] * lp[1:2], axis=-1, keepdims=True)
    s2 = jnp.sum(lp[2:3] * lp[3:4], axis=-1, keepdims=True)
    return jnp.exp(s1) - jnp.exp(s2) + lam_init


def _attn_prompt_body(q_ref, k_ref, vt_ref, tiles_ref, lp_ref, sub_ref, o_ref,
                        qt_ref, m_ref, l_ref, acc_ref, s_ref, *, t, col_block, lam_init):
    i = pl.program_id(2)
    n2 = 2 * t
    qt = q_ref[0].astype(F32).T
    dim = lax.broadcasted_iota(jnp.int32, qt.shape, 0)
    qt_ref[:, 0:t] = jnp.where(dim < DQK_B, qt, 0.0).astype(BF16)
    qt_ref[:, t:n2] = jnp.where(dim >= DQK_B, qt, 0.0).astype(BF16)
    m_ref[...] = jnp.full(m_ref.shape, NEG_INF, F32)
    l_ref[...] = jnp.zeros(l_ref.shape, F32)
    acc_ref[...] = jnp.zeros(acc_ref.shape, F32)
    groups = t // SUBLANES

    def run(blocks):
        kv = []
        for j, _ in blocks:
            start = pl.multiple_of(j * t, t)
            kv.append((k_ref[0, pl.ds(start, t), :], vt_ref[0, j]))
        items = [(b, c0) for b in range(len(blocks)) for c0 in range(0, n2, col_block)]

        def qk(n):
            b, c0 = items[n]
            s = _dot(kv[b][0], qt_ref[:, c0:c0 + col_block])
            if blocks[b][1] is not None:
                s = s + tiles_ref[0, blocks[b][1], :, c0 % t:c0 % t + col_block]
            s_ref[n % (ATTN_LOOKAHEAD + 1)] = s
            return jnp.max(s.reshape(groups, SUBLANES, col_block), axis=0)

        part_max = {n: qk(n) for n in range(min(ATTN_LOOKAHEAD, len(items)))}
        mt = min(ATTN_MXU_TILE, t, col_block)
        for idx, it in enumerate(items):
            b, c0 = it
            cols = slice(c0, c0 + col_block)
            if idx + ATTN_LOOKAHEAD < len(items):
                part_max[idx + ATTN_LOOKAHEAD] = qk(idx + ATTN_LOOKAHEAD)
            s = s_ref.at[idx % (ATTN_LOOKAHEAD + 1)]
            m_prev = m_ref[:, cols]
            m_new = jnp.maximum(m_prev, jnp.max(part_max.pop(idx), axis=0, keepdims=True))
            alpha = jnp.exp2(m_prev - m_new)
            l_new, pv = [], []
            for n0 in range(0, col_block, mt):
                l_n = alpha[:, n0:n0 + mt] * l_ref[:, c0 + n0:c0 + n0 + mt]
                pv_n = None
                for k0 in range(0, t, mt):
                    p3 = jnp.exp2(s[k0:k0 + mt, n0:n0 + mt].reshape(mt // SUBLANES, SUBLANES, mt)
                                  - m_new[None, :, n0:n0 + mt])
                    l_n = l_n + jnp.sum(p3, axis=0)
                    d = _dot(kv[b][1][:, k0:k0 + mt], p3.reshape(mt, mt).astype(BF16))
                    pv_n = d if pv_n is None else pv_n + d
                l_new.append(l_n)
                pv.append(pv_n)
            l_ref[:, cols] = jnp.concatenate(l_new, axis=1)
            acc = acc_ref[:, cols].reshape(DV_B // SUBLANES, SUBLANES, col_block) * alpha[None]
            acc_ref[:, cols] = acc.reshape(DV_B, col_block) + jnp.concatenate(pv, axis=1)
            m_ref[:, cols] = m_new

    n_far = jnp.maximum(i - 1, 0)
    per_trip = ATTN_BLOCKS_PER_TRIP

    def far_group(jj, carry):
        run([(per_trip * jj + u, None) for u in range(per_trip)])
        return carry

    lax.fori_loop(0, n_far // per_trip, far_group, 0)
    rem = n_far % per_trip
    for r in range(per_trip):
        @pl.when((i >= 1) & (rem == r))
        def _():
            run([(n_far - r + u, None) for u in range(r)] + [(i - 1, 1), (i, 0)])

    @pl.when(i == 0)
    def _():
        run([(i, 0)])

    lam = _lam_of(lp_ref, lam_init)
    acc = acc_ref[...] / jnp.sum(l_ref[...], axis=0, keepdims=True)
    o = (acc[:, 0:t] - lam * acc[:, t:n2]).T
    o_ref[0] = (_rms(o, sub_ref[...]) * (1.0 - lam_init)).astype(o_ref.dtype)


def _attn_prompt(q, k, vt, rel_table, lam_params, subln, *, t, lam_init):
    b, l, _ = q.shape
    nq = l // t
    assert t >= MAX_DISTANCE
    assert vt.shape == (H_B, b * nq, DV_B, t)
    table = rel_table.astype(F32)
    dist = jnp.arange(-t + 1, 2 * t)[None, None, :]
    f = _bias_rows(jnp.broadcast_to(dist, (H_B, 1, 3 * t - 1)), table.T[:, None, :])[:, 0]
    f = (f - table[NUM_BUCKETS - 1][:, None]) * LOG2E

    def toeplitz(w):
        wp = jnp.pad(w, ((0, 0), (0, 1)))
        g = jnp.tile(wp, (1, t))[:, :t * (2 * t - 1)].reshape(H_B, t, 2 * t - 1)
        return g[:, :, t - 1:]

    tiles = jnp.stack([toeplitz(f[:, :2 * t - 1]), toeplitz(f[:, t:])], axis=1)
    hw = 2 * DQK_B
    return pl.pallas_call(
        functools.partial(_attn_prompt_body, t=t, col_block=min(ATTN_COL_BLOCK, t), lam_init=lam_init),
        grid=(b, H_B, nq),
        in_specs=[pl.BlockSpec((1, t, hw), lambda bi, h, i: (bi, i, h)),
                  pl.BlockSpec((1, l, hw), lambda bi, h, i: (bi, 0, h)),
                  pl.BlockSpec((1, nq, DV_B, t), lambda bi, h, i: (h, bi, 0, 0)),
                  pl.BlockSpec((1, 2, t, t), lambda bi, h, i: (h, 0, 0, 0)),
                  _const_spec(lam_params.shape), _const_spec((1, DV_B))],
        out_specs=pl.BlockSpec((1, t, DV_B), lambda bi, h, i: (bi, i, h)),
        out_shape=jax.ShapeDtypeStruct((b, l, H_B * DV_B), BF16),
        scratch_shapes=[pltpu.VMEM((hw, 2 * t), BF16), pltpu.VMEM((SUBLANES, 2 * t), F32),
                        pltpu.VMEM((SUBLANES, 2 * t), F32), pltpu.VMEM((DV_B, 2 * t), F32),
                        pltpu.VMEM((ATTN_LOOKAHEAD + 1, t, min(ATTN_COL_BLOCK, t)), F32)],
        compiler_params=_params(("parallel", "parallel", "arbitrary")),
        name="attn_prompt",
    )(q, k, vt, tiles, lam_params, subln.reshape(1, DV_B))


def _attn_sample_body(pt_ref, q_ref, kn_ref, vn_ref, bpast_ref, bself_ref, lp_ref, sub_ref, *rest,
                      pages, lam_init):
    k_refs = rest[:pages]
    v_refs = rest[pages:2 * pages]
    o_ref, qf_ref, m_ref, l_ref, acc_ref = rest[2 * pages:]
    j = pl.program_id(1)
    rows = qf_ref.shape[0]
    half = rows // 2
    page_cols = PAGE_SIZE * H_B

    @pl.when(j == 0)
    def _():
        q = q_ref[0].astype(F32)
        r = lax.broadcasted_iota(jnp.int32, q.shape, 0)
        lane = lax.broadcasted_iota(jnp.int32, q.shape, 1)
        qf = jnp.where(lane // DQK_B == r // half, q, 0.0)
        qf_ref[...] = qf
        kn = kn_ref[0]
        vn = vn_ref[0]
        n_self = kn.shape[0]
        s_self = [jnp.sum(qf * kn[c:c + 1], axis=-1, keepdims=True) + bself_ref[:, c:c + 1]
                  for c in range(n_self)]
        m0 = functools.reduce(jnp.maximum, s_self)
        p_self = [jnp.exp2(s - m0) for s in s_self]
        m_ref[...] = m0
        l_ref[...] = functools.reduce(jnp.add, p_self)
        acc_ref[...] = functools.reduce(jnp.add, [p * vn[c:c + 1] for c, p in enumerate(p_self)])

    q16 = qf_ref[...].astype(BF16)
    group = math.gcd(pages, SAMPLE_PAGE_GROUP)
    groups = [range(g, g + group) for g in range(0, pages, group)]
    qk = lambda grp: [_dot_nt(q16, k_refs[c][0].astype(BF16)) for c in grp]
    scores = qk(groups[0])
    m_run, l_run, acc = m_ref[...], l_ref[...], acc_ref[...]
    for gi, grp in enumerate(groups):
        cur = scores
        if gi + 1 < len(groups):
            scores = qk(groups[gi + 1])
        s = (jnp.concatenate(cur, axis=-1)
             + bpast_ref[j, :, grp[0] * page_cols:(grp[-1] + 1) * page_cols])
        m_new = jnp.maximum(m_run, jnp.max(s, axis=-1, keepdims=True))
        alpha = jnp.exp2(m_run - m_new)
        p = jnp.exp2(s - m_new)
        l_run = alpha * l_run + jnp.sum(p, axis=-1, keepdims=True)
        p16 = p.astype(BF16)
        pv = functools.reduce(jnp.add, [_dot(p16[:, i * page_cols:(i + 1) * page_cols], v_refs[c][0].astype(BF16))
                                        for i, c in enumerate(grp)])
        acc = alpha * acc + pv
        m_run = m_new
    m_ref[...], l_ref[...], acc_ref[...] = m_run, l_run, acc

    @pl.when(j == pl.num_programs(1) - 1)
    def _():
        lam = _lam_of(lp_ref, lam_init)
        an = acc_ref[...] / l_ref[...]
        o = an[0:half] - lam * an[half:rows]
        o_ref[0] = _rms(o, sub_ref[...]) * (1.0 - lam_init)


def _attn_sample(q, k_new, v_new, cache_k, cache_v, page_base, page_table, rel_table, lam_params, subln,
                 *, pages, lam_init):
    bd, l_new, width = q.shape
    n_pages = page_table.shape[1]
    past = n_pages * PAGE_SIZE
    half = l_new * H_B
    rows = 2 * half
    page_cols = PAGE_SIZE * H_B
    table = rel_table.astype(F32)
    t_of = (jnp.arange(rows) % half) // H_B
    h_of = jnp.arange(rows) % H_B
    tab_rows = table.T[h_of]
    far = table[NUM_BUCKETS - 1][h_of][:, None]
    own = h_of[:, None, None] == jnp.arange(H_B)[None, None, :]
    rel_past = past + t_of[:, None] - jnp.arange(past)[None, :]
    bpast = jnp.where(own, ((_bias_rows(rel_past, tab_rows) - far) * LOG2E)[:, :, None], NEG_INF)
    n_steps = n_pages // pages
    bpast = bpast.reshape(rows, n_steps, pages * page_cols).transpose(1, 0, 2)
    n_self = l_new * H_B
    t_key = jnp.arange(LANES) // H_B
    rel_self = jnp.where((jnp.arange(LANES)[None, :] < n_self) & (h_of[:, None] == jnp.arange(LANES)[None, :] % H_B),
                         t_of[:, None] - t_key[None, :], -1)
    bself = (_bias_rows(rel_self, tab_rows) - far) * LOG2E

    hd = width // H_B
    q_rows = q.reshape(bd, half, hd)
    q_rows = jnp.concatenate([q_rows, q_rows], axis=1)
    kn = k_new.reshape(bd, n_self, hd)
    vn = v_new.reshape(bd, n_self, hd)

    def page_spec(c):
        return pl.BlockSpec((1, page_cols, hd),
                            lambda b, j, pt: (page_base + pt[b * n_pages + j * pages + c], 0, 0))

    seq_spec = lambda r: pl.BlockSpec((1, r, hd), lambda b, j, pt: (b, 0, 0))
    whole = lambda shape: pl.BlockSpec(shape, lambda b, j, pt: (0,) * len(shape), pipeline_mode=pl.Buffered(1))
    grid_spec = pltpu.PrefetchScalarGridSpec(
        num_scalar_prefetch=1,
        grid=(bd, n_pages // pages),
        in_specs=[seq_spec(rows), seq_spec(n_self), seq_spec(n_self),
                  whole(bpast.shape), whole(bself.shape), whole(lam_params.shape), whole((1, DV_B))]
                 + [page_spec(c) for c in range(pages)] + [page_spec(c) for c in range(pages)],
        out_specs=seq_spec(half),
        scratch_shapes=[pltpu.VMEM((rows, hd), F32), pltpu.VMEM((rows, 1), F32),
                        pltpu.VMEM((rows, 1), F32), pltpu.VMEM((rows, DV_B), F32)])
    out = pl.pallas_call(
        functools.partial(_attn_sample_body, pages=pages, lam_init=lam_init),
        grid_spec=grid_spec,
        out_shape=jax.ShapeDtypeStruct((bd, half, DV_B), F32),
        compiler_params=_params(("parallel", "arbitrary")),
        name="attn_sample",
    )(page_table.reshape(-1), q_rows, kn, vn, bpast, bself, lam_params, subln.reshape(1, DV_B),
      *([cache_k] * pages), *([cache_v] * pages))
    return out.reshape(bd, l_new, width)


def _merge_ffn_body(x_ref, ya_ref, yb_ref, ga_ref, gb_ref, wa_ref, wb_ref, wo_ref, post_ref,
                    pre2_ref, wg_ref, wu_ref, wd_ref, post2_ref, o_ref, *, f_chunk):
    merged = (jax.nn.sigmoid(ga_ref[...]) * _dot(ya_ref[...], wa_ref[...])
              + jax.nn.sigmoid(gb_ref[...]) * _dot(yb_ref[...], wb_ref[...]))
    x = x_ref[...] + _rms(_dot(merged.astype(BF16), wo_ref[...]), post_ref[...])
    o_ref[...] = _ffn_half_step(x, pre2_ref, wg_ref, wu_ref, wd_ref, post2_ref, f_chunk)


def _merge_ffn(x, ya, yb, ga, gb, wa, wb, wo, post, pre2, wg, wu, wd, post2, *, tm):
    t, d = x.shape
    d_ff = wg.shape[1]
    f_chunk = FFN_CHUNK if d_ff % FFN_CHUNK == 0 else d_ff
    row = lambda w: pl.BlockSpec((tm, w), lambda i: (i, 0))
    weights = (wa, wb, wo, post, pre2, wg, wu, wd, post2)
    return pl.pallas_call(
        functools.partial(_merge_ffn_body, f_chunk=f_chunk),
        grid=(t // tm,),
        in_specs=[row(d), row(ya.shape[1]), row(yb.shape[1]), row(d), row(d)]
                 + [_const_spec(w.shape) for w in weights],
        out_specs=row(d),
        out_shape=jax.ShapeDtypeStruct((t, d), F32),
        compiler_params=_params(("parallel",)),
        name="merge_ffn",
    )(x, ya, yb, ga, gb, *weights)


def _lambda_init(layer):
    return 0.8 - 0.6 * math.exp(-0.3 * layer)


def _token_tile(t):
    tm = TOKEN_TILE
    while t % tm:
        tm //= 2
    return tm


def _attn_tile(seq):
    return min(ATTN_BLOCK, seq)


def _layer_weights(l, W):
    d = W["w_in"].shape[1]
    bf = lambda a: a.astype(BF16)
    row = lambda a: a.reshape(1, -1).astype(F32)
    widths = {"qkv": H_A * (2 * DK_A + DV_A), "z": H_A * DV_A, "beta": H_A, "alpha": H_A,
              "q": H_B * 2 * DQK_B, "k": H_B * 2 * DQK_B, "v": H_B * DV_B, "ga": d, "gb": d}
    order = ("qkv", "z", "beta", "alpha", "q", "k", "v", "ga", "gb")
    offs, off = {}, 0
    for n in order:
        offs[n] = off
        off += widths[n]
    w_in = W["w_in"][l]
    col = lambda n: w_in[:, offs[n]:offs[n] + widths[n]]
    pad = jnp.zeros((d, LANES - 2 * H_A), w_in.dtype)
    w_r = jnp.concatenate([col("qkv"), col("z"), col("q"), col("k"), col("v"), col("ga"), col("gb"),
                           col("beta"), col("alpha"), pad], axis=1)
    groups = tuple((n, (d if w is None else w), dt) for n, w, dt in _PROJ_OUT)
    lam_params = jnp.stack([W["lam_q1"][l], W["lam_k1"][l], W["lam_q2"][l], W["lam_k2"][l]]).astype(F32)
    return dict(
        ffn1=(row(W["ffn1_pre"][l]), bf(W["ffn1_wg"][l]), bf(W["ffn1_wu"][l]), bf(W["ffn1_wd"][l]),
              row(W["ffn1_post"][l])),
        ffn2=(row(W["ffn2_pre"][l]), bf(W["ffn2_wg"][l]), bf(W["ffn2_wu"][l]), bf(W["ffn2_wd"][l]),
              row(W["ffn2_post"][l])),
        mix_pre=row(W["mix_pre"][l]), w_r=bf(w_r), groups=groups,
        conv_w=W["conv_w"][l].astype(F32), a_log=W["a_log"][l].astype(F32), dt_bias=W["dt_bias"][l].astype(F32),
        gdn_norm=W["gdn_norm"][l].astype(F32), lam_params=lam_params, subln=W["subln"][l].astype(F32),
        w_a=bf(W["w_a"][l]), w_b=bf(W["w_b"][l]), w_out=bf(W["w_out"][l]), mix_post=row(W["mix_post"][l]),
        lam_init=_lambda_init(l))


def _decoder_layer(x, lw, conv_state, ssm_state, attend, attn_tile=None):
    b, l, d = x.shape
    t = b * l
    tm = attn_tile or _token_tile(t)
    x2 = _ffn(x.reshape(t, d), *lw["ffn1"], tm=tm)
    c = _proj(x2, lw["mix_pre"], lw["w_r"], lw["groups"], tm=tm, attn_operands=attn_tile is not None)
    c3 = {n: c[n].reshape(b, l, c[n].shape[-1]) for n in ("qkv", "z", "ba")}

    chunk = min(CHUNK, l)
    if chunk % SUBLANES:
        chunk = -(-chunk // SUBLANES) * SUBLANES
    lp = -(-l // chunk) * chunk
    padl = lambda a: jnp.pad(a, ((0, 0), (0, lp - l), (0, 0))) if lp != l else a
    ya, ssm_new = _gdn(padl(c3["qkv"]), conv_state, padl(c3["z"]), padl(c3["ba"]), ssm_state,
                       lw["conv_w"], lw["a_log"], lw["dt_bias"], lw["gdn_norm"],
                       chunk=chunk, n_chunk=math.gcd(lp // chunk, GDN_CHUNKS_PER_STEP),
                       n_seq=math.gcd(b, GDN_SEQS_PER_STEP) if lp == chunk else 1,
                       l_valid=min(l - (lp - chunk), chunk))
    ya = ya[:, :l]
    if l >= CONV_W - 1:
        conv_new = c3["qkv"][:, l - (CONV_W - 1):]
    else:
        conv_new = jnp.concatenate([conv_state.astype(F32), c3["qkv"]], axis=1)[:, -(CONV_W - 1):]

    yb = attend(c, b, l)
    y = _merge_ffn(x2, ya.reshape(t, -1), yb.reshape(t, -1).astype(BF16), c["ga"], c["gb"],
                   lw["w_a"], lw["w_b"], lw["w_out"], lw["mix_post"], *lw["ffn2"], tm=tm)
    return (y.reshape(b, l, d), c["k"].reshape(b, l, H_B, 2 * DQK_B), c["v"].reshape(b, l, H_B, DV_B),
            conv_new, ssm_new)


def kernel(x_prompt, x_sample, cache_k, cache_v, state_conv, state_ssm, page_table, rel_table,
           ffn1_pre, ffn1_wg, ffn1_wu, ffn1_wd, ffn1_post, mix_pre, w_in, conv_w, a_log, dt_bias,
           gdn_norm, lam_q1, lam_k1, lam_q2, lam_k2, subln, w_a, w_b, w_out, mix_post,
           ffn2_pre, ffn2_wg, ffn2_wu, ffn2_wd, ffn2_post):
    W = dict(ffn1_pre=ffn1_pre, ffn1_wg=ffn1_wg, ffn1_wu=ffn1_wu, ffn1_wd=ffn1_wd, ffn1_post=ffn1_post,
             mix_pre=mix_pre, w_in=w_in, conv_w=conv_w, a_log=a_log, dt_bias=dt_bias, gdn_norm=gdn_norm,
             lam_q1=lam_q1, lam_k1=lam_k1, lam_q2=lam_q2, lam_k2=lam_k2, subln=subln, w_a=w_a, w_b=w_b,
             w_out=w_out, mix_post=mix_post, ffn2_pre=ffn2_pre, ffn2_wg=ffn2_wg, ffn2_wu=ffn2_wu,
             ffn2_wd=ffn2_wd, ffn2_post=ffn2_post)
    depth = w_in.shape[0]
    bp, seq, _ = x_prompt.shape
    n_pool = cache_k.shape[1]
    n_pages = page_table.shape[1]
    t_attn = _attn_tile(seq)
    pages = math.gcd(n_pages, SAMPLE_PAGES_PER_STEP)
    ck = cache_k.reshape(depth * n_pool, PAGE_SIZE * H_B, -1)
    cv = cache_v.reshape(depth * n_pool, PAGE_SIZE * H_B, -1)
    xp, xs = x_prompt, x_sample
    outs = [[] for _ in range(8)]
    for l in range(depth):
        lw = _layer_weights(l, W)

        def attend_prompt(c, b, n):
            return _attn_prompt(c["q"].reshape(b, n, -1), c["k16"].reshape(b, n, -1), c["vt"], rel_table,
                                lw["lam_params"], lw["subln"], t=t_attn, lam_init=lw["lam_init"])

        def attend_sample(c, b, n):
            return _attn_sample(c["q"].reshape(b, n, -1), c["k"], c["v"], ck, cv, l * n_pool, page_table,
                                rel_table, lw["lam_params"], lw["subln"], pages=pages, lam_init=lw["lam_init"])

        zero_conv = jnp.zeros((bp, CONV_W - 1, state_conv.shape[-1]), x_prompt.dtype)
        zero_ssm = jnp.zeros((bp,) + state_ssm.shape[2:], state_ssm.dtype)
        xp, k1, v1, c1, s1 = _decoder_layer(xp, lw, zero_conv, zero_ssm, attend_prompt, attn_tile=t_attn)
        xs, k2, v2, c2, s2 = _decoder_layer(xs, lw, state_conv[l], state_ssm[l], attend_sample)
        for o, a in zip(outs, (k1, v1, c1, s1, k2, v2, c2, s2)):
            o.append(a)
    return (xp, xs) + tuple(jnp.stack(o) for o in outs)
```

```python
import functools
import math

import jax
import jax.numpy as jnp
from jax import lax
from jax.experimental import pallas as pl
from jax.experimental.pallas import tpu as pltpu

F32 = jnp.float32
BF16 = jnp.bfloat16

H_A = 4
DK_A = 128
DV_A = 128
CONV_W = 4
CHUNK = 64
H_B = 4
DQK_B = 64
DV_B = 2 * DQK_B
PAGE_SIZE = 128
NUM_BUCKETS = 32
MAX_DISTANCE = 128
EPS = 1e-6
NEG_INF = -1e30

LANES = 128
SUBLANES = 8
V7X_VMEM_BYTES = 64 * 1024 * 1024
VMEM_LIMIT = V7X_VMEM_BYTES - 8 * 1024 * 1024

LOG2E = math.log2(math.e)
QK_LOGIT_SCALE = DQK_B ** -0.5 * LOG2E

TOKEN_TILE = 512
FFN_CHUNK = 256
GDN_CHUNKS_PER_STEP = 8
GDN_SEQS_PER_STEP = 8
ATTN_BLOCK = 512
ATTN_COL_BLOCK = 512
ATTN_MXU_TILE = 256
ATTN_LOOKAHEAD = 2
ATTN_BLOCKS_PER_TRIP = 12
SAMPLE_PAGES_PER_STEP = 32
SAMPLE_PAGE_GROUP = 4


def _dot(a, b):
    return jnp.dot(a, b, preferred_element_type=F32)


def _dot_nt(a, b):
    return lax.dot_general(a, b, (((1,), (1,)), ((), ())), preferred_element_type=F32)


def _rms(x, g):
    return x * lax.rsqrt(jnp.mean(x * x, axis=-1, keepdims=True) + EPS) * g


def _silu(x):
    return x * jax.nn.sigmoid(x)


def _const_spec(shape):
    nd = len(shape)
    return pl.BlockSpec(shape, lambda *_: (0,) * nd, pipeline_mode=pl.Buffered(1))


def _params(sem):
    return pltpu.CompilerParams(dimension_semantics=sem, vmem_limit_bytes=VMEM_LIMIT)


def _ffn_half_step(x, pre_ref, wg_ref, wu_ref, wd_ref, post_ref, f_chunk):
    h = _rms(x, pre_ref[...]).astype(BF16)
    d_ff = wg_ref.shape[1]
    acc = jnp.zeros(x.shape, F32)
    for c in range(d_ff // f_chunk):
        sl = slice(c * f_chunk, (c + 1) * f_chunk)
        g = _dot(h, wg_ref[:, sl])
        u = _dot(h, wu_ref[:, sl])
        acc = acc + _dot((_silu(g) * u).astype(BF16), wd_ref[sl, :])
    return x + 0.5 * _rms(acc, post_ref[...])


def _ffn_body(x_ref, pre_ref, wg_ref, wu_ref, wd_ref, post_ref, o_ref, *, f_chunk):
    o_ref[...] = _ffn_half_step(x_ref[...], pre_ref, wg_ref, wu_ref, wd_ref, post_ref, f_chunk)


def _ffn(x, pre, wg, wu, wd, post, *, tm):
    t, d = x.shape
    d_ff = wg.shape[1]
    f_chunk = FFN_CHUNK if d_ff % FFN_CHUNK == 0 else d_ff
    return pl.pallas_call(
        functools.partial(_ffn_body, f_chunk=f_chunk),
        grid=(t // tm,),
        in_specs=[pl.BlockSpec((tm, d), lambda i: (i, 0)),
                  _const_spec((1, d)), _const_spec((d, d_ff)), _const_spec((d, d_ff)),
                  _const_spec((d_ff, d)), _const_spec((1, d))],
        out_specs=pl.BlockSpec((tm, d), lambda i: (i, 0)),
        out_shape=jax.ShapeDtypeStruct((t, d), F32),
        compiler_params=_params(("parallel",)),
        name="ffn",
    )(x, pre, wg, wu, wd, post)


_PROJ_OUT = (("qkv", H_A * (2 * DK_A + DV_A), F32), ("z", H_A * DV_A, F32),
             ("q", H_B * 2 * DQK_B, BF16), ("k", H_B * 2 * DQK_B, F32), ("v", H_B * DV_B, F32),
             ("ga", None, F32), ("gb", None, F32), ("ba", LANES, F32))


def _proj_body(x_ref, g_ref, w_ref, *o_refs, groups, names):
    h = _rms(x_ref[...], g_ref[...]).astype(BF16)
    refs = dict(zip(names, o_refs))
    off = 0
    for name, w, _ in groups:
        y = _dot(h, w_ref[:, off:off + w])
        if name == "q":
            y = y * QK_LOGIT_SCALE
        if name in ("k", "v"):
            hd = w // H_B
            for hh in range(H_B):
                refs[name][:, hh, :] = y[:, hh * hd:(hh + 1) * hd]
            if name == "k" and "k16" in refs:
                refs["k16"][...] = y.astype(BF16)
            if name == "v" and "vt" in refs:
                for hh in range(H_B):
                    refs["vt"][hh, 0] = y[:, hh * hd:(hh + 1) * hd].T.astype(BF16)
        else:
            refs[name][...] = y.astype(refs[name].dtype)
        off += w


def _proj(x, g, w_r, groups, *, tm, attn_operands):
    t, d = x.shape
    row = lambda w: pl.BlockSpec((tm, w), lambda i: (i, 0))
    names, specs, shapes = [], [], []
    for n, w, dt in groups:
        names.append(n)
        if n in ("k", "v"):
            specs.append(pl.BlockSpec((tm, H_B, w // H_B), lambda i: (i, 0, 0)))
            shapes.append(jax.ShapeDtypeStruct((t, H_B, w // H_B), dt))
        else:
            specs.append(row(w))
            shapes.append(jax.ShapeDtypeStruct((t, w), dt))
    if attn_operands:
        wk, wv = groups[3][1], groups[4][1]
        names += ["k16", "vt"]
        specs += [row(wk), pl.BlockSpec((H_B, 1, wv // H_B, tm), lambda i: (0, i, 0, 0))]
        shapes += [jax.ShapeDtypeStruct((t, wk), BF16), jax.ShapeDtypeStruct((H_B, t // tm, wv // H_B, tm), BF16)]
    res = pl.pallas_call(
        functools.partial(_proj_body, groups=groups, names=names),
        grid=(t // tm,),
        in_specs=[row(d), _const_spec((1, d)), _const_spec(w_r.shape)],
        out_specs=specs,
        out_shape=shapes,
        compiler_params=_params(("parallel",)),
        name="proj",
    )(x, g, w_r)
    return dict(zip(names, res))


def _split2(x):
    hi = x.astype(BF16)
    return hi, (x - hi.astype(F32)).astype(BF16)


def _split3(x):
    hi = x.astype(BF16)
    r = x - hi.astype(F32)
    mid = r.astype(BF16)
    return hi, mid, (r - mid.astype(F32)).astype(BF16)


def _mm3(x, y):
    x_hi, x_lo = _split2(x)
    y_hi, y_lo = _split2(y)
    return _dot(x_hi, y_hi) + _dot(x_lo, y_hi) + _dot(x_hi, y_lo)


def _neumann_level(p, t_inv, c):
    if c % 16 == 0:
        x_hi, x_lo = _split2(jnp.concatenate([p, t_inv], axis=0))
        p_hi, p_lo = x_hi[:c], x_lo[:c]
        y = _dot(jnp.concatenate([x_hi, x_lo], axis=0), p_hi)
        y = y[:2 * c] + y[2 * c:] + _dot(x_hi, p_lo)
        return y[:c], t_inv + y[c:]
    p_hi, p_lo = _split2(p)
    t_hi, t_lo = _split2(t_inv)
    pp = _dot(p_hi, p_hi) + _dot(p_lo, p_hi) + _dot(p_hi, p_lo)
    tp = _dot(t_hi, p_hi) + _dot(t_lo, p_hi) + _dot(t_hi, p_lo)
    return pp, t_inv + tp


def _gdn_body(qkv_ref, cst_ref, z_ref, ba_ref, bat_ref, s0_ref, cw_ref, prow_ref, pcol_ref, gn_ref,
              ya_ref, s_ref, ext_ref, *, chunk, n_chunk, n_seq, l_valid):
    c = chunk
    tb = n_chunk * c
    j = pl.program_id(1)

    @pl.when(j == 0)
    def _():
        ext_ref[:, 0:SUBLANES, :] = cst_ref[...]
        s_ref[...] = s0_ref[...]

    ii = lax.broadcasted_iota(jnp.int32, (c, c), 0)
    jj = lax.broadcasted_iota(jnp.int32, (c, c), 1)
    causal = ii >= jj
    strict = ii > jj
    ltri16 = causal.astype(BF16)
    eye = (ii == jj).astype(F32)
    cw = cw_ref[...]
    hk = H_A * DK_A

    rows = lambda n: slice(n * c, (n + 1) * c)
    act, beta_c, g_c = [], [], []
    for s in range(n_seq):
        u_raw = qkv_ref[s]
        ext_ref[s, SUBLANES:SUBLANES + tb, :] = u_raw
        conv = u_raw * cw[CONV_W - 1:CONV_W, :]
        for k in range(1, CONV_W):
            conv = conv + ext_ref[s, SUBLANES - k:SUBLANES - k + tb, :] * cw[CONV_W - 1 - k:CONV_W - k, :]
        ext_ref[s, 0:SUBLANES, :] = ext_ref[s, tb:tb + SUBLANES, :]
        act.append(_silu(conv))
        ba = ba_ref[s]
        beta_s = jax.nn.sigmoid(ba)
        g_s = -jnp.exp(prow_ref[0:1, :]) * jax.nn.softplus(ba + prow_ref[1:2, :])
        if l_valid < c:
            valid = lax.broadcasted_iota(jnp.int32, g_s.shape, 0) < l_valid
            g_s = jnp.where(valid, g_s, 0.0)
            beta_s = jnp.where(valid, beta_s, 0.0)
        beta_c.append(beta_s)
        g_c.append(g_s)

    blocks = [(s, n) for s in range(n_seq) for n in range(n_chunk)]
    chains = [(s, n, h) for s, n in blocks for h in range(H_A)]

    gcum_c, gcum_r = {}, {}
    for s, n in blocks:
        g_r = -jnp.exp(pcol_ref[0][:, :c]) * jax.nn.softplus(bat_ref[s, n] + pcol_ref[1][:, :c])
        if l_valid < c:
            g_r = jnp.where(lax.broadcasted_iota(jnp.int32, g_r.shape, 1) < l_valid, g_r, 0.0)
        gcum_c[s, n] = functools.reduce(jnp.add, [_dot(ltri16, x) for x in _split3(g_c[s][rows(n)])])
        gcum_r[s, n] = functools.reduce(jnp.add, [_dot_nt(x, ltri16) for x in _split3(g_r)])

    st = {}
    for ch in chains:
        s, n, h = ch
        rs = rows(n)
        q_raw = act[s][rs, h * DK_A:(h + 1) * DK_A]
        k_raw = act[s][rs, hk + h * DK_A:hk + (h + 1) * DK_A]
        v = act[s][rs, 2 * hk + h * DV_A:2 * hk + (h + 1) * DV_A]
        q = q_raw * lax.rsqrt(jnp.sum(q_raw * q_raw, axis=-1, keepdims=True) + EPS) * (DK_A ** -0.5)
        k = k_raw * lax.rsqrt(jnp.sum(k_raw * k_raw, axis=-1, keepdims=True) + EPS)
        beta = beta_c[s][rs, h:h + 1]
        gc = gcum_c[s, n][:, H_A + h:H_A + h + 1]
        gr = gcum_r[s, n][H_A + h:H_A + h + 1, :]
        decay = jnp.where(causal, jnp.exp(jnp.where(causal, gc - gr, 0.0)), 0.0)
        kb = k * beta
        kq = _dot_nt(jnp.concatenate([kb, q], axis=0).astype(BF16), k.astype(BF16))
        eg = jnp.exp(gc)
        g_last = gc[c - 1:c, :]
        p = -jnp.where(strict, kq[:c] * decay, 0.0)
        st[ch] = dict(p=p, t=eye + p, qk=(kq[c:] * decay).astype(BF16),
                      rhs=jnp.concatenate([v * beta, kb * eg], axis=1).astype(BF16), qe=q * eg,
                      k_dec=(k * jnp.exp(g_last - gc)).astype(BF16), dec=jnp.exp(g_last))

    n_factor = c.bit_length() - 1
    if n_factor >= 2:
        for e in st.values():
            e["p"] = _mm3(e["p"], e["p"])
        for _ in range(n_factor - 2):
            for e in st.values():
                e["p"], e["t"] = _neumann_level(e["p"], e["t"], c)
        for e in st.values():
            e["t"] = e["t"] + _mm3(e["t"], e["p"])
    for e in st.values():
        uw = _dot(e["t"].astype(BF16), e["rhs"])
        e["u"] = uw[:, :DV_A]
        e["wq"] = jnp.concatenate([uw[:, DV_A:], e["qe"]], axis=0).astype(BF16)

    state = {(s, h): s_ref[s, h] for s in range(n_seq) for h in range(H_A)}
    for n in range(n_chunk):
        grp = [(s, h) for s in range(n_seq) for h in range(H_A)]
        ws = {g: _dot(st[g[0], n, g[1]]["wq"], state[g].astype(BF16)) for g in grp}
        v_new = {g: (st[g[0], n, g[1]]["u"] - ws[g][:c]).astype(BF16) for g in grp}
        for g in grp:
            s, h = g
            e = st[s, n, h]
            o = ws[g][c:] + _dot(e["qk"], v_new[g])
            state[g] = state[g] * e["dec"] + lax.dot_general(
                e["k_dec"], v_new[g], (((0,), (0,)), ((), ())), preferred_element_type=F32)
            zh = z_ref[s, rows(n), h * DV_A:(h + 1) * DV_A]
            ya_ref[s, rows(n), h * DV_A:(h + 1) * DV_A] = (_rms(o, gn_ref[...]) * _silu(zh)).astype(ya_ref.dtype)
    for (s, h), val in state.items():
        s_ref[s, h] = val


def _gdn(qkv, conv_state, z, ba, s0, conv_w, a_log, dt_bias, g_norm, *, chunk, n_chunk, n_seq, l_valid):
    b, l, cc = qkv.shape
    n = l // chunk
    tb = n_chunk * chunk
    assert l_valid == chunk or n == 1
    assert l % tb == 0 and b % n_seq == 0
    assert chunk & (chunk - 1) == 0
    cst = jnp.pad(conv_state, ((0, 0), (SUBLANES - (CONV_W - 1), 0), (0, 0)))
    bat = ba[..., :2 * H_A].reshape(b, n, chunk, 2 * H_A).transpose(0, 1, 3, 2)
    zeros_h = jnp.zeros((H_A,), F32)
    lane_pad = jnp.zeros((LANES - 2 * H_A,), F32)
    prow = jnp.stack([jnp.concatenate([zeros_h, a_log, lane_pad]),
                      jnp.concatenate([zeros_h, dt_bias, lane_pad])])
    pcol = jnp.broadcast_to(prow[:, :2 * H_A, None], (2, 2 * H_A, LANES))
    return pl.pallas_call(
        functools.partial(_gdn_body, chunk=chunk, n_chunk=n_chunk, n_seq=n_seq, l_valid=l_valid),
        grid=(b // n_seq, l // tb),
        in_specs=[pl.BlockSpec((n_seq, tb, cc), lambda i, j: (i, j, 0)),
                  pl.BlockSpec((n_seq, SUBLANES, cc), lambda i, j: (i, 0, 0)),
                  pl.BlockSpec((n_seq, tb, H_A * DV_A), lambda i, j: (i, j, 0)),
                  pl.BlockSpec((n_seq, tb, LANES), lambda i, j: (i, j, 0)),
                  pl.BlockSpec((n_seq, n_chunk, 2 * H_A, chunk), lambda i, j: (i, j, 0, 0)),
                  pl.BlockSpec((n_seq, H_A, DK_A, DV_A), lambda i, j: (i, 0, 0, 0)),
                  _const_spec((CONV_W, cc)), _const_spec((2, LANES)),
                  _const_spec((2, 2 * H_A, LANES)), _const_spec((1, DV_A))],
        out_specs=[pl.BlockSpec((n_seq, tb, H_A * DV_A), lambda i, j: (i, j, 0)),
                   pl.BlockSpec((n_seq, H_A, DK_A, DV_A), lambda i, j: (i, 0, 0, 0))],
        out_shape=[jax.ShapeDtypeStruct((b, l, H_A * DV_A), BF16),
                   jax.ShapeDtypeStruct((b, H_A, DK_A, DV_A), F32)],
        scratch_shapes=[pltpu.VMEM((n_seq, SUBLANES + tb, cc), F32)],
        compiler_params=_params(("parallel", "arbitrary")),
        name="gdn",
    )(qkv, cst, z, ba, bat, s0, conv_w, prow, pcol, g_norm.reshape(1, DV_A))


def _rel_bucket(n):
    n = jnp.maximum(n, 0)
    max_exact = NUM_BUCKETS // 2
    large = max_exact + (jnp.log(jnp.maximum(n, 1).astype(F32) / max_exact)
                         / math.log(MAX_DISTANCE / max_exact) * (NUM_BUCKETS - max_exact)).astype(jnp.int32)
    large = jnp.minimum(large, NUM_BUCKETS - 1)
    return jnp.where(n < max_exact, n, large)


def _bias_rows(rel, tab_rows):
    onehot = _rel_bucket(rel)[..., None] == jnp.arange(NUM_BUCKETS)
    b = jnp.sum(jnp.where(onehot, tab_rows[..., None, :], 0.0), axis=-1)
    return jnp.where(rel >= 0, b, NEG_INF)


def _lam_of(lp_ref, lam_init):
    lp = lp_ref[...]
    s1 = jnp.sum(lp[0:1] * lp[1:2], axis=-1, keepdims=True)
    s2 = jnp.sum(lp[2:3] * lp[3:4], axis=-1, keepdims=True)
    return jnp.exp(s1) - jnp.exp(s2) + lam_init


def _attn_prompt_body(q_ref, k_ref, vt_ref, tiles_ref, lp_ref, sub_ref, o_ref,
                        qt_ref, m_ref, l_ref, acc_ref, s_ref, *, t, col_block, lam_init):
    i = pl.program_id(2)
    n2 = 2 * t
    qt = q_ref[0].astype(F32).T
    dim = lax.broadcasted_iota(jnp.int32, qt.shape, 0)
    qt_ref[:, 0:t] = jnp.where(dim < DQK_B, qt, 0.0).astype(BF16)
    qt_ref[:, t:n2] = jnp.where(dim >= DQK_B, qt, 0.0).astype(BF16)
    m_ref[...] = jnp.full(m_ref.shape, NEG_INF, F32)
    l_ref[...] = jnp.zeros(l_ref.shape, F32)
    acc_ref[...] = jnp.zeros(acc_ref.shape, F32)
    groups = t // SUBLANES

    def run(blocks):
        kv = []
        for j, _ in blocks:
            start = pl.multiple_of(j * t, t)
            kv.append((k_ref[0, pl.ds(start, t), :], vt_ref[0, j]))
        items = [(b, c0) for b in range(len(blocks)) for c0 in range(0, n2, col_block)]

        def qk(n):
            b, c0 = items[n]
            s = _dot(kv[b][0], qt_ref[:, c0:c0 + col_block])
            if blocks[b][1] is not None:
                s = s + tiles_ref[0, blocks[b][1], :, c0 % t:c0 % t + col_block]
            s_ref[n % (ATTN_LOOKAHEAD + 1)] = s
            return jnp.max(s.reshape(groups, SUBLANES, col_block), axis=0)

        part_max = {n: qk(n) for n in range(min(ATTN_LOOKAHEAD, len(items)))}
        mt = min(ATTN_MXU_TILE, t, col_block)
        for idx, it in enumerate(items):
            b, c0 = it
            cols = slice(c0, c0 + col_block)
            if idx + ATTN_LOOKAHEAD < len(items):
                part_max[idx + ATTN_LOOKAHEAD] = qk(idx + ATTN_LOOKAHEAD)
            s = s_ref.at[idx % (ATTN_LOOKAHEAD + 1)]
            m_prev = m_ref[:, cols]
            m_new = jnp.maximum(m_prev, jnp.max(part_max.pop(idx), axis=0, keepdims=True))
            alpha = jnp.exp2(m_prev - m_new)
            l_new, pv = [], []
            for n0 in range(0, col_block, mt):
                l_n = alpha[:, n0:n0 + mt] * l_ref[:, c0 + n0:c0 + n0 + mt]
                pv_n = None
                for k0 in range(0, t, mt):
                    p3 = jnp.exp2(s[k0:k0 + mt, n0:n0 + mt].reshape(mt // SUBLANES, SUBLANES, mt)
                                  - m_new[None, :, n0:n0 + mt])
                    l_n = l_n + jnp.sum(p3, axis=0)
                    d = _dot(kv[b][1][:, k0:k0 + mt], p3.reshape(mt, mt).astype(BF16))
                    pv_n = d if pv_n is None else pv_n + d
                l_new.append(l_n)
                pv.append(pv_n)
            l_ref[:, cols] = jnp.concatenate(l_new, axis=1)
            acc = acc_ref[:, cols].reshape(DV_B // SUBLANES, SUBLANES, col_block) * alpha[None]
            acc_ref[:, cols] = acc.reshape(DV_B, col_block) + jnp.concatenate(pv, axis=1)
            m_ref[:, cols] = m_new

    n_far = jnp.maximum(i - 1, 0)
    per_trip = ATTN_BLOCKS_PER_TRIP

    def far_group(jj, carry):
        run([(per_trip * jj + u, None) for u in range(per_trip)])
        return carry

    lax.fori_loop(0, n_far // per_trip, far_group, 0)
    rem = n_far % per_trip
    for r in range(per_trip):
        @pl.when((i >= 1) & (rem == r))
        def _():
            run([(n_far - r + u, None) for u in range(r)] + [(i - 1, 1), (i, 0)])

    @pl.when(i == 0)
    def _():
        run([(i, 0)])

    lam = _lam_of(lp_ref, lam_init)
    acc = acc_ref[...] / jnp.sum(l_ref[...], axis=0, keepdims=True)
    o = (acc[:, 0:t] - lam * acc[:, t:n2]).T
    o_ref[0] = (_rms(o, sub_ref[...]) * (1.0 - lam_init)).astype(o_ref.dtype)


def _attn_prompt(q, k, vt, rel_table, lam_params, subln, *, t, lam_init):
    b, l, _ = q.shape
    nq = l // t
    assert t >= MAX_DISTANCE
    assert vt.shape == (H_B, b * nq, DV_B, t)
    table = rel_table.astype(F32)
    dist = jnp.arange(-t + 1, 2 * t)[None, None, :]
    f = _bias_rows(jnp.broadcast_to(dist, (H_B, 1, 3 * t - 1)), table.T[:, None, :])[:, 0]
    f = (f - table[NUM_BUCKETS - 1][:, None]) * LOG2E

    def toeplitz(w):
        wp = jnp.pad(w, ((0, 0), (0, 1)))
        g = jnp.tile(wp, (1, t))[:, :t * (2 * t - 1)].reshape(H_B, t, 2 * t - 1)
        return g[:, :, t - 1:]

    tiles = jnp.stack([toeplitz(f[:, :2 * t - 1]), toeplitz(f[:, t:])], axis=1)
    hw = 2 * DQK_B
    return pl.pallas_call(
        functools.partial(_attn_prompt_body, t=t, col_block=min(ATTN_COL_BLOCK, t), lam_init=lam_init),
        grid=(b, H_B, nq),
        in_specs=[pl.BlockSpec((1, t, hw), lambda bi, h, i: (bi, i, h)),
                  pl.BlockSpec((1, l, hw), lambda bi, h, i: (bi, 0, h)),
                  pl.BlockSpec((1, nq, DV_B, t), lambda bi, h, i: (h, bi, 0, 0)),
                  pl.BlockSpec((1, 2, t, t), lambda bi, h, i: (h, 0, 0, 0)),
                  _const_spec(lam_params.shape), _const_spec((1, DV_B))],
        out_specs=pl.BlockSpec((1, t, DV_B), lambda bi, h, i: (bi, i, h)),
        out_shape=jax.ShapeDtypeStruct((b, l, H_B * DV_B), BF16),
        scratch_shapes=[pltpu.VMEM((hw, 2 * t), BF16), pltpu.VMEM((SUBLANES, 2 * t), F32),
                        pltpu.VMEM((SUBLANES, 2 * t), F32), pltpu.VMEM((DV_B, 2 * t), F32),
                        pltpu.VMEM((ATTN_LOOKAHEAD + 1, t, min(ATTN_COL_BLOCK, t)), F32)],
        compiler_params=_params(("parallel", "parallel", "arbitrary")),
        name="attn_prompt",
    )(q, k, vt, tiles, lam_params, subln.reshape(1, DV_B))


def _attn_sample_body(pt_ref, q_ref, kn_ref, vn_ref, bpast_ref, bself_ref, lp_ref, sub_ref, *rest,
                      pages, lam_init):
    k_refs = rest[:pages]
    v_refs = rest[pages:2 * pages]
    o_ref, qf_ref, m_ref, l_ref, acc_ref = rest[2 * pages:]
    j = pl.program_id(1)
    rows = qf_ref.shape[0]
    half = rows // 2
    page_cols = PAGE_SIZE * H_B

    @pl.when(j == 0)
    def _():
        q = q_ref[0].astype(F32)
        r = lax.broadcasted_iota(jnp.int32, q.shape, 0)
        lane = lax.broadcasted_iota(jnp.int32, q.shape, 1)
        qf = jnp.where(lane // DQK_B == r // half, q, 0.0)
        qf_ref[...] = qf
        kn = kn_ref[0]
        vn = vn_ref[0]
        n_self = kn.shape[0]
        s_self = [jnp.sum(qf * kn[c:c + 1], axis=-1, keepdims=True) + bself_ref[:, c:c + 1]
                  for c in range(n_self)]
        m0 = functools.reduce(jnp.maximum, s_self)
        p_self = [jnp.exp2(s - m0) for s in s_self]
        m_ref[...] = m0
        l_ref[...] = functools.reduce(jnp.add, p_self)
        acc_ref[...] = functools.reduce(jnp.add, [p * vn[c:c + 1] for c, p in enumerate(p_self)])

    q16 = qf_ref[...].astype(BF16)
    group = math.gcd(pages, SAMPLE_PAGE_GROUP)
    groups = [range(g, g + group) for g in range(0, pages, group)]
    qk = lambda grp: [_dot_nt(q16, k_refs[c][0].astype(BF16)) for c in grp]
    scores = qk(groups[0])
    m_run, l_run, acc = m_ref[...], l_ref[...], acc_ref[...]
    for gi, grp in enumerate(groups):
        cur = scores
        if gi + 1 < len(groups):
            scores = qk(groups[gi + 1])
        s = (jnp.concatenate(cur, axis=-1)
             + bpast_ref[j, :, grp[0] * page_cols:(grp[-1] + 1) * page_cols])
        m_new = jnp.maximum(m_run, jnp.max(s, axis=-1, keepdims=True))
        alpha = jnp.exp2(m_run - m_new)
        p = jnp.exp2(s - m_new)
        l_run = alpha * l_run + jnp.sum(p, axis=-1, keepdims=True)
        p16 = p.astype(BF16)
        pv = functools.reduce(jnp.add, [_dot(p16[:, i * page_cols:(i + 1) * page_cols], v_refs[c][0].astype(BF16))
                                        for i, c in enumerate(grp)])
        acc = alpha * acc + pv
        m_run = m_new
    m_ref[...], l_ref[...], acc_ref[...] = m_run, l_run, acc

    @pl.when(j == pl.num_programs(1) - 1)
    def _():
        lam = _lam_of(lp_ref, lam_init)
        an = acc_ref[...] / l_ref[...]
        o = an[0:half] - lam * an[half:rows]
        o_ref[0] = _rms(o, sub_ref[...]) * (1.0 - lam_init)


def _attn_sample(q, k_new, v_new, cache_k, cache_v, page_base, page_table, rel_table, lam_params, subln,
                 *, pages, lam_init):
    bd, l_new, width = q.shape
    n_pages = page_table.shape[1]
    past = n_pages * PAGE_SIZE
    half = l_new * H_B
    rows = 2 * half
    page_cols = PAGE_SIZE * H_B
    table = rel_table.astype(F32)
    t_of = (jnp.arange(rows) % half) // H_B
    h_of = jnp.arange(rows) % H_B
    tab_rows = table.T[h_of]
    far = table[NUM_BUCKETS - 1][h_of][:, None]
    own = h_of[:, None, None] == jnp.arange(H_B)[None, None, :]
    rel_past = past + t_of[:, None] - jnp.arange(past)[None, :]
    bpast = jnp.where(own, ((_bias_rows(rel_past, tab_rows) - far) * LOG2E)[:, :, None], NEG_INF)
    n_steps = n_pages // pages
    bpast = bpast.reshape(rows, n_steps, pages * page_cols).transpose(1, 0, 2)
    n_self = l_new * H_B
    t_key = jnp.arange(LANES) // H_B
    rel_self = jnp.where((jnp.arange(LANES)[None, :] < n_self) & (h_of[:, None] == jnp.arange(LANES)[None, :] % H_B),
                         t_of[:, None] - t_key[None, :], -1)
    bself = (_bias_rows(rel_self, tab_rows) - far) * LOG2E

    hd = width // H_B
    q_rows = q.reshape(bd, half, hd)
    q_rows = jnp.concatenate([q_rows, q_rows], axis=1)
    kn = k_new.reshape(bd, n_self, hd)
    vn = v_new.reshape(bd, n_self, hd)

    def page_spec(c):
        return pl.BlockSpec((1, page_cols, hd),
                            lambda b, j, pt: (page_base + pt[b * n_pages + j * pages + c], 0, 0))

    seq_spec = lambda r: pl.BlockSpec((1, r, hd), lambda b, j, pt: (b, 0, 0))
    whole = lambda shape: pl.BlockSpec(shape, lambda b, j, pt: (0,) * len(shape), pipeline_mode=pl.Buffered(1))
    grid_spec = pltpu.PrefetchScalarGridSpec(
        num_scalar_prefetch=1,
        grid=(bd, n_pages // pages),
        in_specs=[seq_spec(rows), seq_spec(n_self), seq_spec(n_self),
                  whole(bpast.shape), whole(bself.shape), whole(lam_params.shape), whole((1, DV_B))]
                 + [page_spec(c) for c in range(pages)] + [page_spec(c) for c in range(pages)],
        out_specs=seq_spec(half),
        scratch_shapes=[pltpu.VMEM((rows, hd), F32), pltpu.VMEM((rows, 1), F32),
                        pltpu.VMEM((rows, 1), F32), pltpu.VMEM((rows, DV_B), F32)])
    out = pl.pallas_call(
        functools.partial(_attn_sample_body, pages=pages, lam_init=lam_init),
        grid_spec=grid_spec,
        out_shape=jax.ShapeDtypeStruct((bd, half, DV_B), F32),
        compiler_params=_params(("parallel", "arbitrary")),
        name="attn_sample",
    )(page_table.reshape(-1), q_rows, kn, vn, bpast, bself, lam_params, subln.reshape(1, DV_B),
      *([cache_k] * pages), *([cache_v] * pages))
    return out.reshape(bd, l_new, width)


def _merge_ffn_body(x_ref, ya_ref, yb_ref, ga_ref, gb_ref, wa_ref, wb_ref, wo_ref, post_ref,
                    pre2_ref, wg_ref, wu_ref, wd_ref, post2_ref, o_ref, *, f_chunk):
    merged = (jax.nn.sigmoid(ga_ref[...]) * _dot(ya_ref[...], wa_ref[...])
              + jax.nn.sigmoid(gb_ref[...]) * _dot(yb_ref[...], wb_ref[...]))
    x = x_ref[...] + _rms(_dot(merged.astype(BF16), wo_ref[...]), post_ref[...])
    o_ref[...] = _ffn_half_step(x, pre2_ref, wg_ref, wu_ref, wd_ref, post2_ref, f_chunk)


def _merge_ffn(x, ya, yb, ga, gb, wa, wb, wo, post, pre2, wg, wu, wd, post2, *, tm):
    t, d = x.shape
    d_ff = wg.shape[1]
    f_chunk = FFN_CHUNK if d_ff % FFN_CHUNK == 0 else d_ff
    row = lambda w: pl.BlockSpec((tm, w), lambda i: (i, 0))
    weights = (wa, wb, wo, post, pre2, wg, wu, wd, post2)
    return pl.pallas_call(
        functools.partial(_merge_ffn_body, f_chunk=f_chunk),
        grid=(t // tm,),
        in_specs=[row(d), row(ya.shape[1]), row(yb.shape[1]), row(d), row(d)]
                 + [_const_spec(w.shape) for w in weights],
        out_specs=row(d),
        out_shape=jax.ShapeDtypeStruct((t, d), F32),
        compiler_params=_params(("parallel",)),
        name="merge_ffn",
    )(x, ya, yb, ga, gb, *weights)


def _lambda_init(layer):
    return 0.8 - 0.6 * math.exp(-0.3 * layer)


def _token_tile(t):
    tm = TOKEN_TILE
    while t % tm:
        tm //= 2
    return tm


def _attn_tile(seq):
    return min(ATTN_BLOCK, seq)


def _layer_weights(l, W):
    d = W["w_in"].shape[1]
    bf = lambda a: a.astype(BF16)
    row = lambda a: a.reshape(1, -1).astype(F32)
    widths = {"qkv": H_A * (2 * DK_A + DV_A), "z": H_A * DV_A, "beta": H_A, "alpha": H_A,
              "q": H_B * 2 * DQK_B, "k": H_B * 2 * DQK_B, "v": H_B * DV_B, "ga": d, "gb": d}
    order = ("qkv", "z", "beta", "alpha", "q", "k", "v", "ga", "gb")
    offs, off = {}, 0
    for n in order:
        offs[n] = off
        off += widths[n]
    w_in = W["w_in"][l]
    col = lambda n: w_in[:, offs[n]:offs[n] + widths[n]]
    pad = jnp.zeros((d, LANES - 2 * H_A), w_in.dtype)
    w_r = jnp.concatenate([col("qkv"), col("z"), col("q"), col("k"), col("v"), col("ga"), col("gb"),
                           col("beta"), col("alpha"), pad], axis=1)
    groups = tuple((n, (d if w is None else w), dt) for n, w, dt in _PROJ_OUT)
    lam_params = jnp.stack([W["lam_q1"][l], W["lam_k1"][l], W["lam_q2"][l], W["lam_k2"][l]]).astype(F32)
    return dict(
        ffn1=(row(W["ffn1_pre"][l]), bf(W["ffn1_wg"][l]), bf(W["ffn1_wu"][l]), bf(W["ffn1_wd"][l]),
              row(W["ffn1_post"][l])),
        ffn2=(row(W["ffn2_pre"][l]), bf(W["ffn2_wg"][l]), bf(W["ffn2_wu"][l]), bf(W["ffn2_wd"][l]),
              row(W["ffn2_post"][l])),
        mix_pre=row(W["mix_pre"][l]), w_r=bf(w_r), groups=groups,
        conv_w=W["conv_w"][l].astype(F32), a_log=W["a_log"][l].astype(F32), dt_bias=W["dt_bias"][l].astype(F32),
        gdn_norm=W["gdn_norm"][l].astype(F32), lam_params=lam_params, subln=W["subln"][l].astype(F32),
        w_a=bf(W["w_a"][l]), w_b=bf(W["w_b"][l]), w_out=bf(W["w_out"][l]), mix_post=row(W["mix_post"][l]),
        lam_init=_lambda_init(l))


def _decoder_layer(x, lw, conv_state, ssm_state, attend, attn_tile=None):
    b, l, d = x.shape
    t = b * l
    tm = attn_tile or _token_tile(t)
    x2 = _ffn(x.reshape(t, d), *lw["ffn1"], tm=tm)
    c = _proj(x2, lw["mix_pre"], lw["w_r"], lw["groups"], tm=tm, attn_operands=attn_tile is not None)
    c3 = {n: c[n].reshape(b, l, c[n].shape[-1]) for n in ("qkv", "z", "ba")}

    chunk = min(CHUNK, l)
    if chunk % SUBLANES:
        chunk = -(-chunk // SUBLANES) * SUBLANES
    lp = -(-l // chunk) * chunk
    padl = lambda a: jnp.pad(a, ((0, 0), (0, lp - l), (0, 0))) if lp != l else a
    ya, ssm_new = _gdn(padl(c3["qkv"]), conv_state, padl(c3["z"]), padl(c3["ba"]), ssm_state,
                       lw["conv_w"], lw["a_log"], lw["dt_bias"], lw["gdn_norm"],
                       chunk=chunk, n_chunk=math.gcd(lp // chunk, GDN_CHUNKS_PER_STEP),
                       n_seq=math.gcd(b, GDN_SEQS_PER_STEP) if lp == chunk else 1,
                       l_valid=min(l - (lp - chunk), chunk))
    ya = ya[:, :l]
    if l >= CONV_W - 1:
        conv_new = c3["qkv"][:, l - (CONV_W - 1):]
    else:
        conv_new = jnp.concatenate([conv_state.astype(F32), c3["qkv"]], axis=1)[:, -(CONV_W - 1):]

    yb = attend(c, b, l)
    y = _merge_ffn(x2, ya.reshape(t, -1), yb.reshape(t, -1).astype(BF16), c["ga"], c["gb"],
                   lw["w_a"], lw["w_b"], lw["w_out"], lw["mix_post"], *lw["ffn2"], tm=tm)
    return (y.reshape(b, l, d), c["k"].reshape(b, l, H_B, 2 * DQK_B), c["v"].reshape(b, l, H_B, DV_B),
            conv_new, ssm_new)


def kernel(x_prompt, x_sample, cache_k, cache_v, state_conv, state_ssm, page_table, rel_table,
           ffn1_pre, ffn1_wg, ffn1_wu, ffn1_wd, ffn1_post, mix_pre, w_in, conv_w, a_log, dt_bias,
           gdn_norm, lam_q1, lam_k1, lam_q2, lam_k2, subln, w_a, w_b, w_out, mix_post,
           ffn2_pre, ffn2_wg, ffn2_wu, ffn2_wd, ffn2_post):
    W = dict(ffn1_pre=ffn1_pre, ffn1_wg=ffn1_wg, ffn1_wu=ffn1_wu, ffn1_wd=ffn1_wd, ffn1_post=ffn1_post,
             mix_pre=mix_pre, w_in=w_in, conv_w=conv_w, a_log=a_log, dt_bias=dt_bias, gdn_norm=gdn_norm,
             lam_q1=lam_q1, lam_k1=lam_k1, lam_q2=lam_q2, lam_k2=lam_k2, subln=subln, w_a=w_a, w_b=w_b,
             w_out=w_out, mix_post=mix_post, ffn2_pre=ffn2_pre, ffn2_wg=ffn2_wg, ffn2_wu=ffn2_wu,
             ffn2_wd=ffn2_wd, ffn2_post=ffn2_post)
    depth = w_in.shape[0]
    bp, seq, _ = x_prompt.shape
    n_pool = cache_k.shape[1]
    n_pages = page_table.shape[1]
    t_attn = _attn_tile(seq)
    pages = math.gcd(n_pages, SAMPLE_PAGES_PER_STEP)
    ck = cache_k.reshape(depth * n_pool, PAGE_SIZE * H_B, -1)
    cv = cache_v.reshape(depth * n_pool, PAGE_SIZE * H_B, -1)
    xp, xs = x_prompt, x_sample
    outs = [[] for _ in range(8)]
    for l in range(depth):
        lw = _layer_weights(l, W)

        def attend_prompt(c, b, n):
            return _attn_prompt(c["q"].reshape(b, n, -1), c["k16"].reshape(b, n, -1), c["vt"], rel_table,
                                lw["lam_params"], lw["subln"], t=t_attn, lam_init=lw["lam_init"])

        def attend_sample(c, b, n):
            return _attn_sample(c["q"].reshape(b, n, -1), c["k"], c["v"], ck, cv, l * n_pool, page_table,
                                rel_table, lw["lam_params"], lw["subln"], pages=pages, lam_init=lw["lam_init"])

        zero_conv = jnp.zeros((bp, CONV_W - 1, state_conv.shape[-1]), x_prompt.dtype)
        zero_ssm = jnp.zeros((bp,) + state_ssm.shape[2:], state_ssm.dtype)
        xp, k1, v1, c1, s1 = _decoder_layer(xp, lw, zero_conv, zero_ssm, attend_prompt, attn_tile=t_attn)
        xs, k2, v2, c2, s2 = _decoder_layer(xs, lw, state_conv[l], state_ssm[l], attend_sample)
        for o, a in zip(outs, (k1, v1, c1, s1, k2, v2, c2, s2)):
            o.append(a)
    return (xp, xs) + tuple(jnp.stack(o) for o in outs)
```

```python
import functools
import math

import jax
import jax.numpy as jnp
from jax import lax
from jax.experimental import pallas as pl
from jax.experimental.pallas import tpu as pltpu

F32 = jnp.float32
BF16 = jnp.bfloat16

H_A = 4
DK_A = 128
DV_A = 128
CONV_W = 4
CHUNK = 64
H_B = 4
DQK_B = 64
DV_B = 2 * DQK_B
PAGE_SIZE = 128
NUM_BUCKETS = 32
MAX_DISTANCE = 128
EPS = 1e-6
NEG_INF = -1e30

LANES = 128
SUBLANES = 8
V7X_VMEM_BYTES = 64 * 1024 * 1024
VMEM_LIMIT = V7X_VMEM_BYTES - 8 * 1024 * 1024

LOG2E = math.log2(math.e)
QK_LOGIT_SCALE = DQK_B ** -0.5 * LOG2E

TOKEN_TILE = 512
FFN_CHUNK = 256
GDN_CHUNKS_PER_STEP = 8
GDN_SEQS_PER_STEP = 8
ATTN_BLOCK = 512
ATTN_COL_BLOCK = 512
ATTN_MXU_TILE = 256
ATTN_LOOKAHEAD = 2
ATTN_BLOCKS_PER_TRIP = 8
SAMPLE_PAGES_PER_STEP = 32
SAMPLE_PAGE_GROUP = 4


def _dot(a, b):
    return jnp.dot(a, b, preferred_element_type=F32)


def _dot_nt(a, b):
    return lax.dot_general(a, b, (((1,), (1,)), ((), ())), preferred_element_type=F32)


def _rms(x, g):
    return x * lax.rsqrt(jnp.mean(x * x, axis=-1, keepdims=True) + EPS) * g


def _silu(x):
    return x * jax.nn.sigmoid(x)


def _const_spec(shape):
    nd = len(shape)
    return pl.BlockSpec(shape, lambda *_: (0,) * nd, pipeline_mode=pl.Buffered(1))


def _params(sem):
    return pltpu.CompilerParams(dimension_semantics=sem, vmem_limit_bytes=VMEM_LIMIT)


def _ffn_half_step(x, pre_ref, wg_ref, wu_ref, wd_ref, post_ref, f_chunk):
    h = _rms(x, pre_ref[...]).astype(BF16)
    d_ff = wg_ref.shape[1]
    acc = jnp.zeros(x.shape, F32)
    for c in range(d_ff // f_chunk):
        sl = slice(c * f_chunk, (c + 1) * f_chunk)
        g = _dot(h, wg_ref[:, sl])
        u = _dot(h, wu_ref[:, sl])
        acc = acc + _dot((_silu(g) * u).astype(BF16), wd_ref[sl, :])
    return x + 0.5 * _rms(acc, post_ref[...])


def _ffn_body(x_ref, pre_ref, wg_ref, wu_ref, wd_ref, post_ref, o_ref, *, f_chunk):
    o_ref[...] = _ffn_half_step(x_ref[...], pre_ref, wg_ref, wu_ref, wd_ref, post_ref, f_chunk)


def _ffn(x, pre, wg, wu, wd, post, *, tm):
    t, d = x.shape
    d_ff = wg.shape[1]
    f_chunk = FFN_CHUNK if d_ff % FFN_CHUNK == 0 else d_ff
    return pl.pallas_call(
        functools.partial(_ffn_body, f_chunk=f_chunk),
        grid=(t // tm,),
        in_specs=[pl.BlockSpec((tm, d), lambda i: (i, 0)),
                  _const_spec((1, d)), _const_spec((d, d_ff)), _const_spec((d, d_ff)),
                  _const_spec((d_ff, d)), _const_spec((1, d))],
        out_specs=pl.BlockSpec((tm, d), lambda i: (i, 0)),
        out_shape=jax.ShapeDtypeStruct((t, d), F32),
        compiler_params=_params(("parallel",)),
        name="ffn",
    )(x, pre, wg, wu, wd, post)


_PROJ_OUT = (("qkv", H_A * (2 * DK_A + DV_A), F32), ("z", H_A * DV_A, BF16),
             ("q", H_B * 2 * DQK_B, BF16), ("k", H_B * 2 * DQK_B, F32), ("v", H_B * DV_B, F32),
             ("ga", None, BF16), ("gb", None, BF16), ("ba", LANES, F32))


def _proj_body(x_ref, g_ref, w_ref, *o_refs, groups, names):
    h = _rms(x_ref[...], g_ref[...]).astype(BF16)
    refs = dict(zip(names, o_refs))
    off = 0
    for name, w, _ in groups:
        y = _dot(h, w_ref[:, off:off + w])
        if name == "q":
            y = y * QK_LOGIT_SCALE
        if name in ("k", "v"):
            hd = w // H_B
            for hh in range(H_B):
                refs[name][:, hh, :] = y[:, hh * hd:(hh + 1) * hd]
            if name == "k" and "k16" in refs:
                refs["k16"][...] = y.astype(BF16)
            if name == "v" and "vt" in refs:
                for hh in range(H_B):
                    refs["vt"][hh, 0] = y[:, hh * hd:(hh + 1) * hd].T.astype(BF16)
        else:
            refs[name][...] = y.astype(refs[name].dtype)
        off += w


def _proj(x, g, w_r, groups, *, tm, attn_operands):
    t, d = x.shape
    row = lambda w: pl.BlockSpec((tm, w), lambda i: (i, 0))
    names, specs, shapes = [], [], []
    for n, w, dt in groups:
        names.append(n)
        if n in ("k", "v"):
            specs.append(pl.BlockSpec((tm, H_B, w // H_B), lambda i: (i, 0, 0)))
            shapes.append(jax.ShapeDtypeStruct((t, H_B, w // H_B), dt))
        else:
            specs.append(row(w))
            shapes.append(jax.ShapeDtypeStruct((t, w), dt))
    if attn_operands:
        wk, wv = groups[3][1], groups[4][1]
        names += ["k16", "vt"]
        specs += [row(wk), pl.BlockSpec((H_B, 1, wv // H_B, tm), lambda i: (0, i, 0, 0))]
        shapes += [jax.ShapeDtypeStruct((t, wk), BF16), jax.ShapeDtypeStruct((H_B, t // tm, wv // H_B, tm), BF16)]
    res = pl.pallas_call(
        functools.partial(_proj_body, groups=groups, names=names),
        grid=(t // tm,),
        in_specs=[row(d), _const_spec((1, d)), _const_spec(w_r.shape)],
        out_specs=specs,
        out_shape=shapes,
        compiler_params=_params(("parallel",)),
        name="proj",
    )(x, g, w_r)
    return dict(zip(names, res))


def _split2(x):
    hi = x.astype(BF16)
    return hi, (x - hi.astype(F32)).astype(BF16)


def _split3(x):
    hi = x.astype(BF16)
    r = x - hi.astype(F32)
    mid = r.astype(BF16)
    return hi, mid, (r - mid.astype(F32)).astype(BF16)


def _mm3(x, y):
    x_hi, x_lo = _split2(x)
    y_hi, y_lo = _split2(y)
    return _dot(x_hi, y_hi) + _dot(x_lo, y_hi) + _dot(x_hi, y_lo)


def _neumann_level(p, t_inv, c):
    if c % 16 == 0:
        x_hi, x_lo = _split2(jnp.concatenate([p, t_inv], axis=0))
        p_hi, p_lo = x_hi[:c], x_lo[:c]
        y = _dot(jnp.concatenate([x_hi, x_lo], axis=0), p_hi)
        y = y[:2 * c] + y[2 * c:] + _dot(x_hi, p_lo)
        return y[:c], t_inv + y[c:]
    p_hi, p_lo = _split2(p)
    t_hi, t_lo = _split2(t_inv)
    pp = _dot(p_hi, p_hi) + _dot(p_lo, p_hi) + _dot(p_hi, p_lo)
    tp = _dot(t_hi, p_hi) + _dot(t_lo, p_hi) + _dot(t_hi, p_lo)
    return pp, t_inv + tp


def _gdn_body(qkv_ref, cst_ref, z_ref, ba_ref, bat_ref, s0_ref, cw_ref, prow_ref, pcol_ref, gn_ref,
              ya_ref, s_ref, ext_ref, *, chunk, n_chunk, n_seq, l_valid):
    c = chunk
    tb = n_chunk * c
    j = pl.program_id(1)

    @pl.when(j == 0)
    def _():
        ext_ref[:, 0:SUBLANES, :] = cst_ref[...]
        s_ref[...] = s0_ref[...]

    ii = lax.broadcasted_iota(jnp.int32, (c, c), 0)
    jj = lax.broadcasted_iota(jnp.int32, (c, c), 1)
    causal = ii >= jj
    strict = ii > jj
    ltri16 = causal.astype(BF16)
    eye = (ii == jj).astype(F32)
    cw = cw_ref[...]
    hk = H_A * DK_A

    rows = lambda n: slice(n * c, (n + 1) * c)
    act, beta_c, g_c = [], [], []
    for s in range(n_seq):
        u_raw = qkv_ref[s]
        ext_ref[s, SUBLANES:SUBLANES + tb, :] = u_raw
        conv = u_raw * cw[CONV_W - 1:CONV_W, :]
        for k in range(1, CONV_W):
            conv = conv + ext_ref[s, SUBLANES - k:SUBLANES - k + tb, :] * cw[CONV_W - 1 - k:CONV_W - k, :]
        ext_ref[s, 0:SUBLANES, :] = ext_ref[s, tb:tb + SUBLANES, :]
        act.append(_silu(conv))
        ba = ba_ref[s]
        beta_s = jax.nn.sigmoid(ba)
        g_s = -jnp.exp(prow_ref[0:1, :]) * jax.nn.softplus(ba + prow_ref[1:2, :])
        if l_valid < c:
            valid = lax.broadcasted_iota(jnp.int32, g_s.shape, 0) < l_valid
            g_s = jnp.where(valid, g_s, 0.0)
            beta_s = jnp.where(valid, beta_s, 0.0)
        beta_c.append(beta_s)
        g_c.append(g_s)

    blocks = [(s, n) for s in range(n_seq) for n in range(n_chunk)]
    chains = [(s, n, h) for s, n in blocks for h in range(H_A)]

    gcum_c, gcum_r = {}, {}
    for s, n in blocks:
        g_r = -jnp.exp(pcol_ref[0][:, :c]) * jax.nn.softplus(bat_ref[s, n] + pcol_ref[1][:, :c])
        if l_valid < c:
            g_r = jnp.where(lax.broadcasted_iota(jnp.int32, g_r.shape, 1) < l_valid, g_r, 0.0)
        gcum_c[s, n] = functools.reduce(jnp.add, [_dot(ltri16, x) for x in _split3(g_c[s][rows(n)])])
        gcum_r[s, n] = functools.reduce(jnp.add, [_dot_nt(x, ltri16) for x in _split3(g_r)])

    st = {}
    for ch in chains:
        s, n, h = ch
        rs = rows(n)
        q_raw = act[s][rs, h * DK_A:(h + 1) * DK_A]
        k_raw = act[s][rs, hk + h * DK_A:hk + (h + 1) * DK_A]
        v = act[s][rs, 2 * hk + h * DV_A:2 * hk + (h + 1) * DV_A]
        q = q_raw * lax.rsqrt(jnp.sum(q_raw * q_raw, axis=-1, keepdims=True) + EPS) * (DK_A ** -0.5)
        k = k_raw * lax.rsqrt(jnp.sum(k_raw * k_raw, axis=-1, keepdims=True) + EPS)
        beta = beta_c[s][rs, h:h + 1]
        gc = gcum_c[s, n][:, H_A + h:H_A + h + 1]
        gr = gcum_r[s, n][H_A + h:H_A + h + 1, :]
        decay = jnp.where(causal, jnp.exp(jnp.where(causal, gc - gr, 0.0)), 0.0)
        kb = k * beta
        kq = _dot_nt(jnp.concatenate([kb, q], axis=0).astype(BF16), k.astype(BF16))
        eg = jnp.exp(gc)
        g_last = gc[c - 1:c, :]
        p = -jnp.where(strict, kq[:c] * decay, 0.0)
        st[ch] = dict(p=p, t=eye + p, qk=(kq[c:] * decay).astype(BF16),
                      rhs=jnp.concatenate([v * beta, kb * eg], axis=1).astype(BF16), qe=q * eg,
                      k_dec=(k * jnp.exp(g_last - gc)).astype(BF16), dec=jnp.exp(g_last))

    n_factor = c.bit_length() - 1
    if n_factor >= 2:
        for e in st.values():
            e["p"] = _mm3(e["p"], e["p"])
        for _ in range(n_factor - 2):
            for e in st.values():
                e["p"], e["t"] = _neumann_level(e["p"], e["t"], c)
        for e in st.values():
            e["t"] = e["t"] + _mm3(e["t"], e["p"])
    for e in st.values():
        uw = _dot(e["t"].astype(BF16), e["rhs"])
        e["u"] = uw[:, :DV_A]
        e["wq"] = jnp.concatenate([uw[:, DV_A:], e["qe"]], axis=0).astype(BF16)

    state = {(s, h): s_ref[s, h] for s in range(n_seq) for h in range(H_A)}
    for n in range(n_chunk):
        grp = [(s, h) for s in range(n_seq) for h in range(H_A)]
        ws = {g: _dot(st[g[0], n, g[1]]["wq"], state[g].astype(BF16)) for g in grp}
        v_new = {g: (st[g[0], n, g[1]]["u"] - ws[g][:c]).astype(BF16) for g in grp}
        for g in grp:
            s, h = g
            e = st[s, n, h]
            o = ws[g][c:] + _dot(e["qk"], v_new[g])
            state[g] = state[g] * e["dec"] + lax.dot_general(
                e["k_dec"], v_new[g], (((0,), (0,)), ((), ())), preferred_element_type=F32)
            zh = z_ref[s, rows(n), h * DV_A:(h + 1) * DV_A].astype(F32)
            ya_ref[s, rows(n), h * DV_A:(h + 1) * DV_A] = (_rms(o, gn_ref[...]) * _silu(zh)).astype(ya_ref.dtype)
    for (s, h), val in state.items():
        s_ref[s, h] = val


def _gdn(qkv, conv_state, z, ba, s0, conv_w, a_log, dt_bias, g_norm, *, chunk, n_chunk, n_seq, l_valid):
    b, l, cc = qkv.shape
    n = l // chunk
    tb = n_chunk * chunk
    assert l_valid == chunk or n == 1
    assert l % tb == 0 and b % n_seq == 0
    assert chunk & (chunk - 1) == 0
    cst = jnp.pad(conv_state, ((0, 0), (SUBLANES - (CONV_W - 1), 0), (0, 0)))
    bat = ba[..., :2 * H_A].reshape(b, n, chunk, 2 * H_A).transpose(0, 1, 3, 2)
    zeros_h = jnp.zeros((H_A,), F32)
    lane_pad = jnp.zeros((LANES - 2 * H_A,), F32)
    prow = jnp.stack([jnp.concatenate([zeros_h, a_log, lane_pad]),
                      jnp.concatenate([zeros_h, dt_bias, lane_pad])])
    pcol = jnp.broadcast_to(prow[:, :2 * H_A, None], (2, 2 * H_A, LANES))
    return pl.pallas_call(
        functools.partial(_gdn_body, chunk=chunk, n_chunk=n_chunk, n_seq=n_seq, l_valid=l_valid),
        grid=(b // n_seq, l // tb),
        in_specs=[pl.BlockSpec((n_seq, tb, cc), lambda i, j: (i, j, 0)),
                  pl.BlockSpec((n_seq, SUBLANES, cc), lambda i, j: (i, 0, 0)),
                  pl.BlockSpec((n_seq, tb, H_A * DV_A), lambda i, j: (i, j, 0)),
                  pl.BlockSpec((n_seq, tb, LANES), lambda i, j: (i, j, 0)),
                  pl.BlockSpec((n_seq, n_chunk, 2 * H_A, chunk), lambda i, j: (i, j, 0, 0)),
                  pl.BlockSpec((n_seq, H_A, DK_A, DV_A), lambda i, j: (i, 0, 0, 0)),
                  _const_spec((CONV_W, cc)), _const_spec((2, LANES)),
                  _const_spec((2, 2 * H_A, LANES)), _const_spec((1, DV_A))],
        out_specs=[pl.BlockSpec((n_seq, tb, H_A * DV_A), lambda i, j: (i, j, 0)),
                   pl.BlockSpec((n_seq, H_A, DK_A, DV_A), lambda i, j: (i, 0, 0, 0))],
        out_shape=[jax.ShapeDtypeStruct((b, l, H_A * DV_A), BF16),
                   jax.ShapeDtypeStruct((b, H_A, DK_A, DV_A), F32)],
        scratch_shapes=[pltpu.VMEM((n_seq, SUBLANES + tb, cc), F32)],
        compiler_params=_params(("parallel", "arbitrary")),
        name="gdn",
    )(qkv, cst, z, ba, bat, s0, conv_w, prow, pcol, g_norm.reshape(1, DV_A))


def _rel_bucket(n):
    n = jnp.maximum(n, 0)
    max_exact = NUM_BUCKETS // 2
    large = max_exact + (jnp.log(jnp.maximum(n, 1).astype(F32) / max_exact)
                         / math.log(MAX_DISTANCE / max_exact) * (NUM_BUCKETS - max_exact)).astype(jnp.int32)
    large = jnp.minimum(large, NUM_BUCKETS - 1)
    return jnp.where(n < max_exact, n, large)


def _bias_rows(rel, tab_rows):
    onehot = _rel_bucket(rel)[..., None] == jnp.arange(NUM_BUCKETS)
    b = jnp.sum(jnp.where(onehot, tab_rows[..., None, :], 0.0), axis=-1)
    return jnp.where(rel >= 0, b, NEG_INF)


def _lam_of(lp_ref, lam_init):
    lp = lp_ref[...]
    s1 = jnp.sum(lp[0:1] * lp[1:2], axis=-1, keepdims=True)
    s2 = jnp.sum(lp[2:3] * lp[3:4], axis=-1, keepdims=True)
    return jnp.exp(s1) - jnp.exp(s2) + lam_init


def _attn_prompt_body(q_ref, k_ref, vt_ref, tiles_ref, lp_ref, sub_ref, o_ref,
                        qt_ref, m_ref, l_ref, acc_ref, s_ref, *, t, col_block, lam_init):
    i = pl.program_id(2)
    n2 = 2 * t
    qt = q_ref[0].astype(F32).T
    dim = lax.broadcasted_iota(jnp.int32, qt.shape, 0)
    qt_ref[:, 0:t] = jnp.where(dim < DQK_B, qt, 0.0).astype(BF16)
    qt_ref[:, t:n2] = jnp.where(dim >= DQK_B, qt, 0.0).astype(BF16)
    m_ref[...] = jnp.full(m_ref.shape, NEG_INF, F32)
    l_ref[...] = jnp.zeros(l_ref.shape, F32)
    acc_ref[...] = jnp.zeros(acc_ref.shape, F32)
    groups = t // SUBLANES

    def run(blocks):
        kv = []
        for j, _ in blocks:
            start = pl.multiple_of(j * t, t)
            kv.append((k_ref[0, pl.ds(start, t), :], vt_ref[0, j]))
        items = [(b, c0) for b in range(len(blocks)) for c0 in range(0, n2, col_block)]

        def qk(n):
            b, c0 = items[n]
            s = _dot(kv[b][0], qt_ref[:, c0:c0 + col_block])
            if blocks[b][1] is not None:
                s = s + tiles_ref[0, blocks[b][1], :, c0 % t:c0 % t + col_block]
            s_ref[n % (ATTN_LOOKAHEAD + 1)] = s
            return jnp.max(s.reshape(groups, SUBLANES, col_block), axis=0)

        part_max = {n: qk(n) for n in range(min(ATTN_LOOKAHEAD, len(items)))}
        mt = min(ATTN_MXU_TILE, t, col_block)
        for idx, it in enumerate(items):
            b, c0 = it
            cols = slice(c0, c0 + col_block)
            if idx + ATTN_LOOKAHEAD < len(items):
                part_max[idx + ATTN_LOOKAHEAD] = qk(idx + ATTN_LOOKAHEAD)
            s = s_ref.at[idx % (ATTN_LOOKAHEAD + 1)]
            m_prev = m_ref[:, cols]
            m_new = jnp.maximum(m_prev, jnp.max(part_max.pop(idx), axis=0, keepdims=True))
            alpha = jnp.exp2(m_prev - m_new)
            l_new, pv = [], []
            for n0 in range(0, col_block, mt):
                l_n = alpha[:, n0:n0 + mt] * l_ref[:, c0 + n0:c0 + n0 + mt]
                pv_n = None
                for k0 in range(0, t, mt):
                    p3 = jnp.exp2(s[k0:k0 + mt, n0:n0 + mt].reshape(mt // SUBLANES, SUBLANES, mt)
                                  - m_new[None, :, n0:n0 + mt])
                    l_n = l_n + jnp.sum(p3, axis=0)
                    d = _dot(kv[b][1][:, k0:k0 + mt], p3.reshape(mt, mt).astype(BF16))
                    pv_n = d if pv_n is None else pv_n + d
                l_new.append(l_n)
                pv.append(pv_n)
            l_ref[:, cols] = jnp.concatenate(l_new, axis=1)
            acc = acc_ref[:, cols].reshape(DV_B // SUBLANES, SUBLANES, col_block) * alpha[None]
            acc_ref[:, cols] = acc.reshape(DV_B, col_block) + jnp.concatenate(pv, axis=1)
            m_ref[:, cols] = m_new

    n_far = jnp.maximum(i - 1, 0)
    per_trip = ATTN_BLOCKS_PER_TRIP

    def far_group(jj, carry):
        run([(per_trip * jj + u, None) for u in range(per_trip)])
        return carry

    lax.fori_loop(0, n_far // per_trip, far_group, 0)
    rem = n_far % per_trip
    for r in range(per_trip):
        @pl.when((i >= 1) & (rem == r))
        def _():
            run([(n_far - r + u, None) for u in range(r)] + [(i - 1, 1), (i, 0)])

    @pl.when(i == 0)
    def _():
        run([(i, 0)])

    lam = _lam_of(lp_ref, lam_init)
    acc = acc_ref[...] / jnp.sum(l_ref[...], axis=0, keepdims=True)
    o = (acc[:, 0:t] - lam * acc[:, t:n2]).T
    o_ref[0] = (_rms(o, sub_ref[...]) * (1.0 - lam_init)).astype(o_ref.dtype)


def _attn_prompt(q, k, vt, rel_table, lam_params, subln, *, t, lam_init):
    b, l, _ = q.shape
    nq = l // t
    assert t >= MAX_DISTANCE
    assert vt.shape == (H_B, b * nq, DV_B, t)
    table = rel_table.astype(F32)
    dist = jnp.arange(-t + 1, 2 * t)[None, None, :]
    f = _bias_rows(jnp.broadcast_to(dist, (H_B, 1, 3 * t - 1)), table.T[:, None, :])[:, 0]
    f = (f - table[NUM_BUCKETS - 1][:, None]) * LOG2E

    def toeplitz(w):
        wp = jnp.pad(w, ((0, 0), (0, 1)))
        g = jnp.tile(wp, (1, t))[:, :t * (2 * t - 1)].reshape(H_B, t, 2 * t - 1)
        return g[:, :, t - 1:]

    tiles = jnp.stack([toeplitz(f[:, :2 * t - 1]), toeplitz(f[:, t:])], axis=1)
    hw = 2 * DQK_B
    return pl.pallas_call(
        functools.partial(_attn_prompt_body, t=t, col_block=min(ATTN_COL_BLOCK, t), lam_init=lam_init),
        grid=(b, H_B, nq),
        in_specs=[pl.BlockSpec((1, t, hw), lambda bi, h, i: (bi, i, h)),
                  pl.BlockSpec((1, l, hw), lambda bi, h, i: (bi, 0, h)),
                  pl.BlockSpec((1, nq, DV_B, t), lambda bi, h, i: (h, bi, 0, 0)),
                  pl.BlockSpec((1, 2, t, t), lambda bi, h, i: (h, 0, 0, 0)),
                  _const_spec(lam_params.shape), _const_spec((1, DV_B))],
        out_specs=pl.BlockSpec((1, t, DV_B), lambda bi, h, i: (bi, i, h)),
        out_shape=jax.ShapeDtypeStruct((b, l, H_B * DV_B), BF16),
        scratch_shapes=[pltpu.VMEM((hw, 2 * t), BF16), pltpu.VMEM((SUBLANES, 2 * t), F32),
                        pltpu.VMEM((SUBLANES, 2 * t), F32), pltpu.VMEM((DV_B, 2 * t), F32),
                        pltpu.VMEM((ATTN_LOOKAHEAD + 1, t, min(ATTN_COL_BLOCK, t)), F32)],
        compiler_params=_params(("parallel", "parallel", "arbitrary")),
        name="attn_prompt",
    )(q, k, vt, tiles, lam_params, subln.reshape(1, DV_B))


def _attn_sample_body(pt_ref, q_ref, kn_ref, vn_ref, bpast_ref, bself_ref, lp_ref, sub_ref, *rest,
                      pages, lam_init):
    k_refs = rest[:pages]
    v_refs = rest[pages:2 * pages]
    o_ref, qf_ref, m_ref, l_ref, acc_ref = rest[2 * pages:]
    j = pl.program_id(1)
    rows = qf_ref.shape[0]
    half = rows // 2
    page_cols = PAGE_SIZE * H_B

    @pl.when(j == 0)
    def _():
        q = q_ref[0].astype(F32)
        r = lax.broadcasted_iota(jnp.int32, q.shape, 0)
        lane = lax.broadcasted_iota(jnp.int32, q.shape, 1)
        qf = jnp.where(lane // DQK_B == r // half, q, 0.0)
        qf_ref[...] = qf
        kn = kn_ref[0]
        vn = vn_ref[0]
        n_self = kn.shape[0]
        s_self = [jnp.sum(qf * kn[c:c + 1], axis=-1, keepdims=True) + bself_ref[:, c:c + 1]
                  for c in range(n_self)]
        m0 = functools.reduce(jnp.maximum, s_self)
        p_self = [jnp.exp2(s - m0) for s in s_self]
        m_ref[...] = m0
        l_ref[...] = functools.reduce(jnp.add, p_self)
        acc_ref[...] = functools.reduce(jnp.add, [p * vn[c:c + 1] for c, p in enumerate(p_self)])

    q16 = qf_ref[...].astype(BF16)
    group = math.gcd(pages, SAMPLE_PAGE_GROUP)
    groups = [range(g, g + group) for g in range(0, pages, group)]
    qk = lambda grp: [_dot_nt(q16, k_refs[c][0].astype(BF16)) for c in grp]
    scores = qk(groups[0])
    m_run, l_run, acc = m_ref[...], l_ref[...], acc_ref[...]
    for gi, grp in enumerate(groups):
        cur = scores
        if gi + 1 < len(groups):
            scores = qk(groups[gi + 1])
        s = (jnp.concatenate(cur, axis=-1)
             + bpast_ref[j, :, grp[0] * page_cols:(grp[-1] + 1) * page_cols])
        m_new = jnp.maximum(m_run, jnp.max(s, axis=-1, keepdims=True))
        alpha = jnp.exp2(m_run - m_new)
        p = jnp.exp2(s - m_new)
        l_run = alpha * l_run + jnp.sum(p, axis=-1, keepdims=True)
        p16 = p.astype(BF16)
        pv = functools.reduce(jnp.add, [_dot(p16[:, i * page_cols:(i + 1) * page_cols], v_refs[c][0].astype(BF16))
                                        for i, c in enumerate(grp)])
        acc = alpha * acc + pv
        m_run = m_new
    m_ref[...], l_ref[...], acc_ref[...] = m_run, l_run, acc

    @pl.when(j == pl.num_programs(1) - 1)
    def _():
        lam = _lam_of(lp_ref, lam_init)
        an = acc_ref[...] / l_ref[...]
        o = an[0:half] - lam * an[half:rows]
        o_ref[0] = _rms(o, sub_ref[...]) * (1.0 - lam_init)


def _attn_sample(q, k_new, v_new, cache_k, cache_v, page_base, page_table, rel_table, lam_params, subln,
                 *, pages, lam_init):
    bd, l_new, width = q.shape
    n_pages = page_table.shape[1]
    past = n_pages * PAGE_SIZE
    half = l_new * H_B
    rows = 2 * half
    page_cols = PAGE_SIZE * H_B
    table = rel_table.astype(F32)
    t_of = (jnp.arange(rows) % half) // H_B
    h_of = jnp.arange(rows) % H_B
    tab_rows = table.T[h_of]
    far = table[NUM_BUCKETS - 1][h_of][:, None]
    own = h_of[:, None, None] == jnp.arange(H_B)[None, None, :]
    rel_past = past + t_of[:, None] - jnp.arange(past)[None, :]
    bpast = jnp.where(own, ((_bias_rows(rel_past, tab_rows) - far) * LOG2E)[:, :, None], NEG_INF)
    n_steps = n_pages // pages
    bpast = bpast.reshape(rows, n_steps, pages * page_cols).transpose(1, 0, 2)
    n_self = l_new * H_B
    t_key = jnp.arange(LANES) // H_B
    rel_self = jnp.where((jnp.arange(LANES)[None, :] < n_self) & (h_of[:, None] == jnp.arange(LANES)[None, :] % H_B),
                         t_of[:, None] - t_key[None, :], -1)
    bself = (_bias_rows(rel_self, tab_rows) - far) * LOG2E

    hd = width // H_B
    q_rows = q.reshape(bd, half, hd)
    q_rows = jnp.concatenate([q_rows, q_rows], axis=1)
    kn = k_new.reshape(bd, n_self, hd)
    vn = v_new.reshape(bd, n_self, hd)

    def page_spec(c):
        return pl.BlockSpec((1, page_cols, hd),
                            lambda b, j, pt: (page_base + pt[b * n_pages + j * pages + c], 0, 0))

    seq_spec = lambda r: pl.BlockSpec((1, r, hd), lambda b, j, pt: (b, 0, 0))
    whole = lambda shape: pl.BlockSpec(shape, lambda b, j, pt: (0,) * len(shape), pipeline_mode=pl.Buffered(1))
    grid_spec = pltpu.PrefetchScalarGridSpec(
        num_scalar_prefetch=1,
        grid=(bd, n_pages // pages),
        in_specs=[seq_spec(rows), seq_spec(n_self), seq_spec(n_self),
                  whole(bpast.shape), whole(bself.shape), whole(lam_params.shape), whole((1, DV_B))]
                 + [page_spec(c) for c in range(pages)] + [page_spec(c) for c in range(pages)],
        out_specs=seq_spec(half),
        scratch_shapes=[pltpu.VMEM((rows, hd), F32), pltpu.VMEM((rows, 1), F32),
                        pltpu.VMEM((rows, 1), F32), pltpu.VMEM((rows, DV_B), F32)])
    out = pl.pallas_call(
        functools.partial(_attn_sample_body, pages=pages, lam_init=lam_init),
        grid_spec=grid_spec,
        out_shape=jax.ShapeDtypeStruct((bd, half, DV_B), F32),
        compiler_params=_params(("parallel", "arbitrary")),
        name="attn_sample",
    )(page_table.reshape(-1), q_rows, kn, vn, bpast, bself, lam_params, subln.reshape(1, DV_B),
      *([cache_k] * pages), *([cache_v] * pages))
    return out.reshape(bd, l_new, width)


def _merge_ffn_body(x_ref, ya_ref, yb_ref, ga_ref, gb_ref, wa_ref, wb_ref, wo_ref, post_ref,
                    pre2_ref, wg_ref, wu_ref, wd_ref, post2_ref, o_ref, *, f_chunk):
    merged = (jax.nn.sigmoid(ga_ref[...].astype(F32)) * _dot(ya_ref[...], wa_ref[...])
              + jax.nn.sigmoid(gb_ref[...].astype(F32)) * _dot(yb_ref[...], wb_ref[...]))
    x = x_ref[...] + _rms(_dot(merged.astype(BF16), wo_ref[...]), post_ref[...])
    o_ref[...] = _ffn_half_step(x, pre2_ref, wg_ref, wu_ref, wd_ref, post2_ref, f_chunk)


def _merge_ffn(x, ya, yb, ga, gb, wa, wb, wo, post, pre2, wg, wu, wd, post2, *, tm):
    t, d = x.shape
    d_ff = wg.shape[1]
    f_chunk = FFN_CHUNK if d_ff % FFN_CHUNK == 0 else d_ff
    row = lambda w: pl.BlockSpec((tm, w), lambda i: (i, 0))
    weights = (wa, wb, wo, post, pre2, wg, wu, wd, post2)
    return pl.pallas_call(
        functools.partial(_merge_ffn_body, f_chunk=f_chunk),
        grid=(t // tm,),
        in_specs=[row(d), row(ya.shape[1]), row(yb.shape[1]), row(d), row(d)]
                 + [_const_spec(w.shape) for w in weights],
        out_specs=row(d),
        out_shape=jax.ShapeDtypeStruct((t, d), F32),
        compiler_params=_params(("parallel",)),
        name="merge_ffn",
    )(x, ya, yb, ga, gb, *weights)


def _lambda_init(layer):
    return 0.8 - 0.6 * math.exp(-0.3 * layer)


def _token_tile(t):
    tm = TOKEN_TILE
    while t % tm:
        tm //= 2
    return tm


def _attn_tile(seq):
    return min(ATTN_BLOCK, seq)


def _layer_weights(l, W):
    d = W["w_in"].shape[1]
    bf = lambda a: a.astype(BF16)
    row = lambda a: a.reshape(1, -1).astype(F32)
    widths = {"qkv": H_A * (2 * DK_A + DV_A), "z": H_A * DV_A, "beta": H_A, "alpha": H_A,
              "q": H_B * 2 * DQK_B, "k": H_B * 2 * DQK_B, "v": H_B * DV_B, "ga": d, "gb": d}
    order = ("qkv", "z", "beta", "alpha", "q", "k", "v", "ga", "gb")
    offs, off = {}, 0
    for n in order:
        offs[n] = off
        off += widths[n]
    w_in = W["w_in"][l]
    col = lambda n: w_in[:, offs[n]:offs[n] + widths[n]]
    pad = jnp.zeros((d, LANES - 2 * H_A), w_in.dtype)
    w_r = jnp.concatenate([col("qkv"), col("z"), col("q"), col("k"), col("v"), col("ga"), col("gb"),
                           col("beta"), col("alpha"), pad], axis=1)
    groups = tuple((n, (d if w is None else w), dt) for n, w, dt in _PROJ_OUT)
    lam_params = jnp.stack([W["lam_q1"][l], W["lam_k1"][l], W["lam_q2"][l], W["lam_k2"][l]]).astype(F32)
    return dict(
        ffn1=(row(W["ffn1_pre"][l]), bf(W["ffn1_wg"][l]), bf(W["ffn1_wu"][l]), bf(W["ffn1_wd"][l]),
              row(W["ffn1_post"][l])),
        ffn2=(row(W["ffn2_pre"][l]), bf(W["ffn2_wg"][l]), bf(W["ffn2_wu"][l]), bf(W["ffn2_wd"][l]),
              row(W["ffn2_post"][l])),
        mix_pre=row(W["mix_pre"][l]), w_r=bf(w_r), groups=groups,
        conv_w=W["conv_w"][l].astype(F32), a_log=W["a_log"][l].astype(F32), dt_bias=W["dt_bias"][l].astype(F32),
        gdn_norm=W["gdn_norm"][l].astype(F32), lam_params=lam_params, subln=W["subln"][l].astype(F32),
        w_a=bf(W["w_a"][l]), w_b=bf(W["w_b"][l]), w_out=bf(W["w_out"][l]), mix_post=row(W["mix_post"][l]),
        lam_init=_lambda_init(l))


def _decoder_layer(x, lw, conv_state, ssm_state, attend, attn_tile=None):
    b, l, d = x.shape
    t = b * l
    tm = attn_tile or _token_tile(t)
    x2 = _ffn(x.reshape(t, d), *lw["ffn1"], tm=tm)
    c = _proj(x2, lw["mix_pre"], lw["w_r"], lw["groups"], tm=tm, attn_operands=attn_tile is not None)
    c3 = {n: c[n].reshape(b, l, c[n].shape[-1]) for n in ("qkv", "z", "ba")}

    chunk = min(CHUNK, l)
    if chunk % SUBLANES:
        chunk = -(-chunk // SUBLANES) * SUBLANES
    lp = -(-l // chunk) * chunk
    padl = lambda a: jnp.pad(a, ((0, 0), (0, lp - l), (0, 0))) if lp != l else a
    ya, ssm_new = _gdn(padl(c3["qkv"]), conv_state, padl(c3["z"]), padl(c3["ba"]), ssm_state,
                       lw["conv_w"], lw["a_log"], lw["dt_bias"], lw["gdn_norm"],
                       chunk=chunk, n_chunk=math.gcd(lp // chunk, GDN_CHUNKS_PER_STEP),
                       n_seq=math.gcd(b, GDN_SEQS_PER_STEP) if lp == chunk else 1,
                       l_valid=min(l - (lp - chunk), chunk))
    ya = ya[:, :l]
    if l >= CONV_W - 1:
        conv_new = c3["qkv"][:, l - (CONV_W - 1):]
    else:
        conv_new = jnp.concatenate([conv_state.astype(F32), c3["qkv"]], axis=1)[:, -(CONV_W - 1):]

    yb = attend(c, b, l)
    y = _merge_ffn(x2, ya.reshape(t, -1), yb.reshape(t, -1).astype(BF16), c["ga"], c["gb"],
                   lw["w_a"], lw["w_b"], lw["w_out"], lw["mix_post"], *lw["ffn2"], tm=tm)
    return (y.reshape(b, l, d), c["k"].reshape(b, l, H_B, 2 * DQK_B), c["v"].reshape(b, l, H_B, DV_B),
            conv_new, ssm_new)


def kernel(x_prompt, x_sample, cache_k, cache_v, state_conv, state_ssm, page_table, rel_table,
           ffn1_pre, ffn1_wg, ffn1_wu, ffn1_wd, ffn1_post, mix_pre, w_in, conv_w, a_log, dt_bias,
           gdn_norm, lam_q1, lam_k1, lam_q2, lam_k2, subln, w_a, w_b, w_out, mix_post,
           ffn2_pre, ffn2_wg, ffn2_wu, ffn2_wd, ffn2_post):
    W = dict(ffn1_pre=ffn1_pre, ffn1_wg=ffn1_wg, ffn1_wu=ffn1_wu, ffn1_wd=ffn1_wd, ffn1_post=ffn1_post,
             mix_pre=mix_pre, w_in=w_in, conv_w=conv_w, a_log=a_log, dt_bias=dt_bias, gdn_norm=gdn_norm,
             lam_q1=lam_q1, lam_k1=lam_k1, lam_q2=lam_q2, lam_k2=lam_k2, subln=subln, w_a=w_a, w_b=w_b,
             w_out=w_out, mix_post=mix_post, ffn2_pre=ffn2_pre, ffn2_wg=ffn2_wg, ffn2_wu=ffn2_wu,
             ffn2_wd=ffn2_wd, ffn2_post=ffn2_post)
    depth = w_in.shape[0]
    bp, seq, _ = x_prompt.shape
    n_pool = cache_k.shape[1]
    n_pages = page_table.shape[1]
    t_attn = _attn_tile(seq)
    pages = math.gcd(n_pages, SAMPLE_PAGES_PER_STEP)
    ck = cache_k.reshape(depth * n_pool, PAGE_SIZE * H_B, -1)
    cv = cache_v.reshape(depth * n_pool, PAGE_SIZE * H_B, -1)
    xp, xs = x_prompt, x_sample
    outs = [[] for _ in range(8)]
    for l in range(depth):
        lw = _layer_weights(l, W)

        def attend_prompt(c, b, n):
            return _attn_prompt(c["q"].reshape(b, n, -1), c["k16"].reshape(b, n, -1), c["vt"], rel_table,
                                lw["lam_params"], lw["subln"], t=t_attn, lam_init=lw["lam_init"])

        def attend_sample(c, b, n):
            return _attn_sample(c["q"].reshape(b, n, -1), c["k"], c["v"], ck, cv, l * n_pool, page_table,
                                rel_table, lw["lam_params"], lw["subln"], pages=pages, lam_init=lw["lam_init"])

        zero_conv = jnp.zeros((bp, CONV_W - 1, state_conv.shape[-1]), x_prompt.dtype)
        zero_ssm = jnp.zeros((bp,) + state_ssm.shape[2:], state_ssm.dtype)
        xp, k1, v1, c1, s1 = _decoder_layer(xp, lw, zero_conv, zero_ssm, attend_prompt, attn_tile=t_attn)
        xs, k2, v2, c2, s2 = _decoder_layer(xs, lw, state_conv[l], state_ssm[l], attend_sample)
        for o, a in zip(outs, (k1, v1, c1, s1, k2, v2, c2, s2)):
            o.append(a)
    return (xp, xs) + tuple(jnp.stack(o) for o in outs)
```
